```python
import math
import jax, jax.numpy as jnp
from jax import lax
import numpy as np

D_MODEL = 2048
BATCH = 2
SEQ = 4096
DEPTH = 1
DEC_BATCH = 1
DEC_SEQ = 8192
PAST_LEN = 128

N_MEM = 256
SSM_WIDTH = D_MODEL // 2
SSM_GROUP = 16
SSM_GROUPS = SSM_WIDTH // SSM_GROUP
SSM_STATE = 64
GMLP_WIDTH = D_MODEL // 2
GMLP_CHUNK = 128
GMLP_HEADS = 8
GMLP_HEAD_DIM = GMLP_WIDTH // GMLP_HEADS
ATTN_HEADS = 4
ATTN_HEAD_DIM = D_MODEL // 8
ATTN_WIDTH = ATTN_HEADS * ATTN_HEAD_DIM
N_BRANCH = 3
D_FF = 5504
OFF_GU = SSM_WIDTH
OFF_GV = OFF_GU + GMLP_WIDTH
OFF_Q = OFF_GV + GMLP_WIDTH
OFF_GATE = OFF_Q + ATTN_WIDTH
IN_WIDTH = OFF_GATE + N_BRANCH * D_MODEL
ALPHA = (2.0 * DEPTH) ** 0.25
BETA = (8.0 * DEPTH) ** -0.25
LN_EPS = 1e-5

kernel_name = "hybrid_s5_gmlp_memxattn_encoder"


def _layernorm(x, g, b):
    xf = x.astype(jnp.float32)
    mu = jnp.mean(xf, axis=-1, keepdims=True)
    var = jnp.mean(jnp.square(xf - mu), axis=-1, keepdims=True)
    y = (xf - mu) * lax.rsqrt(var + LN_EPS) * g.astype(jnp.float32) + b.astype(jnp.float32)
    return y.astype(x.dtype)


def _swiglu(x, w_gu, w_down):
    gate, up = jnp.split(x @ w_gu, 2, axis=-1)
    return (jax.nn.silu(gate) * up) @ w_down


def _scan_combine(e1, e2):
    a1, b1 = e1
    a2, b2 = e2
    return a1 * a2, a2 * b1 + b2


def _s5_mixer(u, a_re, a_im, log_dt, b_re, b_im, c_re, c_im, d_skip):
    f32 = jnp.float32
    bsz, seqlen, _ = u.shape
    uf = u.astype(f32).reshape(bsz, seqlen, SSM_GROUPS, SSM_GROUP)
    uc = uf.astype(jnp.complex64)
    lam = lax.complex(a_re.astype(f32), a_im.astype(f32))
    dt = jnp.exp(log_dt.astype(f32))[..., None]
    lam_bar = jnp.exp(lam * dt)
    b = lax.complex(b_re.astype(f32), b_im.astype(f32))
    b_bar = ((lam_bar - 1.0) / lam)[..., None] * b[None]
    c = lax.complex(c_re.astype(f32), c_im.astype(f32))
    y = d_skip.astype(f32) * uf
    for direction, rev in ((0, False), (1, True)):
        bu = jnp.einsum('blgc,gpc->blgp', uc, b_bar[direction])
        a = jnp.broadcast_to(lam_bar[direction], bu.shape)
        _, states = lax.associative_scan(_scan_combine, (a, bu), reverse=rev, axis=1)
        y = y + jnp.real(jnp.einsum('gcp,blgp->blgc', c[direction], states))
    return y.reshape(bsz, seqlen, SSM_WIDTH).astype(u.dtype)


def _gmlp_mixer(u, v, ln_g, ln_b, w_s, b_s):
    bsz, seqlen, _ = u.shape
    n_chunks = seqlen // GMLP_CHUNK
    u = jax.nn.gelu(u)
    v = _layernorm(jax.nn.gelu(v), ln_g, ln_b)
    vc = v.reshape(bsz, n_chunks, GMLP_CHUNK, GMLP_HEADS, GMLP_HEAD_DIM)
    mixed = jnp.einsum('hts,bnshd->bnthd', w_s, vc) + b_s[:, :, None]
    return u * mixed.reshape(bsz, seqlen, GMLP_WIDTH)


def _memory_cross_attention(q, mem, ln_g, ln_b, w_kv):
    bsz, seqlen, _ = q.shape
    m = _layernorm(mem, ln_g, ln_b)
    k, v = jnp.split(m @ w_kv, 2, axis=-1)
    q = q.reshape(bsz, seqlen, ATTN_HEADS, ATTN_HEAD_DIM)
    k = k.reshape(bsz, N_MEM, ATTN_HEADS, ATTN_HEAD_DIM)
    v = v.reshape(bsz, N_MEM, ATTN_HEADS, ATTN_HEAD_DIM)
    s = jnp.einsum('bqhd,bkhd->bhqk', q, k).astype(jnp.float32) * (ATTN_HEAD_DIM ** -0.5)
    p = jax.nn.softmax(s, axis=-1).astype(v.dtype)
    o = jnp.einsum('bhqk,bkhd->bqhd', p, v)
    return o.reshape(bsz, seqlen, ATTN_WIDTH)


def _layer(x, mem, ffn1_w_gu, ffn1_w_down, ln1_g, ln1_b, w_in,
           ssm_a_re, ssm_a_im, ssm_log_dt, ssm_b_re, ssm_b_im, ssm_c_re, ssm_c_im, ssm_d, ssm_w_glu,
           gmlp_ln_g, gmlp_ln_b, gmlp_w_s, gmlp_b_s, gmlp_w_proj,
           mem_ln_g, mem_ln_b, attn_w_kv, attn_w_proj,
           w_out, ln2_g, ln2_b, ffn2_w_gu, ffn2_w_down, ln3_g, ln3_b):
    bsz, seqlen, _ = x.shape
    x = _layernorm(ALPHA * x + 0.5 * _swiglu(x, ffn1_w_gu, ffn1_w_down), ln1_g, ln1_b)
    proj = x @ w_in
    u_ssm, u_g, v_g, q, gates = jnp.split(proj, [OFF_GU, OFF_GV, OFF_Q, OFF_GATE], axis=-1)
    y_a = jax.nn.gelu(_s5_mixer(u_ssm, ssm_a_re, ssm_a_im, ssm_log_dt, ssm_b_re, ssm_b_im,
                                ssm_c_re, ssm_c_im, ssm_d))
    glu_a, glu_b = jnp.split(y_a @ ssm_w_glu, 2, axis=-1)
    y_a = glu_a * jax.nn.sigmoid(glu_b)
    y_b = _gmlp_mixer(u_g, v_g, gmlp_ln_g, gmlp_ln_b, gmlp_w_s, gmlp_b_s) @ gmlp_w_proj
    y_c = _memory_cross_attention(q, mem, mem_ln_g, mem_ln_b, attn_w_kv) @ attn_w_proj
    g = jax.nn.sigmoid(gates.astype(jnp.float32)).astype(x.dtype).reshape(bsz, seqlen, N_BRANCH, D_MODEL)
    merged = g[:, :, 0] * y_a + g[:, :, 1] * y_b + g[:, :, 2] * y_c
    x = _layernorm(ALPHA * x + merged @ w_out, ln2_g, ln2_b)
    x = _layernorm(ALPHA * x + 0.5 * _swiglu(x, ffn2_w_gu, ffn2_w_down), ln3_g, ln3_b)
    return x


def setup_inputs(seed: int = 0) -> dict:
    key = jax.random.key(seed)
    ks = iter(jax.random.split(key, 40))
    f32 = jnp.float32

    def nrm(shape, scale):
        return jax.random.normal(next(ks), shape, f32) * scale

    def gain(shape):
        return 1.0 + nrm(shape, 0.01)

    L = DEPTH
    D = D_MODEL
    G, P, C = SSM_GROUPS, SSM_STATE, SSM_GROUP
    n_idx = jnp.arange(P, dtype=f32)
    return {
        "x_prompt": nrm((BATCH, SEQ, D), 1.0),
        "x_sample": nrm((DEC_BATCH, DEC_SEQ, D), 1.0),
        "mem_prompt": nrm((BATCH, N_MEM, D), 1.0),
        "mem_sample": nrm((DEC_BATCH, N_MEM, D), 1.0),
        "ffn1_w_gu": nrm((L, D, 2 * D_FF), D ** -0.5),
        "ffn1_w_down": nrm((L, D_FF, D), BETA * D_FF ** -0.5),
        "ln1_g": gain((L, D)),
        "ln1_b": nrm((L, D), 0.01),
        "w_in": nrm((L, D, IN_WIDTH), D ** -0.5),
        "ssm_a_re": -0.5 + nrm((L, 2, G, P), 0.01),
        "ssm_a_im": math.pi * n_idx + nrm((L, 2, G, P), 0.01),
        "ssm_log_dt": jax.random.uniform(next(ks), (L, 2, G), f32,
                                         minval=math.log(1e-3), maxval=math.log(1e-1)),
        "ssm_b_re": nrm((L, G, P, C), (2.0 * C) ** -0.5),
        "ssm_b_im": nrm((L, G, P, C), (2.0 * C) ** -0.5),
        "ssm_c_re": nrm((L, 2, G, C, P), (2.0 * P) ** -0.5),
        "ssm_c_im": nrm((L, 2, G, C, P), (2.0 * P) ** -0.5),
        "ssm_d": nrm((L, G, C), 1.0),
        "ssm_w_glu": nrm((L, SSM_WIDTH, 2 * D), SSM_WIDTH ** -0.5),
        "gmlp_ln_g": gain((L, GMLP_WIDTH)),
        "gmlp_ln_b": nrm((L, GMLP_WIDTH), 0.01),
        "gmlp_w_s": nrm((L, GMLP_HEADS, GMLP_CHUNK, GMLP_CHUNK), GMLP_CHUNK ** -0.5),
        "gmlp_b_s": gain((L, GMLP_CHUNK, GMLP_HEADS)),
        "gmlp_w_proj": nrm((L, GMLP_WIDTH, D), GMLP_WIDTH ** -0.5),
        "mem_ln_g": gain((L, D)),
        "mem_ln_b": nrm((L, D), 0.01),
        "attn_w_kv": nrm((L, D, 2 * ATTN_WIDTH), D ** -0.5),
        "attn_w_proj": nrm((L, ATTN_WIDTH, D), ATTN_WIDTH ** -0.5),
        "w_out": nrm((L, D, D), BETA * D ** -0.5),
        "ln2_g": gain((L, D)),
        "ln2_b": nrm((L, D), 0.01),
        "ffn2_w_gu": nrm((L, D, 2 * D_FF), D ** -0.5),
        "ffn2_w_down": nrm((L, D_FF, D), BETA * D_FF ** -0.5),
        "ln3_g": gain((L, D)),
        "ln3_b": nrm((L, D), 0.01),
    }


def reference(x_prompt, x_sample, mem_prompt, mem_sample,
              ffn1_w_gu, ffn1_w_down, ln1_g, ln1_b, w_in,
              ssm_a_re, ssm_a_im, ssm_log_dt, ssm_b_re, ssm_b_im, ssm_c_re, ssm_c_im, ssm_d, ssm_w_glu,
              gmlp_ln_g, gmlp_ln_b, gmlp_w_s, gmlp_b_s, gmlp_w_proj,
              mem_ln_g, mem_ln_b, attn_w_kv, attn_w_proj,
              w_out, ln2_g, ln2_b, ffn2_w_gu, ffn2_w_down, ln3_g, ln3_b):
    def run(x, mem):
        for l in range(DEPTH):
            x = _layer(x, mem, ffn1_w_gu[l], ffn1_w_down[l], ln1_g[l], ln1_b[l], w_in[l],
                       ssm_a_re[l], ssm_a_im[l], ssm_log_dt[l], ssm_b_re[l], ssm_b_im[l],
                       ssm_c_re[l], ssm_c_im[l], ssm_d[l], ssm_w_glu[l],
                       gmlp_ln_g[l], gmlp_ln_b[l], gmlp_w_s[l], gmlp_b_s[l], gmlp_w_proj[l],
                       mem_ln_g[l], mem_ln_b[l], attn_w_kv[l], attn_w_proj[l],
                       w_out[l], ln2_g[l], ln2_b[l], ffn2_w_gu[l], ffn2_w_down[l], ln3_g[l], ln3_b[l])
        return x

    y_prompt = run(x_prompt, mem_prompt)
    y_sample = run(x_sample, mem_sample)
    return (y_prompt, y_sample)
```

```python
import functools
import math

import jax
import jax.numpy as jnp
from jax import lax
from jax.experimental import pallas as pl
from jax.experimental.pallas import tpu as pltpu

F32 = jnp.float32
BF16 = jnp.bfloat16

D_MODEL = 2048
DEPTH = 1
SEQ_BLOCK = 4096
N_MEM = 256
SSM_WIDTH = D_MODEL // 2
SSM_GROUP = 16
SSM_GROUPS = SSM_WIDTH // SSM_GROUP
SSM_STATE = 64
GMLP_WIDTH = D_MODEL // 2
GMLP_CHUNK = 128
GMLP_HEADS = 8
GMLP_HEAD_DIM = GMLP_WIDTH // GMLP_HEADS
ATTN_HEADS = 4
ATTN_HEAD_DIM = D_MODEL // 8
ATTN_WIDTH = ATTN_HEADS * ATTN_HEAD_DIM
D_FF = 5504
OFF_GU = SSM_WIDTH
OFF_GV = OFF_GU + GMLP_WIDTH
OFF_Q = OFF_GV + GMLP_WIDTH
OFF_GATE = OFF_Q + ATTN_WIDTH
IN_WIDTH = OFF_GATE + 3 * D_MODEL
ALPHA = (2.0 * DEPTH) ** 0.25
LN_EPS = 1e-5

V7X_LANES = 128
V7X_VMEM_BYTES = 64 * 1024 * 1024

SSM_CHUNK = 32
SSM_GROUPS_PER_STEP = 8
FFN_TM = 512
FFN_TF = 512
D_FF_PAD = ((D_FF + FFN_TF - 1) // FFN_TF) * FFN_TF
PROJ_TM = 1024
PROJ_TN = 1024
MIX_TM = 512
MIX_TN = 512
OUT_TM = 512
KV_TN = 512


def _vmem_limit(nbytes):
    return int(min(nbytes + (16 << 20), V7X_VMEM_BYTES - (4 << 20)))


def _layernorm_rows(y, g, b):
    mu = jnp.mean(y, axis=-1, keepdims=True)
    yc = y - mu
    var = jnp.mean(yc * yc, axis=-1, keepdims=True)
    return yc * lax.rsqrt(var + LN_EPS) * g + b


def _gelu_tanh(x):
    c = math.sqrt(2.0 / math.pi)
    return 0.5 * x * (1.0 + jnp.tanh(c * (x + 0.044715 * (x * x * x))))


def _sigmoid(x):
    return 1.0 / (1.0 + jnp.exp(-x))


def _dot(a, b):
    return jnp.dot(a, b, preferred_element_type=F32)


def _dot_nt(a, b):
    return lax.dot_general(a, b, (((1,), (1,)), ((), ())), preferred_element_type=F32)


def _ffn_ln_kernel(x_ref, wgu_ref, wd_ref, g_ref, b_ref, *refs, tf, with_bf16):
    if with_bf16:
        o_ref, ob_ref, xb_ref = refs
    else:
        o_ref, xb_ref = refs
        ob_ref = None
    j = pl.program_id(1)

    @pl.when(j == 0)
    def _():
        xb_ref[...] = x_ref[...].astype(BF16)
        o_ref[...] = jnp.zeros_like(o_ref)

    h = _dot(xb_ref[...], wgu_ref[...])
    gate = h[:, :tf]
    up = h[:, tf:]
    act = (gate * _sigmoid(gate) * up).astype(BF16)
    o_ref[...] += _dot(act, wd_ref[...])

    @pl.when(j == pl.num_programs(1) - 1)
    def _():
        y = ALPHA * x_ref[...] + 0.5 * o_ref[...]
        out = _layernorm_rows(y, g_ref[...], b_ref[...])
        o_ref[...] = out
        if with_bf16:
            ob_ref[...] = out.astype(BF16)


def _ffn_ln(x, wgu, wd, ln_g, ln_b, *, with_bf16):
    n, d = x.shape
    tm, tf = FFN_TM, FFN_TF
    nf = wd.shape[0] // tf
    out_shape = [jax.ShapeDtypeStruct((n, d), F32)]
    out_specs = [pl.BlockSpec((tm, d), lambda i, j: (i, 0))]
    if with_bf16:
        out_shape.append(jax.ShapeDtypeStruct((n, d), BF16))
        out_specs.append(pl.BlockSpec((tm, d), lambda i, j: (i, 0)))
    est = (2 * tm * d * 4 + 2 * tm * d * 4 + 2 * tm * d * 2 + tm * d * 2
           + 2 * (d * 2 * tf * 2) + 2 * (tf * d * 2) + 3 * tm * 2 * tf * 4)
    res = pl.pallas_call(
        functools.partial(_ffn_ln_kernel, tf=tf, with_bf16=with_bf16),
        grid=(n // tm, nf),
        in_specs=[
            pl.BlockSpec((tm, d), lambda i, j: (i, 0)),
            pl.BlockSpec((d, 2 * tf), lambda i, j: (0, j)),
            pl.BlockSpec((tf, d), lambda i, j: (j, 0)),
            pl.BlockSpec((1, d), lambda i, j: (0, 0)),
            pl.BlockSpec((1, d), lambda i, j: (0, 0)),
        ],
        out_specs=out_specs,
        out_shape=out_shape,
        scratch_shapes=[pltpu.VMEM((tm, d), BF16)],
        compiler_params=pltpu.CompilerParams(
            dimension_semantics=("parallel", "arbitrary"),
            vmem_limit_bytes=_vmem_limit(est)),
        name="ffn_ln",
    )(x, wgu, wd, ln_g, ln_b)
    return res if with_bf16 else res[0]


def _prep_ffn_weights(w_gu, w_down):
    d = w_gu.shape[0]
    pad = D_FF_PAD - D_FF
    nf = D_FF_PAD // FFN_TF
    wg = jnp.pad(w_gu[:, :D_FF], ((0, 0), (0, pad))).reshape(d, nf, 1, FFN_TF)
    wu = jnp.pad(w_gu[:, D_FF:], ((0, 0), (0, pad))).reshape(d, nf, 1, FFN_TF)
    wgu = jnp.concatenate([wg, wu], axis=2).reshape(d, nf * 2 * FFN_TF).astype(BF16)
    wd = jnp.pad(w_down, ((0, pad), (0, 0))).astype(BF16)
    return wgu, wd


def _matmul_kernel(x_ref, w_ref, o_ref):
    o_ref[...] = _dot(x_ref[...], w_ref[...]).astype(o_ref.dtype)


def _matmul(x, w, *, tm, tn, out_dtype):
    n, k = x.shape
    m = w.shape[1]
    est = 2 * tm * k * 2 + 2 * k * tn * 2 + 2 * tm * tn * 2 + tm * tn * 4
    return pl.pallas_call(
        _matmul_kernel,
        grid=(n // tm, m // tn),
        in_specs=[pl.BlockSpec((tm, k), lambda i, j: (i, 0)),
                  pl.BlockSpec((k, tn), lambda i, j: (0, j))],
        out_specs=pl.BlockSpec((tm, tn), lambda i, j: (i, j)),
        out_shape=jax.ShapeDtypeStruct((n, m), out_dtype),
        compiler_params=pltpu.CompilerParams(
            dimension_semantics=("parallel", "parallel"),
            vmem_limit_bytes=_vmem_limit(est)),
        name="in_proj",
    )(x, w)


def _cmul(x, y):
    return x[0] * y[0] - x[1] * y[1], x[0] * y[1] + x[1] * y[0]


def _split_hi_lo(x):
    hi = x.astype(BF16)
    lo = (x - hi.astype(F32)).astype(BF16)
    return hi, lo


def _dot_hi_lo(a, b):
    ah, al = _split_hi_lo(a)
    bh, bl = _split_hi_lo(b)
    return _dot(ah, bh) + _dot(al, bh) + _dot(ah, bl)


def _ssm_tables_kernel(pcol_ref, prow_ref, bcat_ref, x1_ref, x2_ref, ct_ref, dtile_ref,
                       mt_ref, gm_ref, cs_ref, q_ref, *, T):
    P, C = SSM_STATE, SSM_GROUP
    rpt = V7X_LANES // C
    nt = T // rpt
    wide = 2 * T * C

    lane = lax.broadcasted_iota(jnp.int32, (1, V7X_LANES), 1)
    rr = lax.shift_right_logical(lane, 4).astype(F32)
    lanef = lane.astype(F32)
    expand = (lax.broadcasted_iota(jnp.int32, (C, V7X_LANES), 0)
              == (lax.broadcasted_iota(jnp.int32, (C, V7X_LANES), 1) & (C - 1))).astype(F32)

    pc = pcol_ref[...]

    def powers(d, k):
        dt = jnp.exp(pc[:, 4 + d:5 + d])
        zr = pc[:, d:d + 1] * dt
        zi = pc[:, 2 + d:3 + d] * dt
        mag = jnp.exp(k * zr)
        ang = k * zi
        return mag * jnp.cos(ang), mag * jnp.sin(ang)

    def zoh(d, pt):
        are = pc[:, d:d + 1]
        aim = pc[:, 2 + d:3 + d]
        nr = pt[0][:, 1:2] - 1.0
        ni = pt[1][:, 1:2]
        den = are * are + aim * aim
        return (nr * are + ni * aim) / den, (ni * are - nr * aim) / den

    def col(pt, m):
        return pt[0][:, m:m + 1], pt[1][:, m:m + 1]

    asc_f, dsc_f, pt_f = powers(0, rr), powers(0, (rpt - 1.0) - rr), powers(0, lanef)
    asc_b, dsc_b, pt_b = powers(1, rr), powers(1, (rpt - 1.0) - rr), powers(1, lanef)
    w_f, w_b = zoh(0, pt_f), zoh(1, pt_b)

    ctile = jnp.dot(ct_ref[...], expand, precision=lax.Precision.HIGHEST,
                    preferred_element_type=F32)
    btile = jnp.dot(bcat_ref[...], expand, precision=lax.Precision.HIGHEST,
                    preferred_element_type=F32)
    ct_f = (ctile[0:P], ctile[P:2 * P])
    ct_b = (ctile[2 * P:3 * P], ctile[3 * P:4 * P])
    bt = (btile[0:P], btile[P:2 * P])

    ca_f = _cmul(asc_f, ct_f)
    cd_b = _cmul(dsc_b, ct_b)
    bd_f = _cmul(dsc_f, _cmul(w_f, bt))
    ba_b = _cmul(asc_b, _cmul(w_b, bt))

    for j in range(nt):
        cols = slice(j * V7X_LANES, (j + 1) * V7X_LANES)
        xf = _cmul(bd_f, col(pt_f, T - rpt - rpt * j))
        xb = _cmul(ba_b, col(pt_b, rpt * j))
        gm_ref[0:P, cols] = xf[0].astype(BF16)
        gm_ref[P:2 * P, cols] = xf[1].astype(BF16)
        gm_ref[2 * P:3 * P, cols] = xb[0].astype(BF16)
        gm_ref[3 * P:4 * P, cols] = xb[1].astype(BF16)
        yf = _cmul(ca_f, col(pt_f, rpt * j + 1))
        yb = _cmul(cd_b, col(pt_b, T - rpt * j - (rpt - 1)))
        cs_ref[0:P, cols] = yf[0].astype(BF16)
        cs_ref[P:2 * P, cols] = (-yf[1]).astype(BF16)
        cs_ref[2 * P:3 * P, cols] = yb[0].astype(BF16)
        cs_ref[3 * P:4 * P, cols] = (-yb[1]).astype(BF16)

    zeros = jnp.zeros((2 * P, V7X_LANES), F32)
    for j in range(2 * nt):
        cols = slice(j * V7X_LANES, (j + 1) * V7X_LANES)
        if j < nt:
            qb = _cmul(cd_b, col(pt_b, T - rpt * j - (rpt - 1)))
            q_ref[0:2 * P, cols] = zeros
            q_ref[2 * P:3 * P, cols] = qb[0]
            q_ref[3 * P:4 * P, cols] = qb[1]
        else:
            qf = _cmul(ca_f, col(pt_f, rpt * j - T))
            q_ref[0:P, cols] = qf[0]
            q_ref[P:2 * P, cols] = qf[1]
            if j == nt:
                center = rr == 0.0
                q_ref[2 * P:3 * P, cols] = jnp.where(center, ct_b[0], 0.0)
                q_ref[3 * P:4 * P, cols] = jnp.where(center, ct_b[1], 0.0)
            else:
                q_ref[2 * P:4 * P, cols] = zeros

    pr = prow_ref[...]
    are, aim = pr[0:1], pr[1:2]
    dt = jnp.exp(pr[2:3])
    zr, zi = are * dt, aim * dt
    mag = jnp.exp(zr)
    nr, ni = mag * jnp.cos(zi) - 1.0, mag * jnp.sin(zi)
    den = are * are + aim * aim
    wr, wi = (nr * are + ni * aim) / den, (ni * are - nr * aim) / den
    lane4 = lax.broadcasted_iota(jnp.int32, (1, 4 * P), 1)
    sgn = jnp.where((lax.shift_right_logical(lane4, 6) & 1) == 0, 1.0, -1.0)
    lhs = (wr * sgn) * x1_ref[...] - wi * x2_ref[...]

    zt = _dot_hi_lo(lhs, q_ref[...])
    lanew = lax.broadcasted_iota(jnp.int32, (C, wide), 1)
    roww = lax.broadcasted_iota(jnp.int32, (C, wide), 0)
    diag = (lax.shift_right_logical(lanew, 4) == T) & ((lanew & (C - 1)) == roww)
    zt = zt + jnp.where(diag, dtile_ref[...], 0.0)

    for rp in range(T):
        off = (T - rp) * C
        shifted = pltpu.roll(zt, (wide - off) % wide, 1)
        mt_ref[rp * C:(rp + 1) * C, :] = shifted[:, :T * C].astype(BF16)


def _ssm_tables(a_re, a_im, log_dt, b_re, b_im, c_re, c_im, d_skip, *, T):
    G, P, C = SSM_GROUPS, SSM_STATE, SSM_GROUP
    ldt = jnp.broadcast_to(log_dt[:, :, None], (2, G, P))
    zc = jnp.zeros((G, P), F32)
    pcol = jnp.stack([a_re[0], a_re[1], a_im[0], a_im[1], ldt[0], ldt[1], zc, zc], axis=-1)

    def row4(x):
        return jnp.concatenate([x[0], x[0], x[1], x[1]], axis=-1)

    zr4 = jnp.zeros((G, 4 * P), F32)
    prow = jnp.stack([row4(a_re), row4(a_im), row4(ldt), zr4, zr4, zr4, zr4, zr4], axis=1)
    btr = jnp.swapaxes(b_re, 1, 2)
    bti = jnp.swapaxes(b_im, 1, 2)
    x1 = jnp.concatenate([btr, bti, btr, bti], axis=-1)
    x2 = jnp.concatenate([bti, btr, bti, btr], axis=-1)
    bcat = jnp.concatenate([b_re, b_im], axis=1)
    ct = jnp.concatenate([jnp.swapaxes(c_re[0], 1, 2), jnp.swapaxes(c_im[0], 1, 2),
                          jnp.swapaxes(c_re[1], 1, 2), jnp.swapaxes(c_im[1], 1, 2)], axis=1)
    dtile = jnp.tile(d_skip, (1, 2 * T))[:, None, :]

    tc = T * C
    return pl.pallas_call(
        functools.partial(_ssm_tables_kernel, T=T),
        grid=(G,),
        in_specs=[
            pl.BlockSpec((None, P, 8), lambda g: (g, 0, 0)),
            pl.BlockSpec((None, 8, 4 * P), lambda g: (g, 0, 0)),
            pl.BlockSpec((None, 2 * P, C), lambda g: (g, 0, 0)),
            pl.BlockSpec((None, C, 4 * P), lambda g: (g, 0, 0)),
            pl.BlockSpec((None, C, 4 * P), lambda g: (g, 0, 0)),
            pl.BlockSpec((None, 4 * P, C), lambda g: (g, 0, 0)),
            pl.BlockSpec((None, 1, 2 * tc), lambda g: (g, 0, 0)),
        ],
        out_specs=[
            pl.BlockSpec((None, tc, tc), lambda g: (g, 0, 0)),
            pl.BlockSpec((None, 4 * P, tc), lambda g: (g, 0, 0)),
            pl.BlockSpec((None, 4 * P, tc), lambda g: (g, 0, 0)),
        ],
        out_shape=[
            jax.ShapeDtypeStruct((G, tc, tc), BF16),
            jax.ShapeDtypeStruct((G, 4 * P, tc), BF16),
            jax.ShapeDtypeStruct((G, 4 * P, tc), BF16),
        ],
        scratch_shapes=[pltpu.VMEM((4 * P, 2 * tc), F32)],
        compiler_params=pltpu.CompilerParams(dimension_semantics=("parallel",)),
        name="ssm_tables",
    )(pcol, prow, bcat, x1, x2, ct, dtile)


def _ssm_state_kernel(u_ref, gm_ref, o_ref, *, gb, ns):
    for gi in range(gb):
        o_ref[:, gi * ns:(gi + 1) * ns] = _dot_nt(u_ref[gi], gm_ref[gi])


def _ssm_state(u, gm):
    G, nc, tc = u.shape
    ns = gm.shape[1]
    gb = SSM_GROUPS_PER_STEP
    return pl.pallas_call(
        functools.partial(_ssm_state_kernel, gb=gb, ns=ns),
        grid=(G // gb,),
        in_specs=[pl.BlockSpec((gb, nc, tc), lambda s: (s, 0, 0)),
                  pl.BlockSpec((gb, ns, tc), lambda s: (s, 0, 0))],
        out_specs=pl.BlockSpec((nc, gb * ns), lambda s: (0, s)),
        out_shape=jax.ShapeDtypeStruct((nc, G * ns), F32),
        compiler_params=pltpu.CompilerParams(dimension_semantics=("parallel",)),
        name="ssm_state",
    )(u, gm)


def _ssm_scan_kernel(are_ref, aim_ref, ldt_ref, s_ref, o_ref, x_ref, xs_ref, *, T, n0, seq_starts, seq_ends):
    d = pl.program_id(0)
    j = pl.program_id(1)
    nblk = pl.num_programs(1)
    blk = j + d * (nblk - 1 - 2 * j)
    P = SSM_STATE

    is_start = functools.reduce(jnp.logical_or, [blk == s for s in seq_starts])
    is_end = functools.reduce(jnp.logical_or, [blk == e for e in seq_ends])
    reset = jnp.where(d == 0, is_start, is_end)

    @pl.when(reset)
    def _():
        x_ref[...] = jnp.zeros_like(x_ref)
        xs_ref[...] = jnp.zeros_like(xs_ref)

    dt = jnp.exp(ldt_ref[...])
    zr = are_ref[...] * dt
    zi = aim_ref[...] * dt
    mag = jnp.exp(float(T) * zr)
    mr = mag * jnp.cos(float(T) * zi)
    mi = mag * jnp.sin(float(T) * zi)
    lane = lax.broadcasted_iota(jnp.int32, mr.shape, 1)
    m2 = jnp.where(lane < P, -mi, mi)
    m2s = -m2

    def body(k, carry):
        x, xs = carry
        row = k + d * (n0 - 1 - 2 * k)
        loc = s_ref[row]
        o_ref[row] = x.astype(o_ref.dtype)
        locs = pltpu.roll(loc, P, 1)
        return x * mr + xs * m2 + loc, xs * mr + x * m2s + locs

    x, xs = lax.fori_loop(0, n0, body, (x_ref[...], xs_ref[...]), unroll=4)
    x_ref[...] = x
    xs_ref[...] = xs


def _ssm_scan(s_loc, a_re, a_im, log_dt, *, T, n_tok):
    G, P = SSM_GROUPS, SSM_STATE
    nc = s_loc.shape[0]
    n0 = SEQ_BLOCK // T
    nblk = nc // n0
    seq_starts = (0, 1, 2)
    seq_ends = (0, 1, nblk - 1)
    dup = lambda x: jnp.concatenate([x, x], axis=-1)
    are2, aim2 = dup(a_re), dup(a_im)
    ldt2 = jnp.broadcast_to(log_dt[:, :, None], (2, G, 2 * P))

    def blk_map(d, j):
        return (j + d * (nblk - 1 - 2 * j), 0, d)

    par = pl.BlockSpec((None, G, 2 * P), lambda d, j: (d, 0, 0))
    return pl.pallas_call(
        functools.partial(_ssm_scan_kernel, T=T, n0=n0, seq_starts=seq_starts, seq_ends=seq_ends),
        grid=(2, nblk),
        in_specs=[par, par, par, pl.BlockSpec((n0, G, 2 * P), blk_map)],
        out_specs=pl.BlockSpec((n0, G, 2 * P), blk_map),
        out_shape=jax.ShapeDtypeStruct((nc, G, 4 * P), BF16),
        scratch_shapes=[pltpu.VMEM((G, 2 * P), F32), pltpu.VMEM((G, 2 * P), F32)],
        compiler_params=pltpu.CompilerParams(dimension_semantics=("arbitrary", "arbitrary")),
        name="ssm_scan",
    )(are2, aim2, ldt2, s_loc)


def _ssm_out_kernel(u_ref, mt_ref, s_ref, cs_ref, y_ref, *, gb, ns):
    for gi in range(gb):
        y = _dot(u_ref[gi], mt_ref[gi]) + _dot(s_ref[:, gi * ns:(gi + 1) * ns], cs_ref[gi])
        y_ref[gi] = _gelu_tanh(y).astype(y_ref.dtype)


def _ssm_out(u, mt, s_in, cs):
    G, nc, tc = u.shape
    ns = cs.shape[1]
    gb = SSM_GROUPS_PER_STEP
    return pl.pallas_call(
        functools.partial(_ssm_out_kernel, gb=gb, ns=ns),
        grid=(G // gb,),
        in_specs=[pl.BlockSpec((gb, nc, tc), lambda s: (s, 0, 0)),
                  pl.BlockSpec((gb, tc, tc), lambda s: (s, 0, 0)),
                  pl.BlockSpec((nc, gb * ns), lambda s: (0, s)),
                  pl.BlockSpec((gb, ns, tc), lambda s: (s, 0, 0))],
        out_specs=pl.BlockSpec((gb, nc, tc), lambda s: (s, 0, 0)),
        out_shape=jax.ShapeDtypeStruct((G, nc, tc), BF16),
        compiler_params=pltpu.CompilerParams(dimension_semantics=("parallel",)),
        name="ssm_out",
    )(u, mt, s_in, cs)


def _s5_mixer_gelu(u_tok, a_re, a_im, log_dt, b_re, b_im, c_re, c_im, d_skip):
    n = u_tok.shape[0]
    T, G, C = SSM_CHUNK, SSM_GROUPS, SSM_GROUP
    nc = n // T
    mt, gm, cs = _ssm_tables(a_re, a_im, log_dt, b_re, b_im, c_re, c_im, d_skip, T=T)
    u = u_tok.reshape(nc, T, G, C).transpose(2, 0, 1, 3).reshape(G, nc, T * C)
    s_loc = _ssm_state(u, gm)
    s_in = _ssm_scan(s_loc.reshape(nc, G, 4 * SSM_STATE), a_re, a_im, log_dt, T=T, n_tok=n)
    y = _ssm_out(u, mt, s_in.reshape(nc, G * 4 * SSM_STATE), cs)
    return y.reshape(G, nc, T, C).transpose(1, 2, 0, 3).reshape(n, G * C)


def _kv_kernel(m_ref, g_ref, b_ref, w_ref, o_ref, mb_ref):
    @pl.when(pl.program_id(0) == 0)
    def _():
        mb_ref[...] = _layernorm_rows(m_ref[...], g_ref[...], b_ref[...]).astype(BF16)

    o_ref[...] = _dot(mb_ref[...], w_ref[...]).astype(o_ref.dtype)


def _kv_proj(mem, ln_g, ln_b, w_kv):
    n, d = mem.shape
    m = w_kv.shape[1]
    tn = KV_TN
    est = 2 * n * d * 4 + n * d * 2 + 2 * d * tn * 2 + 2 * n * tn * 2 + 3 * n * d * 4
    return pl.pallas_call(
        _kv_kernel,
        grid=(m // tn,),
        in_specs=[pl.BlockSpec((n, d), lambda j: (0, 0)),
                  pl.BlockSpec((1, d), lambda j: (0, 0)),
                  pl.BlockSpec((1, d), lambda j: (0, 0)),
                  pl.BlockSpec((d, tn), lambda j: (0, j))],
        out_specs=pl.BlockSpec((n, tn), lambda j: (0, j)),
        out_shape=jax.ShapeDtypeStruct((n, m), BF16),
        scratch_shapes=[pltpu.VMEM((n, d), BF16)],
        compiler_params=pltpu.CompilerParams(
            dimension_semantics=("arbitrary",), vmem_limit_bytes=_vmem_limit(est)),
        name="kv_proj",
    )(mem, ln_g, ln_b, w_kv)


def _mixer_kernel(ya_ref, ug_ref, vg_ref, q_ref, g0_ref, g1_ref, g2_ref, kv_ref,
                  ws_ref, bs_ref, lng_ref, lnb_ref, wa_ref, wb_ref, wgp_ref, wap_ref,
                  o_ref, ub_ref, ob_ref, *, tm):
    n = pl.program_id(1)

    @pl.when(n == 0)
    def _():
        hd = GMLP_HEAD_DIM

        def chunk_body(ci, carry):
            r0 = pl.multiple_of(ci * GMLP_CHUNK, GMLP_CHUNK)
            rows = pl.ds(r0, GMLP_CHUNK)
            u = _gelu_tanh(ug_ref[rows, :].astype(F32))
            v = _gelu_tanh(vg_ref[rows, :].astype(F32))
            v = _layernorm_rows(v, lng_ref[...], lnb_ref[...]).astype(BF16)
            for h in range(GMLP_HEADS):
                cols = slice(h * hd, (h + 1) * hd)
                mixed = _dot(ws_ref[h], v[:, cols]) + bs_ref[:, h:h + 1]
                ub_ref[rows, cols] = (u[:, cols] * mixed).astype(BF16)
            return carry

        lax.fori_loop(0, tm // GMLP_CHUNK, chunk_body, 0)

        ad = ATTN_HEAD_DIM
        scale = ad ** -0.5
        for h in range(ATTN_HEADS):
            cols = slice(h * ad, (h + 1) * ad)
            vcols = slice(ATTN_WIDTH + h * ad, ATTN_WIDTH + (h + 1) * ad)
            s = _dot_nt(q_ref[:, cols], kv_ref[:, cols]) * scale
            s = s - jnp.max(s, axis=-1, keepdims=True)
            p = jnp.exp(s)
            p = p / jnp.sum(p, axis=-1, keepdims=True)
            ob_ref[:, cols] = _dot(p.astype(BF16), kv_ref[:, vcols]).astype(BF16)

    ya = ya_ref[...]
    y_a = _dot(ya, wa_ref[...]) * _sigmoid(_dot(ya, wb_ref[...]))
    y_b = _dot(ub_ref[...], wgp_ref[...])
    y_c = _dot(ob_ref[...], wap_ref[...])
    merged = (_sigmoid(g0_ref[...].astype(F32)) * y_a
              + _sigmoid(g1_ref[...].astype(F32)) * y_b
              + _sigmoid(g2_ref[...].astype(F32)) * y_c)
    o_ref[...] = merged.astype(o_ref.dtype)


def _mixer(ya, proj, kv, w_s, b_s, gln_g, gln_b, w_glu, w_gproj, w_aproj):
    n = ya.shape[0]
    d = D_MODEL
    tm, tn = MIX_TM, MIX_TN
    nn = d // tn
    wid = GMLP_WIDTH
    blocks_per_seq = SEQ_BLOCK // tm
    n_mem_batches = kv.shape[0] // N_MEM

    def kv_map(i, j):
        return (jnp.minimum(i // blocks_per_seq, n_mem_batches - 1), 0)

    def gate_spec(br):
        base = (OFF_GATE + br * d) // tn
        return pl.BlockSpec((tm, tn), lambda i, j: (i, base + j))

    row_blk = lambda c: pl.BlockSpec((tm, wid), lambda i, j: (i, c))
    full2 = lambda a: pl.BlockSpec(a.shape, lambda i, j: (0, 0))
    est = (2 * 4 * tm * wid * 2 + 2 * 3 * tm * tn * 2 + 2 * N_MEM * 2 * ATTN_WIDTH * 2
           + 2 * 4 * wid * tn * 2 + 2 * tm * tn * 2 + 2 * tm * wid * 2
           + 6 * tm * tn * 4 + 4 * GMLP_CHUNK * wid * 4 + 3 * tm * N_MEM * 4)
    return pl.pallas_call(
        functools.partial(_mixer_kernel, tm=tm),
        grid=(n // tm, nn),
        in_specs=[
            pl.BlockSpec((tm, wid), lambda i, j: (i, 0)),
            row_blk(OFF_GU // wid), row_blk(OFF_GV // wid), row_blk(OFF_Q // wid),
            gate_spec(0), gate_spec(1), gate_spec(2),
            pl.BlockSpec((N_MEM, 2 * ATTN_WIDTH), kv_map),
            pl.BlockSpec(w_s.shape, lambda i, j: (0, 0, 0)),
            full2(b_s), full2(gln_g), full2(gln_b),
            pl.BlockSpec((wid, tn), lambda i, j: (0, j)),
            pl.BlockSpec((wid, tn), lambda i, j: (0, nn + j)),
            pl.BlockSpec((wid, tn), lambda i, j: (0, j)),
            pl.BlockSpec((wid, tn), lambda i, j: (0, j)),
        ],
        out_specs=pl.BlockSpec((tm, tn), lambda i, j: (i, j)),
        out_shape=jax.ShapeDtypeStruct((n, d), BF16),
        scratch_shapes=[pltpu.VMEM((tm, wid), BF16), pltpu.VMEM((tm, wid), BF16)],
        compiler_params=pltpu.CompilerParams(
            dimension_semantics=("parallel", "arbitrary"),
            vmem_limit_bytes=_vmem_limit(est)),
        name="mixer",
    )(ya, proj, proj, proj, proj, proj, proj, kv, w_s, b_s, gln_g, gln_b,
      w_glu, w_glu, w_gproj, w_aproj)


def _outproj_ln_kernel(m_ref, w_ref, x_ref, g_ref, b_ref, o_ref):
    y = ALPHA * x_ref[...] + _dot(m_ref[...], w_ref[...])
    o_ref[...] = _layernorm_rows(y, g_ref[...], b_ref[...])


def _outproj_ln(merged, w_out, x, ln_g, ln_b):
    n, d = x.shape
    tm = OUT_TM
    est = 2 * tm * d * 2 + 2 * d * d * 2 + 4 * tm * d * 4 + 2 * tm * d * 4
    return pl.pallas_call(
        _outproj_ln_kernel,
        grid=(n // tm,),
        in_specs=[pl.BlockSpec((tm, d), lambda i: (i, 0)),
                  pl.BlockSpec((d, d), lambda i: (0, 0)),
                  pl.BlockSpec((tm, d), lambda i: (i, 0)),
                  pl.BlockSpec((1, d), lambda i: (0, 0)),
                  pl.BlockSpec((1, d), lambda i: (0, 0))],
        out_specs=pl.BlockSpec((tm, d), lambda i: (i, 0)),
        out_shape=jax.ShapeDtypeStruct((n, d), F32),
        compiler_params=pltpu.CompilerParams(
            dimension_semantics=("parallel",), vmem_limit_bytes=_vmem_limit(est)),
        name="outproj_ln",
    )(merged, w_out, x, ln_g, ln_b)


def _layer(x, mem, ffn1_w_gu, ffn1_w_down, ln1_g, ln1_b, w_in,
           ssm_a_re, ssm_a_im, ssm_log_dt, ssm_b_re, ssm_b_im, ssm_c_re, ssm_c_im, ssm_d, ssm_w_glu,
           gmlp_ln_g, gmlp_ln_b, gmlp_w_s, gmlp_b_s, gmlp_w_proj,
           mem_ln_g, mem_ln_b, attn_w_kv, attn_w_proj,
           w_out, ln2_g, ln2_b, ffn2_w_gu, ffn2_w_down, ln3_g, ln3_b):
    row = lambda v: v.reshape(1, -1)
    wgu1, wd1 = _prep_ffn_weights(ffn1_w_gu, ffn1_w_down)
    x1, x1b = _ffn_ln(x, wgu1, wd1, row(ln1_g), row(ln1_b), with_bf16=True)
    proj = _matmul(x1b, w_in.astype(BF16), tm=PROJ_TM, tn=PROJ_TN, out_dtype=BF16)
    ya = _s5_mixer_gelu(proj[:, :SSM_WIDTH], ssm_a_re, ssm_a_im, ssm_log_dt,
                        ssm_b_re, ssm_b_im, ssm_c_re, ssm_c_im, ssm_d)
    kv = _kv_proj(mem, row(mem_ln_g), row(mem_ln_b), attn_w_kv.astype(BF16))
    merged = _mixer(ya, proj, kv, gmlp_w_s.astype(BF16), gmlp_b_s, row(gmlp_ln_g), row(gmlp_ln_b),
                    ssm_w_glu.astype(BF16), gmlp_w_proj.astype(BF16), attn_w_proj.astype(BF16))
    x2 = _outproj_ln(merged, w_out.astype(BF16), x1, row(ln2_g), row(ln2_b))
    wgu2, wd2 = _prep_ffn_weights(ffn2_w_gu, ffn2_w_down)
    return _ffn_ln(x2, wgu2, wd2, row(ln3_g), row(ln3_b), with_bf16=False)


def kernel(x_prompt, x_sample, mem_prompt, mem_sample, ffn1_w_gu, ffn1_w_down, ln1_g, ln1_b, w_in,
           ssm_a_re, ssm_a_im, ssm_log_dt, ssm_b_re, ssm_b_im, ssm_c_re, ssm_c_im, ssm_d, ssm_w_glu,
           gmlp_ln_g, gmlp_ln_b, gmlp_w_s, gmlp_b_s, gmlp_w_proj, mem_ln_g, mem_ln_b, attn_w_kv,
           attn_w_proj, w_out, ln2_g, ln2_b, ffn2_w_gu, ffn2_w_down, ln3_g, ln3_b):
    d = x_prompt.shape[-1]
    n_prompt = x_prompt.shape[0] * x_prompt.shape[1]
    assert x_prompt.shape[1] == SEQ_BLOCK and x_sample.shape[0] == 1
    assert x_sample.shape[1] % SEQ_BLOCK == 0
    x = jnp.concatenate([x_prompt.reshape(-1, d), x_sample.reshape(-1, d)], axis=0)
    mem = jnp.concatenate([mem_prompt.reshape(-1, d), mem_sample.reshape(-1, d)], axis=0)
    for l in range(DEPTH):
        x = _layer(x, mem, ffn1_w_gu[l], ffn1_w_down[l], ln1_g[l], ln1_b[l], w_in[l],
                   ssm_a_re[l], ssm_a_im[l], ssm_log_dt[l], ssm_b_re[l], ssm_b_im[l],
                   ssm_c_re[l], ssm_c_im[l], ssm_d[l], ssm_w_glu[l],
                   gmlp_ln_g[l], gmlp_ln_b[l], gmlp_w_s[l], gmlp_b_s[l], gmlp_w_proj[l],
                   mem_ln_g[l], mem_ln_b[l], attn_w_kv[l], attn_w_proj[l],
                   w_out[l], ln2_g[l], ln2_b[l], ffn2_w_gu[l], ffn2_w_down[l], ln3_g[l], ln3_b[l])
    y_prompt = x[:n_prompt].reshape(x_prompt.shape)
    y_sample = x[n_prompt:].reshape(x_sample.shape)
    return (y_prompt, y_sample)
```

```python
import functools
import math

import jax
import jax.numpy as jnp
from jax import lax
from jax.experimental import pallas as pl
from jax.experimental.pallas import tpu as pltpu

F32 = jnp.float32
BF16 = jnp.bfloat16

D_MODEL = 2048
DEPTH = 1
SEQ_BLOCK = 4096
N_MEM = 256
SSM_WIDTH = D_MODEL // 2
SSM_GROUP = 16
SSM_GROUPS = SSM_WIDTH // SSM_GROUP
SSM_STATE = 64
GMLP_WIDTH = D_MODEL // 2
GMLP_CHUNK = 128
GMLP_HEADS = 8
GMLP_HEAD_DIM = GMLP_WIDTH // GMLP_HEADS
ATTN_HEADS = 4
ATTN_HEAD_DIM = D_MODEL // 8
ATTN_WIDTH = ATTN_HEADS * ATTN_HEAD_DIM
D_FF = 5504
OFF_GU = SSM_WIDTH
OFF_GV = OFF_GU + GMLP_WIDTH
OFF_Q = OFF_GV + GMLP_WIDTH
OFF_GATE = OFF_Q + ATTN_WIDTH
IN_WIDTH = OFF_GATE + 3 * D_MODEL
ALPHA = (2.0 * DEPTH) ** 0.25
LN_EPS = 1e-5

V7X_LANES = 128
V7X_VMEM_BYTES = 64 * 1024 * 1024

SSM_CHUNK = 32
SSM_GROUPS_PER_STEP = 8
FFN_TM = 512
FFN_TF = 1024
PROJ_TM = 1024
PROJ_TN = 1024
MIX_TM = 512
MIX_TN = 512
OUT_TM = 512
KV_TN = 512


def _vmem_limit(nbytes):
    return int(min(nbytes + (16 << 20), V7X_VMEM_BYTES - (4 << 20)))


def _layernorm_rows(y, g, b):
    mu = jnp.mean(y, axis=-1, keepdims=True)
    yc = y - mu
    var = jnp.mean(yc * yc, axis=-1, keepdims=True)
    return yc * lax.rsqrt(var + LN_EPS) * g + b


def _gelu_tanh(x):
    c = math.sqrt(2.0 / math.pi)
    return 0.5 * x * (1.0 + jnp.tanh(c * (x + 0.044715 * (x * x * x))))


def _sigmoid(x):
    return 1.0 / (1.0 + jnp.exp(-x))


def _dot(a, b):
    return jnp.dot(a, b, preferred_element_type=F32)


def _dot_nt(a, b):
    return lax.dot_general(a, b, (((1,), (1,)), ((), ())), preferred_element_type=F32)


def _swiglu_down(xb, wg, wu, wd):
    gate = _dot(xb, wg)
    up = _dot(xb, wu)
    act = (gate * _sigmoid(gate) * up).astype(BF16)
    return _dot(act, wd)


def _ffn_ln_kernel(x_ref, wg_ref, wu_ref, wd_ref, wgt_ref, wut_ref, wdt_ref, g_ref, b_ref,
                   o_ref, xb_ref):
    j = pl.program_id(1)

    @pl.when(j == 0)
    def _():
        xb_ref[...] = x_ref[...].astype(BF16)
        o_ref[...] = jnp.zeros_like(o_ref)

    o_ref[...] += _swiglu_down(xb_ref[...], wg_ref[...], wu_ref[...], wd_ref[...])

    @pl.when(j == pl.num_programs(1) - 1)
    def _():
        acc = o_ref[...] + _swiglu_down(xb_ref[...], wgt_ref[...], wut_ref[...], wdt_ref[...])
        y = ALPHA * x_ref[...] + 0.5 * acc
        o_ref[...] = _layernorm_rows(y, g_ref[...], b_ref[...])


def _ffn_ln(x, weights, ln_g, ln_b):
    wg, wu, wd, wgt, wut, wdt = weights
    n, d = x.shape
    tm, tf = FFN_TM, FFN_TF
    nf = wd.shape[0] // tf
    ft = wdt.shape[0]
    once = dict(pipeline_mode=pl.Buffered(1))
    est = (2 * tm * d * 4 + 2 * tm * d * 4 + tm * d * 2
           + 2 * 3 * (d * tf * 2) + 3 * (d * ft * 2) + 3 * tm * tf * 4)
    return pl.pallas_call(
        _ffn_ln_kernel,
        grid=(n // tm, nf),
        in_specs=[
            pl.BlockSpec((tm, d), lambda i, j: (i, 0)),
            pl.BlockSpec((d, tf), lambda i, j: (0, j)),
            pl.BlockSpec((d, tf), lambda i, j: (0, j)),
            pl.BlockSpec((tf, d), lambda i, j: (j, 0)),
            pl.BlockSpec((d, ft), lambda i, j: (0, 0), **once),
            pl.BlockSpec((d, ft), lambda i, j: (0, 0), **once),
            pl.BlockSpec((ft, d), lambda i, j: (0, 0), **once),
            pl.BlockSpec((1, d), lambda i, j: (0, 0)),
            pl.BlockSpec((1, d), lambda i, j: (0, 0)),
        ],
        out_specs=pl.BlockSpec((tm, d), lambda i, j: (i, 0)),
        out_shape=jax.ShapeDtypeStruct((n, d), F32),
        scratch_shapes=[pltpu.VMEM((tm, d), BF16)],
        compiler_params=pltpu.CompilerParams(
            dimension_semantics=("parallel", "arbitrary"),
            vmem_limit_bytes=_vmem_limit(est)),
        name="ffn_ln",
    )(x, wg, wu, wd, wgt, wut, wdt, ln_g, ln_b)


def _prep_ffn_weights(w_gu, w_down):
    wg = w_gu[:, :D_FF].astype(BF16)
    wu = w_gu[:, D_FF:].astype(BF16)
    wd = w_down.astype(BF16)
    full = (D_FF // FFN_TF) * FFN_TF
    return wg, wu, wd, wg[:, full:], wu[:, full:], wd[full:]


def _in_proj_kernel(x_ref, w_ref, o_ref, xb_ref):
    @pl.when(pl.program_id(1) == 0)
    def _():
        xb_ref[...] = x_ref[...].astype(BF16)

    o_ref[...] = _dot(xb_ref[...], w_ref[...]).astype(o_ref.dtype)


def _in_proj(x, w, *, tm, tn, out_dtype):
    n, k = x.shape
    m = w.shape[1]
    est = 2 * tm * k * 4 + tm * k * 2 + 2 * k * tn * 2 + 2 * tm * tn * 2 + tm * tn * 4
    return pl.pallas_call(
        _in_proj_kernel,
        grid=(n // tm, m // tn),
        in_specs=[pl.BlockSpec((tm, k), lambda i, j: (i, 0)),
                  pl.BlockSpec((k, tn), lambda i, j: (0, j))],
        out_specs=pl.BlockSpec((tm, tn), lambda i, j: (i, j)),
        out_shape=jax.ShapeDtypeStruct((n, m), out_dtype),
        scratch_shapes=[pltpu.VMEM((tm, k), BF16)],
        compiler_params=pltpu.CompilerParams(
            dimension_semantics=("parallel", "arbitrary"),
            vmem_limit_bytes=_vmem_limit(est)),
        name="in_proj",
    )(x, w)


def _cmul(x, y):
    return x[0] * y[0] - x[1] * y[1], x[0] * y[1] + x[1] * y[0]


def _split_hi_lo(x):
    hi = x.astype(BF16)
    lo = (x - hi.astype(F32)).astype(BF16)
    return hi, lo


def _dot_hi_lo(a, b):
    ah, al = _split_hi_lo(a)
    bh, bl = _split_hi_lo(b)
    return _dot(ah, bh) + _dot(al, bh) + _dot(ah, bl)


def _ssm_tables_kernel(pcol_ref, prow_ref, bcat_ref, x1_ref, x2_ref, ct_ref, dtile_ref,
                       mt_ref, gm_ref, cs_ref, q_ref, *, T):
    P, C = SSM_STATE, SSM_GROUP
    rpt = V7X_LANES // C
    nt = T // rpt
    wide = 2 * T * C

    lane = lax.broadcasted_iota(jnp.int32, (1, V7X_LANES), 1)
    rr = lax.shift_right_logical(lane, 4).astype(F32)
    lanef = lane.astype(F32)
    expand = (lax.broadcasted_iota(jnp.int32, (C, V7X_LANES), 0)
              == (lax.broadcasted_iota(jnp.int32, (C, V7X_LANES), 1) & (C - 1))).astype(F32)

    pc = pcol_ref[...]

    def powers(d, k):
        dt = jnp.exp(pc[:, 4 + d:5 + d])
        zr = pc[:, d:d + 1] * dt
        zi = pc[:, 2 + d:3 + d] * dt
        mag = jnp.exp(k * zr)
        ang = k * zi
        return mag * jnp.cos(ang), mag * jnp.sin(ang)

    def zoh(d, pt):
        are = pc[:, d:d + 1]
        aim = pc[:, 2 + d:3 + d]
        nr = pt[0][:, 1:2] - 1.0
        ni = pt[1][:, 1:2]
        den = are * are + aim * aim
        return (nr * are + ni * aim) / den, (ni * are - nr * aim) / den

    def col(pt, m):
        return pt[0][:, m:m + 1], pt[1][:, m:m + 1]

    asc_f, dsc_f, pt_f = powers(0, rr), powers(0, (rpt - 1.0) - rr), powers(0, lanef)
    asc_b, dsc_b, pt_b = powers(1, rr), powers(1, (rpt - 1.0) - rr), powers(1, lanef)
    w_f, w_b = zoh(0, pt_f), zoh(1, pt_b)

    ctile = jnp.dot(ct_ref[...], expand, precision=lax.Precision.HIGHEST,
                    preferred_element_type=F32)
    btile = jnp.dot(bcat_ref[...], expand, precision=lax.Precision.HIGHEST,
                    preferred_element_type=F32)
    ct_f = (ctile[0:P], ctile[P:2 * P])
    ct_b = (ctile[2 * P:3 * P], ctile[3 * P:4 * P])
    bt = (btile[0:P], btile[P:2 * P])

    ca_f = _cmul(asc_f, ct_f)
    cd_b = _cmul(dsc_b, ct_b)
    bd_f = _cmul(dsc_f, _cmul(w_f, bt))
    ba_b = _cmul(asc_b, _cmul(w_b, bt))

    for j in range(nt):
        cols = slice(j * V7X_LANES, (j + 1) * V7X_LANES)
        xf = _cmul(bd_f, col(pt_f, T - rpt - rpt * j))
        xb = _cmul(ba_b, col(pt_b, rpt * j))
        gm_ref[0:P, cols] = xf[0].astype(BF16)
        gm_ref[P:2 * P, cols] = xf[1].astype(BF16)
        gm_ref[2 * P:3 * P, cols] = xb[0].astype(BF16)
        gm_ref[3 * P:4 * P, cols] = xb[1].astype(BF16)
        yf = _cmul(ca_f, col(pt_f, rpt * j + 1))
        yb = _cmul(cd_b, col(pt_b, T - rpt * j - (rpt - 1)))
        cs_ref[0:P, cols] = yf[0].astype(BF16)
        cs_ref[P:2 * P, cols] = (-yf[1]).astype(BF16)
        cs_ref[2 * P:3 * P, cols] = yb[0].astype(BF16)
        cs_ref[3 * P:4 * P, cols] = (-yb[1]).astype(BF16)

    zeros = jnp.zeros((2 * P, V7X_LANES), F32)
    for j in range(2 * nt):
        cols = slice(j * V7X_LANES, (j + 1) * V7X_LANES)
        if j < nt:
            qb = _cmul(cd_b, col(pt_b, T - rpt * j - (rpt - 1)))
            q_ref[0:2 * P, cols] = zeros
            q_ref[2 * P:3 * P, cols] = qb[0]
            q_ref[3 * P:4 * P, cols] = qb[1]
        else:
            qf = _cmul(ca_f, col(pt_f, rpt * j - T))
            q_ref[0:P, cols] = qf[0]
            q_ref[P:2 * P, cols] = qf[1]
            if j == nt:
                center = rr == 0.0
                q_ref[2 * P:3 * P, cols] = jnp.where(center, ct_b[0], 0.0)
                q_ref[3 * P:4 * P, cols] = jnp.where(center, ct_b[1], 0.0)
            else:
                q_ref[2 * P:4 * P, cols] = zeros

    pr = prow_ref[...]
    are, aim = pr[0:1], pr[1:2]
    dt = jnp.exp(pr[2:3])
    zr, zi = are * dt, aim * dt
    mag = jnp.exp(zr)
    nr, ni = mag * jnp.cos(zi) - 1.0, mag * jnp.sin(zi)
    den = are * are + aim * aim
    wr, wi = (nr * are + ni * aim) / den, (ni * are - nr * aim) / den
    lane4 = lax.broadcasted_iota(jnp.int32, (1, 4 * P), 1)
    sgn = jnp.where((lax.shift_right_logical(lane4, 6) & 1) == 0, 1.0, -1.0)
    lhs = (wr * sgn) * x1_ref[...] - wi * x2_ref[...]

    zt = _dot_hi_lo(lhs, q_ref[...])
    lanew = lax.broadcasted_iota(jnp.int32, (C, wide), 1)
    roww = lax.broadcasted_iota(jnp.int32, (C, wide), 0)
    diag = (lax.shift_right_logical(lanew, 4) == T) & ((lanew & (C - 1)) == roww)
    zt = zt + jnp.where(diag, dtile_ref[...], 0.0)

    for rp in range(T):
        off = (T - rp) * C
        shifted = pltpu.roll(zt, (wide - off) % wide, 1)
        mt_ref[rp * C:(rp + 1) * C, :] = shifted[:, :T * C].astype(BF16)


def _ssm_tables(a_re, a_im, log_dt, b_re, b_im, c_re, c_im, d_skip, *, T):
    G, P, C = SSM_GROUPS, SSM_STATE, SSM_GROUP
    ldt = jnp.broadcast_to(log_dt[:, :, None], (2, G, P))
    zc = jnp.zeros((G, P), F32)
    pcol = jnp.stack([a_re[0], a_re[1], a_im[0], a_im[1], ldt[0], ldt[1], zc, zc], axis=-1)

    def row4(x):
        return jnp.concatenate([x[0], x[0], x[1], x[1]], axis=-1)

    zr4 = jnp.zeros((G, 4 * P), F32)
    prow = jnp.stack([row4(a_re), row4(a_im), row4(ldt), zr4, zr4, zr4, zr4, zr4], axis=1)
    btr = jnp.swapaxes(b_re, 1, 2)
    bti = jnp.swapaxes(b_im, 1, 2)
    x1 = jnp.concatenate([btr, bti, btr, bti], axis=-1)
    x2 = jnp.concatenate([bti, btr, bti, btr], axis=-1)
    bcat = jnp.concatenate([b_re, b_im], axis=1)
    ct = jnp.concatenate([jnp.swapaxes(c_re[0], 1, 2), jnp.swapaxes(c_im[0], 1, 2),
                          jnp.swapaxes(c_re[1], 1, 2), jnp.swapaxes(c_im[1], 1, 2)], axis=1)
    dtile = jnp.tile(d_skip, (1, 2 * T))[:, None, :]

    tc = T * C
    return pl.pallas_call(
        functools.partial(_ssm_tables_kernel, T=T),
        grid=(G,),
        in_specs=[
            pl.BlockSpec((None, P, 8), lambda g: (g, 0, 0)),
            pl.BlockSpec((None, 8, 4 * P), lambda g: (g, 0, 0)),
            pl.BlockSpec((None, 2 * P, C), lambda g: (g, 0, 0)),
            pl.BlockSpec((None, C, 4 * P), lambda g: (g, 0, 0)),
            pl.BlockSpec((None, C, 4 * P), lambda g: (g, 0, 0)),
            pl.BlockSpec((None, 4 * P, C), lambda g: (g, 0, 0)),
            pl.BlockSpec((None, 1, 2 * tc), lambda g: (g, 0, 0)),
        ],
        out_specs=[
            pl.BlockSpec((None, tc, tc), lambda g: (g, 0, 0)),
            pl.BlockSpec((None, 4 * P, tc), lambda g: (g, 0, 0)),
            pl.BlockSpec((None, 4 * P, tc), lambda g: (g, 0, 0)),
        ],
        out_shape=[
            jax.ShapeDtypeStruct((G, tc, tc), BF16),
            jax.ShapeDtypeStruct((G, 4 * P, tc), BF16),
            jax.ShapeDtypeStruct((G, 4 * P, tc), BF16),
        ],
        scratch_shapes=[pltpu.VMEM((4 * P, 2 * tc), F32)],
        compiler_params=pltpu.CompilerParams(dimension_semantics=("parallel",)),
        name="ssm_tables",
    )(pcol, prow, bcat, x1, x2, ct, dtile)


def _ssm_state_kernel(u_ref, gm_ref, o_ref, *, gb, ns):
    for gi in range(gb):
        o_ref[:, gi * ns:(gi + 1) * ns] = _dot_nt(u_ref[gi], gm_ref[gi])


def _ssm_state(u, gm):
    G, nc, tc = u.shape
    ns = gm.shape[1]
    gb = SSM_GROUPS_PER_STEP
    return pl.pallas_call(
        functools.partial(_ssm_state_kernel, gb=gb, ns=ns),
        grid=(G // gb,),
        in_specs=[pl.BlockSpec((gb, nc, tc), lambda s: (s, 0, 0)),
                  pl.BlockSpec((gb, ns, tc), lambda s: (s, 0, 0))],
        out_specs=pl.BlockSpec((nc, gb * ns), lambda s: (0, s)),
        out_shape=jax.ShapeDtypeStruct((nc, G * ns), F32),
        compiler_params=pltpu.CompilerParams(dimension_semantics=("parallel",)),
        name="ssm_state",
    )(u, gm)


def _ssm_scan_kernel(are_ref, aim_ref, ldt_ref, s_ref, o_ref, x_ref, xs_ref, *, T, n0, seq_starts, seq_ends):
    d = pl.program_id(0)
    j = pl.program_id(1)
    nblk = pl.num_programs(1)
    blk = j + d * (nblk - 1 - 2 * j)
    P = SSM_STATE

    is_start = functools.reduce(jnp.logical_or, [blk == s for s in seq_starts])
    is_end = functools.reduce(jnp.logical_or, [blk == e for e in seq_ends])
    reset = jnp.where(d == 0, is_start, is_end)

    @pl.when(reset)
    def _():
        x_ref[...] = jnp.zeros_like(x_ref)
        xs_ref[...] = jnp.zeros_like(xs_ref)

    dt = jnp.exp(ldt_ref[...])
    zr = are_ref[...] * dt
    zi = aim_ref[...] * dt
    mag = jnp.exp(float(T) * zr)
    mr = mag * jnp.cos(float(T) * zi)
    mi = mag * jnp.sin(float(T) * zi)
    lane = lax.broadcasted_iota(jnp.int32, mr.shape, 1)
    m2 = jnp.where(lane < P, -mi, mi)
    m2s = -m2

    def body(k, carry):
        x, xs = carry
        row = k + d * (n0 - 1 - 2 * k)
        loc = s_ref[row]
        o_ref[row] = x.astype(o_ref.dtype)
        locs = pltpu.roll(loc, P, 1)
        return x * mr + xs * m2 + loc, xs * mr + x * m2s + locs

    x, xs = lax.fori_loop(0, n0, body, (x_ref[...], xs_ref[...]), unroll=4)
    x_ref[...] = x
    xs_ref[...] = xs


def _ssm_scan(s_loc, a_re, a_im, log_dt, *, T, n_tok):
    G, P = SSM_GROUPS, SSM_STATE
    nc = s_loc.shape[0]
    n0 = SEQ_BLOCK // T
    nblk = nc // n0
    seq_starts = (0, 1, 2)
    seq_ends = (0, 1, nblk - 1)
    dup = lambda x: jnp.concatenate([x, x], axis=-1)
    are2, aim2 = dup(a_re), dup(a_im)
    ldt2 = jnp.broadcast_to(log_dt[:, :, None], (2, G, 2 * P))

    def blk_map(d, j):
        return (j + d * (nblk - 1 - 2 * j), 0, d)

    par = pl.BlockSpec((None, G, 2 * P), lambda d, j: (d, 0, 0))
    return pl.pallas_call(
        functools.partial(_ssm_scan_kernel, T=T, n0=n0, seq_starts=seq_starts, seq_ends=seq_ends),
        grid=(2, nblk),
        in_specs=[par, par, par, pl.BlockSpec((n0, G, 2 * P), blk_map)],
        out_specs=pl.BlockSpec((n0, G, 2 * P), blk_map),
        out_shape=jax.ShapeDtypeStruct((nc, G, 4 * P), BF16),
        scratch_shapes=[pltpu.VMEM((G, 2 * P), F32), pltpu.VMEM((G, 2 * P), F32)],
        compiler_params=pltpu.CompilerParams(dimension_semantics=("arbitrary", "arbitrary")),
        name="ssm_scan",
    )(are2, aim2, ldt2, s_loc)


def _ssm_out_kernel(u_ref, mt_ref, s_ref, cs_ref, y_ref, *, gb, ns):
    for gi in range(gb):
        y = _dot(u_ref[gi], mt_ref[gi]) + _dot(s_ref[:, gi * ns:(gi + 1) * ns], cs_ref[gi])
        y_ref[gi] = _gelu_tanh(y).astype(y_ref.dtype)


def _ssm_out(u, mt, s_in, cs):
    G, nc, tc = u.shape
    ns = cs.shape[1]
    gb = SSM_GROUPS_PER_STEP
    return pl.pallas_call(
        functools.partial(_ssm_out_kernel, gb=gb, ns=ns),
        grid=(G // gb,),
        in_specs=[pl.BlockSpec((gb, nc, tc), lambda s: (s, 0, 0)),
                  pl.BlockSpec((gb, tc, tc), lambda s: (s, 0, 0)),
                  pl.BlockSpec((nc, gb * ns), lambda s: (0, s)),
                  pl.BlockSpec((gb, ns, tc), lambda s: (s, 0, 0))],
        out_specs=pl.BlockSpec((gb, nc, tc), lambda s: (s, 0, 0)),
        out_shape=jax.ShapeDtypeStruct((G, nc, tc), BF16),
        compiler_params=pltpu.CompilerParams(dimension_semantics=("parallel",)),
        name="ssm_out",
    )(u, mt, s_in, cs)


def _s5_mixer_gelu(u_tok, a_re, a_im, log_dt, b_re, b_im, c_re, c_im, d_skip):
    n = u_tok.shape[0]
    T, G, C = SSM_CHUNK, SSM_GROUPS, SSM_GROUP
    nc = n // T
    mt, gm, cs = _ssm_tables(a_re, a_im, log_dt, b_re, b_im, c_re, c_im, d_skip, T=T)
    u = u_tok.reshape(nc, T, G, C).transpose(2, 0, 1, 3).reshape(G, nc, T * C)
    s_loc = _ssm_state(u, gm)
    s_in = _ssm_scan(s_loc.reshape(nc, G, 4 * SSM_STATE), a_re, a_im, log_dt, T=T, n_tok=n)
    y = _ssm_out(u, mt, s_in.reshape(nc, G * 4 * SSM_STATE), cs)
    return y.reshape(G, nc, T, C).transpose(1, 2, 0, 3).reshape(n, G * C)


def _kv_kernel(m_ref, g_ref, b_ref, w_ref, o_ref, mb_ref):
    @pl.when(pl.program_id(0) == 0)
    def _():
        mb_ref[...] = _layernorm_rows(m_ref[...], g_ref[...], b_ref[...]).astype(BF16)

    o_ref[...] = _dot(mb_ref[...], w_ref[...]).astype(o_ref.dtype)


def _kv_proj(mem, ln_g, ln_b, w_kv):
    n, d = mem.shape
    m = w_kv.shape[1]
    tn = KV_TN
    est = 2 * n * d * 4 + n * d * 2 + 2 * d * tn * 2 + 2 * n * tn * 2 + 3 * n * d * 4
    return pl.pallas_call(
        _kv_kernel,
        grid=(m // tn,),
        in_specs=[pl.BlockSpec((n, d), lambda j: (0, 0)),
                  pl.BlockSpec((1, d), lambda j: (0, 0)),
                  pl.BlockSpec((1, d), lambda j: (0, 0)),
                  pl.BlockSpec((d, tn), lambda j: (0, j))],
        out_specs=pl.BlockSpec((n, tn), lambda j: (0, j)),
        out_shape=jax.ShapeDtypeStruct((n, m), BF16),
        scratch_shapes=[pltpu.VMEM((n, d), BF16)],
        compiler_params=pltpu.CompilerParams(
            dimension_semantics=("arbitrary",), vmem_limit_bytes=_vmem_limit(est)),
        name="kv_proj",
    )(mem, ln_g, ln_b, w_kv)


def _mixer_kernel(ya_ref, ug_ref, vg_ref, q_ref, g0_ref, g1_ref, g2_ref, kv_ref,
                  ws_ref, bs_ref, lng_ref, lnb_ref, wa_ref, wb_ref, wgp_ref, wap_ref,
                  o_ref, ub_ref, ob_ref, *, tm):
    n = pl.program_id(1)

    @pl.when(n == 0)
    def _():
        hd = GMLP_HEAD_DIM

        def chunk_body(ci, carry):
            r0 = pl.multiple_of(ci * GMLP_CHUNK, GMLP_CHUNK)
            rows = pl.ds(r0, GMLP_CHUNK)
            u = _gelu_tanh(ug_ref[rows, :].astype(F32))
            v = _gelu_tanh(vg_ref[rows, :].astype(F32))
            v = _layernorm_rows(v, lng_ref[...], lnb_ref[...]).astype(BF16)
            for h in range(GMLP_HEADS):
                cols = slice(h * hd, (h + 1) * hd)
                mixed = _dot(ws_ref[h], v[:, cols]) + bs_ref[:, h:h + 1]
                ub_ref[rows, cols] = (u[:, cols] * mixed).astype(BF16)
            return carry

        lax.fori_loop(0, tm // GMLP_CHUNK, chunk_body, 0)

        ad = ATTN_HEAD_DIM
        scale = ad ** -0.5
        for h in range(ATTN_HEADS):
            cols = slice(h * ad, (h + 1) * ad)
            vcols = slice(ATTN_WIDTH + h * ad, ATTN_WIDTH + (h + 1) * ad)
            s = _dot_nt(q_ref[:, cols], kv_ref[:, cols]) * scale
            s = s - jnp.max(s, axis=-1, keepdims=True)
            p = jnp.exp(s)
            p = p / jnp.sum(p, axis=-1, keepdims=True)
            ob_ref[:, cols] = _dot(p.astype(BF16), kv_ref[:, vcols]).astype(BF16)

    ya = ya_ref[...]
    y_a = _dot(ya, wa_ref[...]) * _sigmoid(_dot(ya, wb_ref[...]))
    y_b = _dot(ub_ref[...], wgp_ref[...])
    y_c = _dot(ob_ref[...], wap_ref[...])
    merged = (_sigmoid(g0_ref[...].astype(F32)) * y_a
              + _sigmoid(g1_ref[...].astype(F32)) * y_b
              + _sigmoid(g2_ref[...].astype(F32)) * y_c)
    o_ref[...] = merged.astype(o_ref.dtype)


def _mixer(ya, proj, kv, w_s, b_s, gln_g, gln_b, w_glu, w_gproj, w_aproj):
    n = ya.shape[0]
    d = D_MODEL
    tm, tn = MIX_TM, MIX_TN
    nn = d // tn
    wid = GMLP_WIDTH
    blocks_per_seq = SEQ_BLOCK // tm
    n_mem_batches = kv.shape[0] // N_MEM

    def kv_map(i, j):
        return (jnp.minimum(i // blocks_per_seq, n_mem_batches - 1), 0)

    def gate_spec(br):
        base = (OFF_GATE + br * d) // tn
        return pl.BlockSpec((tm, tn), lambda i, j: (i, base + j))

    row_blk = lambda c: pl.BlockSpec((tm, wid), lambda i, j: (i, c))
    full2 = lambda a: pl.BlockSpec(a.shape, lambda i, j: (0, 0))
    est = (2 * 4 * tm * wid * 2 + 2 * 3 * tm * tn * 2 + 2 * N_MEM * 2 * ATTN_WIDTH * 2
           + 2 * 4 * wid * tn * 2 + 2 * tm * tn * 2 + 2 * tm * wid * 2
           + 6 * tm * tn * 4 + 4 * GMLP_CHUNK * wid * 4 + 3 * tm * N_MEM * 4)
    return pl.pallas_call(
        functools.partial(_mixer_kernel, tm=tm),
        grid=(n // tm, nn),
        in_specs=[
            pl.BlockSpec((tm, wid), lambda i, j: (i, 0)),
            row_blk(OFF_GU // wid), row_blk(OFF_GV // wid), row_blk(OFF_Q // wid),
            gate_spec(0), gate_spec(1), gate_spec(2),
            pl.BlockSpec((N_MEM, 2 * ATTN_WIDTH), kv_map),
            pl.BlockSpec(w_s.shape, lambda i, j: (0, 0, 0)),
            full2(b_s), full2(gln_g), full2(gln_b),
            pl.BlockSpec((wid, tn), lambda i, j: (0, j)),
            pl.BlockSpec((wid, tn), lambda i, j: (0, nn + j)),
            pl.BlockSpec((wid, tn), lambda i, j: (0, j)),
            pl.BlockSpec((wid, tn), lambda i, j: (0, j)),
        ],
        out_specs=pl.BlockSpec((tm, tn), lambda i, j: (i, j)),
        out_shape=jax.ShapeDtypeStruct((n, d), BF16),
        scratch_shapes=[pltpu.VMEM((tm, wid), BF16), pltpu.VMEM((tm, wid), BF16)],
        compiler_params=pltpu.CompilerParams(
            dimension_semantics=("parallel", "arbitrary"),
            vmem_limit_bytes=_vmem_limit(est)),
        name="mixer",
    )(ya, proj, proj, proj, proj, proj, proj, kv, w_s, b_s, gln_g, gln_b,
      w_glu, w_glu, w_gproj, w_aproj)


def _outproj_ln_kernel(m_ref, w_ref, x_ref, g_ref, b_ref, o_ref):
    y = ALPHA * x_ref[...] + _dot(m_ref[...], w_ref[...])
    o_ref[...] = _layernorm_rows(y, g_ref[...], b_ref[...])


def _outproj_ln(merged, w_out, x, ln_g, ln_b):
    n, d = x.shape
    tm = OUT_TM
    est = 2 * tm * d * 2 + 2 * d * d * 2 + 4 * tm * d * 4 + 2 * tm * d * 4
    return pl.pallas_call(
        _outproj_ln_kernel,
        grid=(n // tm,),
        in_specs=[pl.BlockSpec((tm, d), lambda i: (i, 0)),
                  pl.BlockSpec((d, d), lambda i: (0, 0)),
                  pl.BlockSpec((tm, d), lambda i: (i, 0)),
                  pl.BlockSpec((1, d), lambda i: (0, 0)),
                  pl.BlockSpec((1, d), lambda i: (0, 0))],
        out_specs=pl.BlockSpec((tm, d), lambda i: (i, 0)),
        out_shape=jax.ShapeDtypeStruct((n, d), F32),
        compiler_params=pltpu.CompilerParams(
            dimension_semantics=("parallel",), vmem_limit_bytes=_vmem_limit(est)),
        name="outproj_ln",
    )(merged, w_out, x, ln_g, ln_b)


def _layer(x, mem, ffn1_w_gu, ffn1_w_down, ln1_g, ln1_b, w_in,
           ssm_a_re, ssm_a_im, ssm_log_dt, ssm_b_re, ssm_b_im, ssm_c_re, ssm_c_im, ssm_d, ssm_w_glu,
           gmlp_ln_g, gmlp_ln_b, gmlp_w_s, gmlp_b_s, gmlp_w_proj,
           mem_ln_g, mem_ln_b, attn_w_kv, attn_w_proj,
           w_out, ln2_g, ln2_b, ffn2_w_gu, ffn2_w_down, ln3_g, ln3_b):
    row = lambda v: v.reshape(1, -1)
    x1 = _ffn_ln(x, _prep_ffn_weights(ffn1_w_gu, ffn1_w_down), row(ln1_g), row(ln1_b))
    proj = _in_proj(x1, w_in.astype(BF16), tm=PROJ_TM, tn=PROJ_TN, out_dtype=BF16)
    ya = _s5_mixer_gelu(proj[:, :SSM_WIDTH], ssm_a_re, ssm_a_im, ssm_log_dt,
                        ssm_b_re, ssm_b_im, ssm_c_re, ssm_c_im, ssm_d)
    kv = _kv_proj(mem, row(mem_ln_g), row(mem_ln_b), attn_w_kv.astype(BF16))
    merged = _mixer(ya, proj, kv, gmlp_w_s.astype(BF16), gmlp_b_s, row(gmlp_ln_g), row(gmlp_ln_b),
                    ssm_w_glu.astype(BF16), gmlp_w_proj.astype(BF16), attn_w_proj.astype(BF16))
    x2 = _outproj_ln(merged, w_out.astype(BF16), x1, row(ln2_g), row(ln2_b))
    return _ffn_ln(x2, _prep_ffn_weights(ffn2_w_gu, ffn2_w_down), row(ln3_g), row(ln3_b))


def kernel(x_prompt, x_sample, mem_prompt, mem_sample, ffn1_w_gu, ffn1_w_down, ln1_g, ln1_b, w_in,
           ssm_a_re, ssm_a_im, ssm_log_dt, ssm_b_re, ssm_b_im, ssm_c_re, ssm_c_im, ssm_d, ssm_w_glu,
           gmlp_ln_g, gmlp_ln_b, gmlp_w_s, gmlp_b_s, gmlp_w_proj, mem_ln_g, mem_ln_b, attn_w_kv,
           attn_w_proj, w_out, ln2_g, ln2_b, ffn2_w_gu, ffn2_w_down, ln3_g, ln3_b):
    d = x_prompt.shape[-1]
    n_prompt = x_prompt.shape[0] * x_prompt.shape[1]
    assert x_prompt.shape[1] == SEQ_BLOCK and x_sample.shape[0] == 1
    assert x_sample.shape[1] % SEQ_BLOCK == 0
    x = jnp.concatenate([x_prompt.reshape(-1, d), x_sample.reshape(-1, d)], axis=0)
    mem = jnp.concatenate([mem_prompt.reshape(-1, d), mem_sample.reshape(-1, d)], axis=0)
    for l in range(DEPTH):
        x = _layer(x, mem, ffn1_w_gu[l], ffn1_w_down[l], ln1_g[l], ln1_b[l], w_in[l],
                   ssm_a_re[l], ssm_a_im[l], ssm_log_dt[l], ssm_b_re[l], ssm_b_im[l],
                   ssm_c_re[l], ssm_c_im[l], ssm_d[l], ssm_w_glu[l],
                   gmlp_ln_g[l], gmlp_ln_b[l], gmlp_w_s[l], gmlp_b_s[l], gmlp_w_proj[l],
                   mem_ln_g[l], mem_ln_b[l], attn_w_kv[l], attn_w_proj[l],
                   w_out[l], ln2_g[l], ln2_b[l], ffn2_w_gu[l], ffn2_w_down[l], ln3_g[l], ln3_b[l])
    y_prompt = x[:n_prompt].reshape(x_prompt.shape)
    y_sample = x[n_prompt:].reshape(x_sample.shape)
    return (y_prompt, y_sample)
```

```python
import functools
import math

import jax
import jax.numpy as jnp
from jax import lax
from jax.experimental import pallas as pl
from jax.experimental.pallas import tpu as pltpu

F32 = jnp.float32
BF16 = jnp.bfloat16

D_MODEL = 2048
DEPTH = 1
SEQ_BLOCK = 4096
N_MEM = 256
SSM_WIDTH = D_MODEL // 2
SSM_GROUP = 16
SSM_GROUPS = SSM_WIDTH // SSM_GROUP
SSM_STATE = 64
GMLP_WIDTH = D_MODEL // 2
GMLP_CHUNK = 128
GMLP_HEADS = 8
GMLP_HEAD_DIM = GMLP_WIDTH // GMLP_HEADS
ATTN_HEADS = 4
ATTN_HEAD_DIM = D_MODEL // 8
ATTN_WIDTH = ATTN_HEADS * ATTN_HEAD_DIM
D_FF = 5504
OFF_GU = 0
OFF_GV = OFF_GU + GMLP_WIDTH
OFF_Q = OFF_GV + GMLP_WIDTH
OFF_GATE = OFF_Q + ATTN_WIDTH
ALPHA = (2.0 * DEPTH) ** 0.25
LN_EPS = 1e-5

V7X_LANES = 128
V7X_VMEM_BYTES = 64 * 1024 * 1024

SSM_CHUNK = 32
SSM_GROUPS_PER_STEP = 8
FFN_TM = 512
FFN_TF = 1024
PROJ_TM = 1024
PROJ_TN = 1024
MIX_TM = 512
MIX_TN = 512
OUT_TM = 512
KV_TN = 512


def _vmem_limit(nbytes):
    return int(min(nbytes + (16 << 20), V7X_VMEM_BYTES - (4 << 20)))


def _layernorm_rows(y, g, b):
    mu = jnp.mean(y, axis=-1, keepdims=True)
    yc = y - mu
    var = jnp.mean(yc * yc, axis=-1, keepdims=True)
    return yc * lax.rsqrt(var + LN_EPS) * g + b


def _gelu_tanh(x):
    c = math.sqrt(2.0 / math.pi)
    return 0.5 * x * (1.0 + jnp.tanh(c * (x + 0.044715 * (x * x * x))))


def _sigmoid(x):
    return 1.0 / (1.0 + jnp.exp(-x))


def _dot(a, b):
    return jnp.dot(a, b, preferred_element_type=F32)


def _dot_nt(a, b):
    return lax.dot_general(a, b, (((1,), (1,)), ((), ())), preferred_element_type=F32)


def _swiglu_down(xb, wg, wu, wd):
    gate = _dot(xb, wg)
    up = _dot(xb, wu)
    act = (gate * _sigmoid(gate) * up).astype(BF16)
    return _dot(act, wd)


def _ffn_ln_kernel(x_ref, wg_ref, wu_ref, wd_ref, wgt_ref, wut_ref, wdt_ref, g_ref, b_ref,
                   o_ref, xb_ref):
    j = pl.program_id(1)

    @pl.when(j == 0)
    def _():
        xb_ref[...] = x_ref[...].astype(BF16)
        o_ref[...] = jnp.zeros_like(o_ref)

    o_ref[...] += _swiglu_down(xb_ref[...], wg_ref[...], wu_ref[...], wd_ref[...])

    @pl.when(j == pl.num_programs(1) - 1)
    def _():
        acc = o_ref[...] + _swiglu_down(xb_ref[...], wgt_ref[...], wut_ref[...], wdt_ref[...])
        y = ALPHA * x_ref[...] + 0.5 * acc
        o_ref[...] = _layernorm_rows(y, g_ref[...], b_ref[...])


def _ffn_ln(x, weights, ln_g, ln_b):
    wg, wu, wd, wgt, wut, wdt = weights
    n, d = x.shape
    tm, tf = FFN_TM, FFN_TF
    nf = wd.shape[0] // tf
    ft = wdt.shape[0]
    once = dict(pipeline_mode=pl.Buffered(1))
    est = (2 * tm * d * 4 + 2 * tm * d * 4 + tm * d * 2
           + 2 * 3 * (d * tf * 2) + 3 * (d * ft * 2) + 3 * tm * tf * 4)
    return pl.pallas_call(
        _ffn_ln_kernel,
        grid=(n // tm, nf),
        in_specs=[
            pl.BlockSpec((tm, d), lambda i, j: (i, 0)),
            pl.BlockSpec((d, tf), lambda i, j: (0, j)),
            pl.BlockSpec((d, tf), lambda i, j: (0, j)),
            pl.BlockSpec((tf, d), lambda i, j: (j, 0)),
            pl.BlockSpec((d, ft), lambda i, j: (0, 0), **once),
            pl.BlockSpec((d, ft), lambda i, j: (0, 0), **once),
            pl.BlockSpec((ft, d), lambda i, j: (0, 0), **once),
            pl.BlockSpec((1, d), lambda i, j: (0, 0)),
            pl.BlockSpec((1, d), lambda i, j: (0, 0)),
        ],
        out_specs=pl.BlockSpec((tm, d), lambda i, j: (i, 0)),
        out_shape=jax.ShapeDtypeStruct((n, d), F32),
        scratch_shapes=[pltpu.VMEM((tm, d), BF16)],
        compiler_params=pltpu.CompilerParams(
            dimension_semantics=("parallel", "arbitrary"),
            vmem_limit_bytes=_vmem_limit(est)),
        name="ffn_ln",
    )(x, wg, wu, wd, wgt, wut, wdt, ln_g, ln_b)


def _prep_ffn_weights(w_gu, w_down):
    wg = w_gu[:, :D_FF].astype(BF16)
    wu = w_gu[:, D_FF:].astype(BF16)
    wd = w_down.astype(BF16)
    full = (D_FF // FFN_TF) * FFN_TF
    return wg, wu, wd, wg[:, full:], wu[:, full:], wd[full:]


def _in_proj_kernel(x_ref, w_ref, o_ref, xb_ref):
    @pl.when(pl.program_id(1) == 0)
    def _():
        xb_ref[...] = x_ref[...].astype(BF16)

    o_ref[...] = _dot(xb_ref[...], w_ref[...]).astype(o_ref.dtype)


def _in_proj(x, w, *, tm, tn, out_dtype):
    n, k = x.shape
    m = w.shape[1]
    est = 2 * tm * k * 4 + tm * k * 2 + 2 * k * tn * 2 + 2 * tm * tn * 2 + tm * tn * 4
    return pl.pallas_call(
        _in_proj_kernel,
        grid=(n // tm, m // tn),
        in_specs=[pl.BlockSpec((tm, k), lambda i, j: (i, 0)),
                  pl.BlockSpec((k, tn), lambda i, j: (0, j))],
        out_specs=pl.BlockSpec((tm, tn), lambda i, j: (i, j)),
        out_shape=jax.ShapeDtypeStruct((n, m), out_dtype),
        scratch_shapes=[pltpu.VMEM((tm, k), BF16)],
        compiler_params=pltpu.CompilerParams(
            dimension_semantics=("parallel", "arbitrary"),
            vmem_limit_bytes=_vmem_limit(est)),
        name="in_proj",
    )(x, w)


def _ssm_in_kernel(x_ref, wt_ref, o_ref):
    xb = x_ref[...].astype(BF16)
    ut = _dot_nt(wt_ref[...], xb)
    o_ref[...] = ut.astype(BF16).reshape(o_ref.shape)


def _ssm_in(x, wt, *, T):
    n, d = x.shape
    G, C = SSM_GROUPS, SSM_GROUP
    nc = n // T
    est = 2 * nc * d * 4 + nc * d * 2 + 2 * G * C * d * 2 + 2 * G * C * nc * 2 + G * C * nc * 4
    return pl.pallas_call(
        _ssm_in_kernel,
        grid=(T,),
        in_specs=[pl.BlockSpec((nc, d), lambda r: (0, r)),
                  pl.BlockSpec((G * C, d), lambda r: (0, 0))],
        out_specs=pl.BlockSpec((G, None, C, nc), lambda r: (0, r, 0, 0)),
        out_shape=jax.ShapeDtypeStruct((G, T, C, nc), BF16),
        compiler_params=pltpu.CompilerParams(
            dimension_semantics=("parallel",), vmem_limit_bytes=_vmem_limit(est)),
        name="ssm_in",
    )(x.reshape(nc, T * d), wt)


def _cmul(x, y):
    return x[0] * y[0] - x[1] * y[1], x[0] * y[1] + x[1] * y[0]


def _split_hi_lo(x):
    hi = x.astype(BF16)
    lo = (x - hi.astype(F32)).astype(BF16)
    return hi, lo


def _dot_hi_lo(a, b):
    ah, al = _split_hi_lo(a)
    bh, bl = _split_hi_lo(b)
    return _dot(ah, bh) + _dot(al, bh) + _dot(ah, bl)


def _ssm_tables_kernel(pcol_ref, prow_ref, bcat_ref, cre_ref, cim_ref, dtile_ref,
                       mt_ref, gm_ref, cs_ref, q_ref, *, T):
    P, C = SSM_STATE, SSM_GROUP
    rpt = V7X_LANES // C
    nt = T // rpt
    wide = 2 * T * C

    lane = lax.broadcasted_iota(jnp.int32, (1, V7X_LANES), 1)
    rr = lax.shift_right_logical(lane, 4).astype(F32)
    lanef = lane.astype(F32)
    expand = (lax.broadcasted_iota(jnp.int32, (C, V7X_LANES), 0)
              == (lax.broadcasted_iota(jnp.int32, (C, V7X_LANES), 1) & (C - 1))).astype(F32)

    pc = pcol_ref[...]

    def powers(d, k):
        dt = jnp.exp(pc[:, 4 + d:5 + d])
        zr = pc[:, d:d + 1] * dt
        zi = pc[:, 2 + d:3 + d] * dt
        mag = jnp.exp(k * zr)
        ang = k * zi
        return mag * jnp.cos(ang), mag * jnp.sin(ang)

    def zoh(d, pt):
        are = pc[:, d:d + 1]
        aim = pc[:, 2 + d:3 + d]
        nr = pt[0][:, 1:2] - 1.0
        ni = pt[1][:, 1:2]
        den = are * are + aim * aim
        return (nr * are + ni * aim) / den, (ni * are - nr * aim) / den

    def col(pt, m):
        return pt[0][:, m:m + 1], pt[1][:, m:m + 1]

    dsc_f, pt_f = powers(0, (rpt - 1.0) - rr), powers(0, lanef)
    asc_b, pt_b = powers(1, rr), powers(1, lanef)
    btile = jnp.dot(bcat_ref[...], expand, precision=lax.Precision.HIGHEST,
                    preferred_element_type=F32)
    bt = (btile[0:P], btile[P:2 * P])
    bb_f = _cmul(zoh(0, pt_f), bt)
    bd_f = _cmul(dsc_f, bb_f)
    ba_b = _cmul(asc_b, _cmul(zoh(1, pt_b), bt))

    for j in range(nt):
        cols = slice(j * V7X_LANES, (j + 1) * V7X_LANES)
        xf = _cmul(bd_f, col(pt_f, T - rpt - rpt * j))
        xb = _cmul(ba_b, col(pt_b, rpt * j))
        gm_ref[0:P, cols] = xf[0].astype(BF16)
        gm_ref[P:2 * P, cols] = xf[1].astype(BF16)
        gm_ref[2 * P:3 * P, cols] = xb[0].astype(BF16)
        gm_ref[3 * P:4 * P, cols] = xb[1].astype(BF16)

    zeros = jnp.zeros((2 * P, V7X_LANES), F32)
    for j in range(2 * nt):
        cols = slice(j * V7X_LANES, (j + 1) * V7X_LANES)
        if j < nt:
            qf = _cmul(bd_f, col(pt_f, T - rpt * j - (rpt - 1)))
            q_ref[0:P, cols] = qf[0]
            q_ref[P:2 * P, cols] = qf[1]
            q_ref[2 * P:4 * P, cols] = zeros
        else:
            qb = _cmul(ba_b, col(pt_b, rpt * j - T))
            q_ref[2 * P:3 * P, cols] = qb[0]
            q_ref[3 * P:4 * P, cols] = qb[1]
            if j == nt:
                center = rr == 0.0
                q_ref[0:P, cols] = jnp.where(center, bb_f[0], 0.0)
                q_ref[P:2 * P, cols] = jnp.where(center, bb_f[1], 0.0)
            else:
                q_ref[0:2 * P, cols] = zeros

    lane4 = lax.broadcasted_iota(jnp.int32, (1, 4 * P), 1)
    is_re = (lax.shift_right_logical(lane4, 6) & 1) == 0
    is_f = lane4 < 2 * P
    cre = cre_ref[...]
    cim = cim_ref[...]
    pr = prow_ref[...]
    dt = jnp.exp(pr[2:3])
    zr, zi = pr[0:1] * dt, pr[1:2] * dt

    rowi = lax.broadcasted_iota(jnp.int32, (T, 4 * P), 0)
    kmat = jnp.where(is_f, rowi + 1, T - rowi).astype(F32)
    mag = jnp.exp(kmat * zr)
    pw_r, pw_i = mag * jnp.cos(kmat * zi), mag * jnp.sin(kmat * zi)
    for r in range(T):
        prr, pii = pw_r[r:r + 1], pw_i[r:r + 1]
        blk = jnp.where(is_re, cre * prr - cim * pii, -(cre * pii + cim * prr))
        cs_ref[r * C:(r + 1) * C, :] = blk.astype(BF16)

    lhs = jnp.where(is_re, cre, -cim)
    zt = _dot_hi_lo(lhs, q_ref[...])
    lanew = lax.broadcasted_iota(jnp.int32, (C, wide), 1)
    roww = lax.broadcasted_iota(jnp.int32, (C, wide), 0)
    diag = (lax.shift_right_logical(lanew, 4) == T) & ((lanew & (C - 1)) == roww)
    zt = zt + jnp.where(diag, dtile_ref[...], 0.0)

    for r in range(T):
        off = (T - r) * C
        shifted = pltpu.roll(zt, (wide - off) % wide, 1)
        mt_ref[r * C:(r + 1) * C, :] = shifted[:, :T * C].astype(BF16)


def _ssm_tables(a_re, a_im, log_dt, b_re, b_im, c_re, c_im, d_skip, *, T):
    G, P, C = SSM_GROUPS, SSM_STATE, SSM_GROUP
    ldt = jnp.broadcast_to(log_dt[:, :, None], (2, G, P))
    zc = jnp.zeros((G, P), F32)
    pcol = jnp.stack([a_re[0], a_re[1], a_im[0], a_im[1], ldt[0], ldt[1], zc, zc], axis=-1)

    def row4(x):
        return jnp.concatenate([x[0], x[0], x[1], x[1]], axis=-1)

    zr4 = jnp.zeros((G, 4 * P), F32)
    prow = jnp.stack([row4(a_re), row4(a_im), row4(ldt), zr4, zr4, zr4, zr4, zr4], axis=1)
    bcat = jnp.concatenate([b_re, b_im], axis=1)
    cre4 = row4(c_re)
    cim4 = row4(c_im)
    dtile = jnp.tile(d_skip, (1, 2 * T))[:, None, :]

    tc = T * C
    return pl.pallas_call(
        functools.partial(_ssm_tables_kernel, T=T),
        grid=(G,),
        in_specs=[
            pl.BlockSpec((None, P, 8), lambda g: (g, 0, 0)),
            pl.BlockSpec((None, 8, 4 * P), lambda g: (g, 0, 0)),
            pl.BlockSpec((None, 2 * P, C), lambda g: (g, 0, 0)),
            pl.BlockSpec((None, C, 4 * P), lambda g: (g, 0, 0)),
            pl.BlockSpec((None, C, 4 * P), lambda g: (g, 0, 0)),
            pl.BlockSpec((None, 1, 2 * tc), lambda g: (g, 0, 0)),
        ],
        out_specs=[
            pl.BlockSpec((None, tc, tc), lambda g: (g, 0, 0)),
            pl.BlockSpec((None, 4 * P, tc), lambda g: (g, 0, 0)),
            pl.BlockSpec((None, tc, 4 * P), lambda g: (g, 0, 0)),
        ],
        out_shape=[
            jax.ShapeDtypeStruct((G, tc, tc), BF16),
            jax.ShapeDtypeStruct((G, 4 * P, tc), BF16),
            jax.ShapeDtypeStruct((G, tc, 4 * P), BF16),
        ],
        scratch_shapes=[pltpu.VMEM((4 * P, 2 * tc), F32)],
        compiler_params=pltpu.CompilerParams(dimension_semantics=("parallel",)),
        name="ssm_tables",
    )(pcol, prow, bcat, cre4, cim4, dtile)


def _ssm_state_kernel(u_ref, gm_ref, o_ref, *, gb, ns):
    nc = u_ref.shape[-1]
    for gi in range(gb):
        u = u_ref[gi].reshape(-1, nc)
        o_ref[:, gi * ns:(gi + 1) * ns] = _dot(gm_ref[gi], u).T


def _ssm_state(ut, gm):
    G, T, C, nc = ut.shape
    ns = gm.shape[1]
    gb = SSM_GROUPS_PER_STEP
    return pl.pallas_call(
        functools.partial(_ssm_state_kernel, gb=gb, ns=ns),
        grid=(G // gb,),
        in_specs=[pl.BlockSpec((gb, T, C, nc), lambda s: (s, 0, 0, 0)),
                  pl.BlockSpec((gb, ns, T * C), lambda s: (s, 0, 0))],
        out_specs=pl.BlockSpec((nc, gb * ns), lambda s: (0, s)),
        out_shape=jax.ShapeDtypeStruct((nc, G * ns), F32),
        compiler_params=pltpu.CompilerParams(dimension_semantics=("parallel",)),
        name="ssm_state",
    )(ut, gm)


def _ssm_scan_kernel(are_ref, aim_ref, ldt_ref, s_ref, o_ref, x_ref, xs_ref, *, T, n0, seq_starts, seq_ends):
    d = pl.program_id(0)
    j = pl.program_id(1)
    nblk = pl.num_programs(1)
    blk = j + d * (nblk - 1 - 2 * j)
    P = SSM_STATE

    is_start = functools.reduce(jnp.logical_or, [blk == s for s in seq_starts])
    is_end = functools.reduce(jnp.logical_or, [blk == e for e in seq_ends])
    reset = jnp.where(d == 0, is_start, is_end)

    @pl.when(reset)
    def _():
        x_ref[...] = jnp.zeros_like(x_ref)
        xs_ref[...] = jnp.zeros_like(xs_ref)

    dt = jnp.exp(ldt_ref[...])
    zr = are_ref[...] * dt
    zi = aim_ref[...] * dt
    mag = jnp.exp(float(T) * zr)
    mr = mag * jnp.cos(float(T) * zi)
    mi = mag * jnp.sin(float(T) * zi)
    lane = lax.broadcasted_iota(jnp.int32, mr.shape, 1)
    m2 = jnp.where(lane < P, -mi, mi)
    m2s = -m2

    def body(k, carry):
        x, xs = carry
        row = k + d * (n0 - 1 - 2 * k)
        loc = s_ref[row]
        o_ref[row] = x.astype(o_ref.dtype)
        locs = pltpu.roll(loc, P, 1)
        return x * mr + xs * m2 + loc, xs * mr + x * m2s + locs

    x, xs = lax.fori_loop(0, n0, body, (x_ref[...], xs_ref[...]), unroll=4)
    x_ref[...] = x
    xs_ref[...] = xs


def _ssm_scan(s_loc, a_re, a_im, log_dt, *, T, n_unit_seqs):
    G, P = SSM_GROUPS, SSM_STATE
    nc = s_loc.shape[0]
    n0 = SEQ_BLOCK // T
    nblk = nc // n0
    seq_starts = tuple(range(n_unit_seqs + 1))
    seq_ends = tuple(range(n_unit_seqs)) + (nblk - 1,)
    dup = lambda x: jnp.concatenate([x, x], axis=-1)
    are2, aim2 = dup(a_re), dup(a_im)
    ldt2 = jnp.broadcast_to(log_dt[:, :, None], (2, G, 2 * P))

    def blk_map(d, j):
        return (j + d * (nblk - 1 - 2 * j), 0, d)

    par = pl.BlockSpec((None, G, 2 * P), lambda d, j: (d, 0, 0))
    return pl.pallas_call(
        functools.partial(_ssm_scan_kernel, T=T, n0=n0, seq_starts=seq_starts, seq_ends=seq_ends),
        grid=(2, nblk),
        in_specs=[par, par, par, pl.BlockSpec((n0, G, 2 * P), blk_map)],
        out_specs=pl.BlockSpec((n0, G, 2 * P), blk_map),
        out_shape=jax.ShapeDtypeStruct((nc, G, 4 * P), BF16),
        scratch_shapes=[pltpu.VMEM((G, 2 * P), F32), pltpu.VMEM((G, 2 * P), F32)],
        compiler_params=pltpu.CompilerParams(dimension_semantics=("arbitrary", "arbitrary")),
        name="ssm_scan",
    )(are2, aim2, ldt2, s_loc)


def _ssm_out_kernel(u_ref, mt_ref, s_ref, cs_ref, y_ref, *, gb, ns):
    nc = u_ref.shape[-1]
    for gi in range(gb):
        u = u_ref[gi].reshape(-1, nc)
        y = _dot(mt_ref[gi], u) + _dot_nt(cs_ref[gi], s_ref[:, gi * ns:(gi + 1) * ns])
        y_ref[gi] = _gelu_tanh(y).astype(y_ref.dtype).reshape(y_ref.shape[1:])


def _ssm_out(ut, mt, s_in, cs):
    G, T, C, nc = ut.shape
    ns = cs.shape[2]
    gb = SSM_GROUPS_PER_STEP
    return pl.pallas_call(
        functools.partial(_ssm_out_kernel, gb=gb, ns=ns),
        grid=(G // gb,),
        in_specs=[pl.BlockSpec((gb, T, C, nc), lambda s: (s, 0, 0, 0)),
                  pl.BlockSpec((gb, T * C, T * C), lambda s: (s, 0, 0)),
                  pl.BlockSpec((nc, gb * ns), lambda s: (0, s)),
                  pl.BlockSpec((gb, T * C, ns), lambda s: (s, 0, 0))],
        out_specs=pl.BlockSpec((gb, T, C, nc), lambda s: (s, 0, 0, 0)),
        out_shape=jax.ShapeDtypeStruct((G, T, C, nc), BF16),
        compiler_params=pltpu.CompilerParams(dimension_semantics=("parallel",)),
        name="ssm_out",
    )(ut, mt, s_in, cs)


def _ssm_tok_kernel(y_ref, o_ref):
    nc = y_ref.shape[-1]
    yt = y_ref[...].reshape(-1, nc)
    eye = (lax.broadcasted_iota(jnp.int32, (nc, nc), 0)
           == lax.broadcasted_iota(jnp.int32, (nc, nc), 1)).astype(BF16)
    o_ref[...] = _dot_nt(eye, yt).astype(o_ref.dtype)


def _ssm_tok(yt):
    G, T, C, nc = yt.shape
    out = pl.pallas_call(
        _ssm_tok_kernel,
        grid=(T,),
        in_specs=[pl.BlockSpec((G, None, C, nc), lambda r: (0, r, 0, 0))],
        out_specs=pl.BlockSpec((nc, G * C), lambda r: (0, r)),
        out_shape=jax.ShapeDtypeStruct((nc, T * G * C), BF16),
        compiler_params=pltpu.CompilerParams(dimension_semantics=("parallel",)),
        name="ssm_tok",
    )(yt)
    return out.reshape(nc * T, G * C)


def _s5_mixer_gelu(x, wt, a_re, a_im, log_dt, b_re, b_im, c_re, c_im, d_skip, *, n_unit_seqs):
    T, G, P = SSM_CHUNK, SSM_GROUPS, SSM_STATE
    nc = x.shape[0] // T
    mt, gm, cs = _ssm_tables(a_re, a_im, log_dt, b_re, b_im, c_re, c_im, d_skip, T=T)
    ut = _ssm_in(x, wt, T=T)
    s_loc = _ssm_state(ut, gm)
    s_in = _ssm_scan(s_loc.reshape(nc, G, 4 * P), a_re, a_im, log_dt, T=T, n_unit_seqs=n_unit_seqs)
    yt = _ssm_out(ut, mt, s_in.reshape(nc, G * 4 * P), cs)
    return _ssm_tok(yt)


def _kv_kernel(m_ref, g_ref, b_ref, w_ref, o_ref, mb_ref):
    @pl.when(pl.program_id(0) == 0)
    def _():
        mb_ref[...] = _layernorm_rows(m_ref[...], g_ref[...], b_ref[...]).astype(BF16)

    o_ref[...] = _dot(mb_ref[...], w_ref[...]).astype(o_ref.dtype)


def _kv_proj(mem, ln_g, ln_b, w_kv):
    n, d = mem.shape
    m = w_kv.shape[1]
    tn = KV_TN
    est = 2 * n * d * 4 + n * d * 2 + 2 * d * tn * 2 + 2 * n * tn * 2 + 3 * n * d * 4
    return pl.pallas_call(
        _kv_kernel,
        grid=(m // tn,),
        in_specs=[pl.BlockSpec((n, d), lambda j: (0, 0)),
                  pl.BlockSpec((1, d), lambda j: (0, 0)),
                  pl.BlockSpec((1, d), lambda j: (0, 0)),
                  pl.BlockSpec((d, tn), lambda j: (0, j))],
        out_specs=pl.BlockSpec((n, tn), lambda j: (0, j)),
        out_shape=jax.ShapeDtypeStruct((n, m), BF16),
        scratch_shapes=[pltpu.VMEM((n, d), BF16)],
        compiler_params=pltpu.CompilerParams(
            dimension_semantics=("arbitrary",), vmem_limit_bytes=_vmem_limit(est)),
        name="kv_proj",
    )(mem, ln_g, ln_b, w_kv)


def _mixer_kernel(ya_ref, ug_ref, vg_ref, q_ref, g0_ref, g1_ref, g2_ref, kv_ref,
                  ws_ref, bs_ref, lng_ref, lnb_ref, wa_ref, wb_ref, wgp_ref, wap_ref,
                  o_ref, ub_ref, ob_ref, *, tm):
    n = pl.program_id(1)

    @pl.when(n == 0)
    def _():
        hd = GMLP_HEAD_DIM

        def chunk_body(ci, carry):
            r0 = pl.multiple_of(ci * GMLP_CHUNK, GMLP_CHUNK)
            rows = pl.ds(r0, GMLP_CHUNK)
            u = _gelu_tanh(ug_ref[rows, :].astype(F32))
            v = _gelu_tanh(vg_ref[rows, :].astype(F32))
            v = _layernorm_rows(v, lng_ref[...], lnb_ref[...]).astype(BF16)
            for h in range(GMLP_HEADS):
                cols = slice(h * hd, (h + 1) * hd)
                mixed = _dot(ws_ref[h], v[:, cols]) + bs_ref[:, h:h + 1]
                ub_ref[rows, cols] = (u[:, cols] * mixed).astype(BF16)
            return carry

        lax.fori_loop(0, tm // GMLP_CHUNK, chunk_body, 0)

        ad = ATTN_HEAD_DIM
        scale = ad ** -0.5
        for h in range(ATTN_HEADS):
            cols = slice(h * ad, (h + 1) * ad)
            vcols = slice(ATTN_WIDTH + h * ad, ATTN_WIDTH + (h + 1) * ad)
            s = _dot_nt(q_ref[:, cols], kv_ref[:, cols]) * scale
            s = s - jnp.max(s, axis=-1, keepdims=True)
            p = jnp.exp(s)
            p = p / jnp.sum(p, axis=-1, keepdims=True)
            ob_ref[:, cols] = _dot(p.astype(BF16), kv_ref[:, vcols]).astype(BF16)

    ya = ya_ref[...]
    y_a = _dot(ya, wa_ref[...]) * _sigmoid(_dot(ya, wb_ref[...]))
    y_b = _dot(ub_ref[...], wgp_ref[...])
    y_c = _dot(ob_ref[...], wap_ref[...])
    merged = (_sigmoid(g0_ref[...].astype(F32)) * y_a
              + _sigmoid(g1_ref[...].astype(F32)) * y_b
              + _sigmoid(g2_ref[...].astype(F32)) * y_c)
    o_ref[...] = merged.astype(o_ref.dtype)


def _mixer(ya, proj, kv, w_s, b_s, gln_g, gln_b, w_glu, w_gproj, w_aproj):
    n = ya.shape[0]
    d = D_MODEL
    tm, tn = MIX_TM, MIX_TN
    nn = d // tn
    wid = GMLP_WIDTH
    blocks_per_seq = SEQ_BLOCK // tm
    n_mem_batches = kv.shape[0] // N_MEM

    def kv_map(i, j):
        return (jnp.minimum(i // blocks_per_seq, n_mem_batches - 1), 0)

    def gate_spec(br):
        base = (OFF_GATE + br * d) // tn
        return pl.BlockSpec((tm, tn), lambda i, j: (i, base + j))

    row_blk = lambda c: pl.BlockSpec((tm, wid), lambda i, j: (i, c))
    full2 = lambda a: pl.BlockSpec(a.shape, lambda i, j: (0, 0))
    est = (2 * 4 * tm * wid * 2 + 2 * 3 * tm * tn * 2 + 2 * N_MEM * 2 * ATTN_WIDTH * 2
           + 2 * 4 * wid * tn * 2 + 2 * tm * tn * 2 + 2 * tm * wid * 2
           + 6 * tm * tn * 4 + 4 * GMLP_CHUNK * wid * 4 + 3 * tm * N_MEM * 4)
    return pl.pallas_call(
        functools.partial(_mixer_kernel, tm=tm),
        grid=(n // tm, nn),
        in_specs=[
            pl.BlockSpec((tm, wid), lambda i, j: (i, 0)),
            row_blk(OFF_GU // wid), row_blk(OFF_GV // wid), row_blk(OFF_Q // wid),
            gate_spec(0), gate_spec(1), gate_spec(2),
            pl.BlockSpec((N_MEM, 2 * ATTN_WIDTH), kv_map),
            pl.BlockSpec(w_s.shape, lambda i, j: (0, 0, 0)),
            full2(b_s), full2(gln_g), full2(gln_b),
            pl.BlockSpec((wid, tn), lambda i, j: (0, j)),
            pl.BlockSpec((wid, tn), lambda i, j: (0, nn + j)),
            pl.BlockSpec((wid, tn), lambda i, j: (0, j)),
            pl.BlockSpec((wid, tn), lambda i, j: (0, j)),
        ],
        out_specs=pl.BlockSpec((tm, tn), lambda i, j: (i, j)),
        out_shape=jax.ShapeDtypeStruct((n, d), BF16),
        scratch_shapes=[pltpu.VMEM((tm, wid), BF16), pltpu.VMEM((tm, wid), BF16)],
        compiler_params=pltpu.CompilerParams(
            dimension_semantics=("parallel", "arbitrary"),
            vmem_limit_bytes=_vmem_limit(est)),
        name="mixer",
    )(ya, proj, proj, proj, proj, proj, proj, kv, w_s, b_s, gln_g, gln_b,
      w_glu, w_glu, w_gproj, w_aproj)


def _outproj_ln_kernel(m_ref, w_ref, x_ref, g_ref, b_ref, o_ref):
    y = ALPHA * x_ref[...] + _dot(m_ref[...], w_ref[...])
    o_ref[...] = _layernorm_rows(y, g_ref[...], b_ref[...])


def _outproj_ln(merged, w_out, x, ln_g, ln_b):
    n, d = x.shape
    tm = OUT_TM
    est = 2 * tm * d * 2 + 2 * d * d * 2 + 4 * tm * d * 4 + 2 * tm * d * 4
    return pl.pallas_call(
        _outproj_ln_kernel,
        grid=(n // tm,),
        in_specs=[pl.BlockSpec((tm, d), lambda i: (i, 0)),
                  pl.BlockSpec((d, d), lambda i: (0, 0)),
                  pl.BlockSpec((tm, d), lambda i: (i, 0)),
                  pl.BlockSpec((1, d), lambda i: (0, 0)),
                  pl.BlockSpec((1, d), lambda i: (0, 0))],
        out_specs=pl.BlockSpec((tm, d), lambda i: (i, 0)),
        out_shape=jax.ShapeDtypeStruct((n, d), F32),
        compiler_params=pltpu.CompilerParams(
            dimension_semantics=("parallel",), vmem_limit_bytes=_vmem_limit(est)),
        name="outproj_ln",
    )(merged, w_out, x, ln_g, ln_b)


def _layer(x, mem, n_unit_seqs, ffn1_w_gu, ffn1_w_down, ln1_g, ln1_b, w_in,
           ssm_a_re, ssm_a_im, ssm_log_dt, ssm_b_re, ssm_b_im, ssm_c_re, ssm_c_im, ssm_d, ssm_w_glu,
           gmlp_ln_g, gmlp_ln_b, gmlp_w_s, gmlp_b_s, gmlp_w_proj,
           mem_ln_g, mem_ln_b, attn_w_kv, attn_w_proj,
           w_out, ln2_g, ln2_b, ffn2_w_gu, ffn2_w_down, ln3_g, ln3_b):
    row = lambda v: v.reshape(1, -1)
    x1 = _ffn_ln(x, _prep_ffn_weights(ffn1_w_gu, ffn1_w_down), row(ln1_g), row(ln1_b))
    proj = _in_proj(x1, w_in[:, SSM_WIDTH:].astype(BF16), tm=PROJ_TM, tn=PROJ_TN, out_dtype=BF16)
    ya = _s5_mixer_gelu(x1, w_in[:, :SSM_WIDTH].T.astype(BF16), ssm_a_re, ssm_a_im, ssm_log_dt,
                        ssm_b_re, ssm_b_im, ssm_c_re, ssm_c_im, ssm_d, n_unit_seqs=n_unit_seqs)
    kv = _kv_proj(mem, row(mem_ln_g), row(mem_ln_b), attn_w_kv.astype(BF16))
    merged = _mixer(ya, proj, kv, gmlp_w_s.astype(BF16), gmlp_b_s, row(gmlp_ln_g), row(gmlp_ln_b),
                    ssm_w_glu.astype(BF16), gmlp_w_proj.astype(BF16), attn_w_proj.astype(BF16))
    x2 = _outproj_ln(merged, w_out.astype(BF16), x1, row(ln2_g), row(ln2_b))
    return _ffn_ln(x2, _prep_ffn_weights(ffn2_w_gu, ffn2_w_down), row(ln3_g), row(ln3_b))


def kernel(x_prompt, x_sample, mem_prompt, mem_sample, ffn1_w_gu, ffn1_w_down, ln1_g, ln1_b, w_in,
           ssm_a_re, ssm_a_im, ssm_log_dt, ssm_b_re, ssm_b_im, ssm_c_re, ssm_c_im, ssm_d, ssm_w_glu,
           gmlp_ln_g, gmlp_ln_b, gmlp_w_s, gmlp_b_s, gmlp_w_proj, mem_ln_g, mem_ln_b, attn_w_kv,
           attn_w_proj, w_out, ln2_g, ln2_b, ffn2_w_gu, ffn2_w_down, ln3_g, ln3_b):
    d = x_prompt.shape[-1]
    n_prompt = x_prompt.shape[0] * x_prompt.shape[1]
    assert x_prompt.shape[1] == SEQ_BLOCK and x_sample.shape[0] == 1
    assert x_sample.shape[1] % SEQ_BLOCK == 0
    x = jnp.concatenate([x_prompt.reshape(-1, d), x_sample.reshape(-1, d)], axis=0)
    mem = jnp.concatenate([mem_prompt.reshape(-1, d), mem_sample.reshape(-1, d)], axis=0)
    for l in range(DEPTH):
        x = _layer(x, mem, x_prompt.shape[0],
                   ffn1_w_gu[l], ffn1_w_down[l], ln1_g[l], ln1_b[l], w_in[l],
                   ssm_a_re[l], ssm_a_im[l], ssm_log_dt[l], ssm_b_re[l], ssm_b_im[l],
                   ssm_c_re[l], ssm_c_im[l], ssm_d[l], ssm_w_glu[l],
                   gmlp_ln_g[l], gmlp_ln_b[l], gmlp_w_s[l], gmlp_b_s[l], gmlp_w_proj[l],
                   mem_ln_g[l], mem_ln_b[l], attn_w_kv[l], attn_w_proj[l],
                   w_out[l], ln2_g[l], ln2_b[l], ffn2_w_gu[l], ffn2_w_down[l], ln3_g[l], ln3_b[l])
    y_prompt = x[:n_prompt].reshape(x_prompt.shape)
    y_sample = x[n_prompt:].reshape(x_sample.shape)
    return (y_prompt, y_sample)
```

```python
import functools
import math

import jax
import jax.numpy as jnp
from jax import lax
from jax.experimental import pallas as pl
from jax.experimental.pallas import tpu as pltpu

F32 = jnp.float32
BF16 = jnp.bfloat16

D_MODEL = 2048
DEPTH = 1
SEQ_BLOCK = 4096
N_MEM = 256
SSM_WIDTH = D_MODEL // 2
SSM_GROUP = 16
SSM_GROUPS = SSM_WIDTH // SSM_GROUP
SSM_STATE = 64
GMLP_WIDTH = D_MODEL // 2
GMLP_CHUNK = 128
GMLP_HEADS = 8
GMLP_HEAD_DIM = GMLP_WIDTH // GMLP_HEADS
ATTN_HEADS = 4
ATTN_HEAD_DIM = D_MODEL // 8
ATTN_WIDTH = ATTN_HEADS * ATTN_HEAD_DIM
D_FF = 5504
OFF_GU = 0
OFF_GV = OFF_GU + GMLP_WIDTH
OFF_Q = OFF_GV + GMLP_WIDTH
OFF_GATE = OFF_Q + ATTN_WIDTH
ALPHA = (2.0 * DEPTH) ** 0.25
LN_EPS = 1e-5

V7X_LANES = 128
V7X_VMEM_BYTES = 64 * 1024 * 1024

SSM_CHUNK = 32
SSM_GROUPS_PER_STEP = 8
SSM_TOK_CHUNKS = 128
FFN_TM = 512
FFN_TF = 1024
PROJ_TM = 1024
PROJ_TN = 1024
MIX_TM = 512
MIX_TN = 512
OUT_TM = 512
KV_TN = 512


def _vmem_limit(nbytes):
    return int(min(nbytes + (16 << 20), V7X_VMEM_BYTES - (4 << 20)))


def _layernorm_rows(y, g, b):
    mu = jnp.mean(y, axis=-1, keepdims=True)
    yc = y - mu
    var = jnp.mean(yc * yc, axis=-1, keepdims=True)
    return yc * lax.rsqrt(var + LN_EPS) * g + b


def _gelu_tanh(x):
    c = math.sqrt(2.0 / math.pi)
    return 0.5 * x * (1.0 + jnp.tanh(c * (x + 0.044715 * (x * x * x))))


def _sigmoid(x):
    return 1.0 / (1.0 + jnp.exp(-x))


def _dot(a, b):
    return jnp.dot(a, b, preferred_element_type=F32)


def _dot_nt(a, b):
    return lax.dot_general(a, b, (((1,), (1,)), ((), ())), preferred_element_type=F32)


def _swiglu_down(xb, wg, wu, wd):
    gate = _dot(xb, wg)
    up = _dot(xb, wu)
    act = (gate * _sigmoid(gate) * up).astype(BF16)
    return _dot(act, wd)


def _ffn_ln_kernel(x_ref, wg_ref, wu_ref, wd_ref, wgt_ref, wut_ref, wdt_ref, g_ref, b_ref,
                   o_ref, xb_ref):
    j = pl.program_id(1)

    @pl.when(j == 0)
    def _():
        xb_ref[...] = x_ref[...].astype(BF16)
        o_ref[...] = jnp.zeros_like(o_ref)

    o_ref[...] += _swiglu_down(xb_ref[...], wg_ref[...], wu_ref[...], wd_ref[...])

    @pl.when(j == pl.num_programs(1) - 1)
    def _():
        acc = o_ref[...] + _swiglu_down(xb_ref[...], wgt_ref[...], wut_ref[...], wdt_ref[...])
        y = ALPHA * x_ref[...] + 0.5 * acc
        o_ref[...] = _layernorm_rows(y, g_ref[...], b_ref[...])


def _ffn_ln(x, weights, ln_g, ln_b):
    wgu, wd, wgt, wut, wdt = weights
    n, d = x.shape
    tm, tf = FFN_TM, FFN_TF
    nf = wd.shape[0] // tf
    ft = wdt.shape[0]
    once = dict(pipeline_mode=pl.Buffered(1))
    est = (2 * tm * d * 4 + 2 * tm * d * 4 + tm * d * 2
           + 2 * 3 * (d * tf * 2) + 3 * (d * ft * 2) + 3 * tm * tf * 4)
    return pl.pallas_call(
        _ffn_ln_kernel,
        grid=(n // tm, nf),
        in_specs=[
            pl.BlockSpec((tm, d), lambda i, j: (i, 0)),
            pl.BlockSpec((d, tf), lambda i, j: (0, j)),
            pl.BlockSpec((pl.Element(d), pl.Element(tf)),
                         lambda i, j: (0, pl.multiple_of(D_FF + j * tf, V7X_LANES))),
            pl.BlockSpec((tf, d), lambda i, j: (j, 0)),
            pl.BlockSpec((d, ft), lambda i, j: (0, 0), **once),
            pl.BlockSpec((d, ft), lambda i, j: (0, 0), **once),
            pl.BlockSpec((ft, d), lambda i, j: (0, 0), **once),
            pl.BlockSpec((1, d), lambda i, j: (0, 0)),
            pl.BlockSpec((1, d), lambda i, j: (0, 0)),
        ],
        out_specs=pl.BlockSpec((tm, d), lambda i, j: (i, 0)),
        out_shape=jax.ShapeDtypeStruct((n, d), F32),
        scratch_shapes=[pltpu.VMEM((tm, d), BF16)],
        compiler_params=pltpu.CompilerParams(
            dimension_semantics=("parallel", "arbitrary"),
            vmem_limit_bytes=_vmem_limit(est)),
        name="ffn_ln",
    )(x, wgu, wgu, wd, wgt, wut, wdt, ln_g, ln_b)


def _prep_ffn_weights(w_gu, w_down):
    wgu = w_gu.astype(BF16)
    wd = w_down.astype(BF16)
    full = (D_FF // FFN_TF) * FFN_TF
    return wgu, wd, wgu[:, full:D_FF], wgu[:, D_FF + full:], wd[full:]


def _in_proj_kernel(x_ref, w_ref, o_ref, xb_ref):
    @pl.when(pl.program_id(1) == 0)
    def _():
        xb_ref[...] = x_ref[...].astype(BF16)

    o_ref[...] = _dot(xb_ref[...], w_ref[...]).astype(o_ref.dtype)


def _in_proj(x, w):
    n, k = x.shape
    tm, tn = PROJ_TM, PROJ_TN
    skip = SSM_WIDTH // tn
    m = w.shape[1] - SSM_WIDTH
    est = 2 * tm * k * 4 + 2 * tm * k * 2 + 2 * k * tn * 2 + 2 * tm * tn * 2 + tm * tn * 4
    return pl.pallas_call(
        _in_proj_kernel,
        grid=(n // tm, m // tn),
        in_specs=[pl.BlockSpec((tm, k), lambda i, j: (i, 0)),
                  pl.BlockSpec((k, tn), lambda i, j: (0, skip + j))],
        out_specs=[pl.BlockSpec((tm, tn), lambda i, j: (i, j)),
                   pl.BlockSpec((tm, k), lambda i, j: (i, 0))],
        out_shape=[jax.ShapeDtypeStruct((n, m), BF16),
                   jax.ShapeDtypeStruct((n, k), BF16)],
        compiler_params=pltpu.CompilerParams(
            dimension_semantics=("parallel", "arbitrary"),
            vmem_limit_bytes=_vmem_limit(est)),
        name="in_proj",
    )(x, w)


def _ssm_in_kernel(x_ref, wt_ref, o_ref):
    ut = _dot_nt(wt_ref[...], x_ref[...])
    o_ref[...] = ut.astype(BF16).reshape(o_ref.shape)


def _ssm_in(xc, wt, *, T):
    nc = xc.shape[0]
    d = xc.shape[1] // T
    G, C = SSM_GROUPS, SSM_GROUP
    est = 2 * nc * d * 2 + 2 * G * C * d * 2 + 2 * G * C * nc * 2 + G * C * nc * 4
    return pl.pallas_call(
        _ssm_in_kernel,
        grid=(T,),
        in_specs=[pl.BlockSpec((nc, d), lambda r: (0, r)),
                  pl.BlockSpec((G * C, d), lambda r: (0, 0))],
        out_specs=pl.BlockSpec((G, None, C, nc), lambda r: (0, r, 0, 0)),
        out_shape=jax.ShapeDtypeStruct((G, T, C, nc), BF16),
        compiler_params=pltpu.CompilerParams(
            dimension_semantics=("parallel",), vmem_limit_bytes=_vmem_limit(est)),
        name="ssm_in",
    )(xc, wt)


def _cmul(x, y):
    return x[0] * y[0] - x[1] * y[1], x[0] * y[1] + x[1] * y[0]


def _split_hi_lo(x):
    hi = x.astype(BF16)
    lo = (x - hi.astype(F32)).astype(BF16)
    return hi, lo


def _dot_hi_lo(a, b):
    ah, al = _split_hi_lo(a)
    bh, bl = _split_hi_lo(b)
    return _dot(ah, bh) + _dot(al, bh) + _dot(ah, bl)


def _ssm_tables_kernel(pcol_ref, prow_ref, bcat_ref, cre_ref, cim_ref, dtile_ref,
                       mt_ref, gm_ref, cs_ref, q_ref, *, T):
    P, C = SSM_STATE, SSM_GROUP
    rpt = V7X_LANES // C
    nt = T // rpt
    wide = 2 * T * C

    lane = lax.broadcasted_iota(jnp.int32, (1, V7X_LANES), 1)
    rr = lax.shift_right_logical(lane, 4).astype(F32)
    lanef = lane.astype(F32)
    expand = (lax.broadcasted_iota(jnp.int32, (C, V7X_LANES), 0)
              == (lax.broadcasted_iota(jnp.int32, (C, V7X_LANES), 1) & (C - 1))).astype(F32)

    pc = pcol_ref[...]

    def powers(d, k):
        dt = jnp.exp(pc[:, 4 + d:5 + d])
        zr = pc[:, d:d + 1] * dt
        zi = pc[:, 2 + d:3 + d] * dt
        mag = jnp.exp(k * zr)
        ang = k * zi
        return mag * jnp.cos(ang), mag * jnp.sin(ang)

    def zoh(d, pt):
        are = pc[:, d:d + 1]
        aim = pc[:, 2 + d:3 + d]
        nr = pt[0][:, 1:2] - 1.0
        ni = pt[1][:, 1:2]
        den = are * are + aim * aim
        return (nr * are + ni * aim) / den, (ni * are - nr * aim) / den

    def col(pt, m):
        return pt[0][:, m:m + 1], pt[1][:, m:m + 1]

    dsc_f, pt_f = powers(0, (rpt - 1.0) - rr), powers(0, lanef)
    asc_b, pt_b = powers(1, rr), powers(1, lanef)
    btile = jnp.dot(bcat_ref[...], expand, precision=lax.Precision.HIGHEST,
                    preferred_element_type=F32)
    bt = (btile[0:P], btile[P:2 * P])
    bb_f = _cmul(zoh(0, pt_f), bt)
    bd_f = _cmul(dsc_f, bb_f)
    ba_b = _cmul(asc_b, _cmul(zoh(1, pt_b), bt))

    for j in range(nt):
        cols = slice(j * V7X_LANES, (j + 1) * V7X_LANES)
        xf = _cmul(bd_f, col(pt_f, T - rpt - rpt * j))
        xb = _cmul(ba_b, col(pt_b, rpt * j))
        gm_ref[0:P, cols] = xf[0].astype(BF16)
        gm_ref[P:2 * P, cols] = xf[1].astype(BF16)
        gm_ref[2 * P:3 * P, cols] = xb[0].astype(BF16)
        gm_ref[3 * P:4 * P, cols] = xb[1].astype(BF16)

    zeros = jnp.zeros((2 * P, V7X_LANES), F32)
    for j in range(2 * nt):
        cols = slice(j * V7X_LANES, (j + 1) * V7X_LANES)
        if j < nt:
            qf = _cmul(bd_f, col(pt_f, T - rpt * j - (rpt - 1)))
            q_ref[0:P, cols] = qf[0]
            q_ref[P:2 * P, cols] = qf[1]
            q_ref[2 * P:4 * P, cols] = zeros
        else:
            qb = _cmul(ba_b, col(pt_b, rpt * j - T))
            q_ref[2 * P:3 * P, cols] = qb[0]
            q_ref[3 * P:4 * P, cols] = qb[1]
            if j == nt:
                center = rr == 0.0
                q_ref[0:P, cols] = jnp.where(center, bb_f[0], 0.0)
                q_ref[P:2 * P, cols] = jnp.where(center, bb_f[1], 0.0)
            else:
                q_ref[0:2 * P, cols] = zeros

    lane4 = lax.broadcasted_iota(jnp.int32, (1, 4 * P), 1)
    is_re = (lax.shift_right_logical(lane4, 6) & 1) == 0
    is_f = lane4 < 2 * P
    cre = cre_ref[...]
    cim = cim_ref[...]
    pr = prow_ref[...]
    dt = jnp.exp(pr[2:3])
    zr, zi = pr[0:1] * dt, pr[1:2] * dt

    rowi = lax.broadcasted_iota(jnp.int32, (T, 4 * P), 0)
    kmat = jnp.where(is_f, rowi + 1, T - rowi).astype(F32)
    mag = jnp.exp(kmat * zr)
    pw_r, pw_i = mag * jnp.cos(kmat * zi), mag * jnp.sin(kmat * zi)
    for r in range(T):
        prr, pii = pw_r[r:r + 1], pw_i[r:r + 1]
        blk = jnp.where(is_re, cre * prr - cim * pii, -(cre * pii + cim * prr))
        cs_ref[r * C:(r + 1) * C, :] = blk.astype(BF16)

    lhs = jnp.where(is_re, cre, -cim)
    zt = _dot_hi_lo(lhs, q_ref[...])
    lanew = lax.broadcasted_iota(jnp.int32, (C, wide), 1)
    roww = lax.broadcasted_iota(jnp.int32, (C, wide), 0)
    diag = (lax.shift_right_logical(lanew, 4) == T) & ((lanew & (C - 1)) == roww)
    zt = zt + jnp.where(diag, dtile_ref[...], 0.0)

    for r in range(T):
        off = (T - r) * C
        shifted = pltpu.roll(zt, (wide - off) % wide, 1)
        mt_ref[r * C:(r + 1) * C, :] = shifted[:, :T * C].astype(BF16)


def _ssm_tables(a_re, a_im, log_dt, b_re, b_im, c_re, c_im, d_skip, *, T):
    G, P, C = SSM_GROUPS, SSM_STATE, SSM_GROUP
    ldt = jnp.broadcast_to(log_dt[:, :, None], (2, G, P))
    zc = jnp.zeros((G, P), F32)
    pcol = jnp.stack([a_re[0], a_re[1], a_im[0], a_im[1], ldt[0], ldt[1], zc, zc], axis=-1)

    def row4(x):
        return jnp.concatenate([x[0], x[0], x[1], x[1]], axis=-1)

    zr4 = jnp.zeros((G, 4 * P), F32)
    prow = jnp.stack([row4(a_re), row4(a_im), row4(ldt), zr4, zr4, zr4, zr4, zr4], axis=1)
    bcat = jnp.concatenate([b_re, b_im], axis=1)
    cre4 = row4(c_re)
    cim4 = row4(c_im)
    dtile = jnp.tile(d_skip, (1, 2 * T))[:, None, :]

    tc = T * C
    return pl.pallas_call(
        functools.partial(_ssm_tables_kernel, T=T),
        grid=(G,),
        in_specs=[
            pl.BlockSpec((None, P, 8), lambda g: (g, 0, 0)),
            pl.BlockSpec((None, 8, 4 * P), lambda g: (g, 0, 0)),
            pl.BlockSpec((None, 2 * P, C), lambda g: (g, 0, 0)),
            pl.BlockSpec((None, C, 4 * P), lambda g: (g, 0, 0)),
            pl.BlockSpec((None, C, 4 * P), lambda g: (g, 0, 0)),
            pl.BlockSpec((None, 1, 2 * tc), lambda g: (g, 0, 0)),
        ],
        out_specs=[
            pl.BlockSpec((None, tc, tc), lambda g: (g, 0, 0)),
            pl.BlockSpec((None, 4 * P, tc), lambda g: (g, 0, 0)),
            pl.BlockSpec((None, tc, 4 * P), lambda g: (g, 0, 0)),
        ],
        out_shape=[
            jax.ShapeDtypeStruct((G, tc, tc), BF16),
            jax.ShapeDtypeStruct((G, 4 * P, tc), BF16),
            jax.ShapeDtypeStruct((G, tc, 4 * P), BF16),
        ],
        scratch_shapes=[pltpu.VMEM((4 * P, 2 * tc), F32)],
        compiler_params=pltpu.CompilerParams(dimension_semantics=("parallel",)),
        name="ssm_tables",
    )(pcol, prow, bcat, cre4, cim4, dtile)


def _ssm_state_kernel(u_ref, gm_ref, o_ref, *, gb, ns):
    nc = u_ref.shape[-1]
    for gi in range(gb):
        u = u_ref[gi].reshape(-1, nc)
        o_ref[:, gi * ns:(gi + 1) * ns] = _dot(gm_ref[gi], u).T


def _ssm_state(ut, gm):
    G, T, C, nc = ut.shape
    ns = gm.shape[1]
    gb = SSM_GROUPS_PER_STEP
    return pl.pallas_call(
        functools.partial(_ssm_state_kernel, gb=gb, ns=ns),
        grid=(G // gb,),
        in_specs=[pl.BlockSpec((gb, T, C, nc), lambda s: (s, 0, 0, 0)),
                  pl.BlockSpec((gb, ns, T * C), lambda s: (s, 0, 0))],
        out_specs=pl.BlockSpec((nc, gb * ns), lambda s: (0, s)),
        out_shape=jax.ShapeDtypeStruct((nc, G * ns), F32),
        compiler_params=pltpu.CompilerParams(dimension_semantics=("parallel",)),
        name="ssm_state",
    )(ut, gm)


def _ssm_scan_kernel(are_ref, aim_ref, ldt_ref, s_ref, o_ref, x_ref, xs_ref, *, T, n0, seq_starts, seq_ends):
    d = pl.program_id(0)
    j = pl.program_id(1)
    nblk = pl.num_programs(1)
    blk = j + d * (nblk - 1 - 2 * j)
    P = SSM_STATE

    is_start = functools.reduce(jnp.logical_or, [blk == s for s in seq_starts])
    is_end = functools.reduce(jnp.logical_or, [blk == e for e in seq_ends])
    reset = jnp.where(d == 0, is_start, is_end)

    @pl.when(reset)
    def _():
        x_ref[...] = jnp.zeros_like(x_ref)
        xs_ref[...] = jnp.zeros_like(xs_ref)

    dt = jnp.exp(ldt_ref[...])
    zr = are_ref[...] * dt
    zi = aim_ref[...] * dt
    mag = jnp.exp(float(T) * zr)
    mr = mag * jnp.cos(float(T) * zi)
    mi = mag * jnp.sin(float(T) * zi)
    lane = lax.broadcasted_iota(jnp.int32, mr.shape, 1)
    m2 = jnp.where(lane < P, -mi, mi)
    m2s = -m2

    def body(k, carry):
        x, xs = carry
        row = k + d * (n0 - 1 - 2 * k)
        loc = s_ref[row]
        o_ref[row] = x.astype(o_ref.dtype)
        locs = pltpu.roll(loc, P, 1)
        return x * mr + xs * m2 + loc, xs * mr + x * m2s + locs

    x, xs = lax.fori_loop(0, n0, body, (x_ref[...], xs_ref[...]), unroll=4)
    x_ref[...] = x
    xs_ref[...] = xs


def _ssm_scan(s_loc, a_re, a_im, log_dt, *, T, n_unit_seqs):
    G, P = SSM_GROUPS, SSM_STATE
    nc = s_loc.shape[0]
    n0 = SEQ_BLOCK // T
    nblk = nc // n0
    seq_starts = tuple(range(n_unit_seqs + 1))
    seq_ends = tuple(range(n_unit_seqs)) + (nblk - 1,)
    dup = lambda x: jnp.concatenate([x, x], axis=-1)
    are2, aim2 = dup(a_re), dup(a_im)
    ldt2 = jnp.broadcast_to(log_dt[:, :, None], (2, G, 2 * P))

    def blk_map(d, j):
        return (j + d * (nblk - 1 - 2 * j), 0, d)

    par = pl.BlockSpec((None, G, 2 * P), lambda d, j: (d, 0, 0))
    return pl.pallas_call(
        functools.partial(_ssm_scan_kernel, T=T, n0=n0, seq_starts=seq_starts, seq_ends=seq_ends),
        grid=(2, nblk),
        in_specs=[par, par, par, pl.BlockSpec((n0, G, 2 * P), blk_map)],
        out_specs=pl.BlockSpec((n0, G, 2 * P), blk_map),
        out_shape=jax.ShapeDtypeStruct((nc, G, 4 * P), BF16),
        scratch_shapes=[pltpu.VMEM((G, 2 * P), F32), pltpu.VMEM((G, 2 * P), F32)],
        compiler_params=pltpu.CompilerParams(dimension_semantics=("arbitrary", "arbitrary")),
        name="ssm_scan",
    )(are2, aim2, ldt2, s_loc)


def _ssm_out_kernel(u_ref, mt_ref, s_ref, cs_ref, y_ref, *, gb, ns):
    nc = u_ref.shape[-1]
    for gi in range(gb):
        u = u_ref[gi].reshape(-1, nc)
        y = _dot(mt_ref[gi], u) + _dot_nt(cs_ref[gi], s_ref[:, gi * ns:(gi + 1) * ns])
        y_ref[gi] = _gelu_tanh(y).astype(y_ref.dtype).reshape(y_ref.shape[1:])


def _ssm_out(ut, mt, s_in, cs):
    G, T, C, nc = ut.shape
    ns = cs.shape[2]
    gb = SSM_GROUPS_PER_STEP
    return pl.pallas_call(
        functools.partial(_ssm_out_kernel, gb=gb, ns=ns),
        grid=(G // gb,),
        in_specs=[pl.BlockSpec((gb, T, C, nc), lambda s: (s, 0, 0, 0)),
                  pl.BlockSpec((gb, T * C, T * C), lambda s: (s, 0, 0)),
                  pl.BlockSpec((nc, gb * ns), lambda s: (0, s)),
                  pl.BlockSpec((gb, T * C, ns), lambda s: (s, 0, 0))],
        out_specs=pl.BlockSpec((gb, T, C, nc), lambda s: (s, 0, 0, 0)),
        out_shape=jax.ShapeDtypeStruct((G, T, C, nc), BF16),
        compiler_params=pltpu.CompilerParams(dimension_semantics=("parallel",)),
        name="ssm_out",
    )(ut, mt, s_in, cs)


def _ssm_tok_kernel(y_ref, o_ref, *stage_refs):
    T, ncb = y_ref.shape[1], y_ref.shape[-1]
    eye = (lax.broadcasted_iota(jnp.int32, (ncb, ncb), 0)
           == lax.broadcasted_iota(jnp.int32, (ncb, ncb), 1)).astype(BF16)
    for r in range(T):
        yt = y_ref[:, r].reshape(-1, ncb)
        tok = _dot_nt(eye, yt)
        for s, stage in enumerate(stage_refs):
            stage[pl.ds(r, ncb, stride=T), :] = tok[:, s * V7X_LANES:(s + 1) * V7X_LANES]
    for s, stage in enumerate(stage_refs):
        o_ref[:, s * V7X_LANES:(s + 1) * V7X_LANES] = stage[...].astype(o_ref.dtype)


def _ssm_tok(yt):
    G, T, C, nc = yt.shape
    ncb = SSM_TOK_CHUNKS
    est = 2 * G * T * C * ncb * 2 + ncb * T * G * C * 4 + 2 * ncb * T * G * C * 2
    return pl.pallas_call(
        _ssm_tok_kernel,
        grid=(nc // ncb,),
        in_specs=[pl.BlockSpec((G, T, C, ncb), lambda s: (0, 0, 0, s))],
        out_specs=pl.BlockSpec((ncb * T, G * C), lambda s: (s, 0)),
        out_shape=jax.ShapeDtypeStruct((nc * T, G * C), BF16),
        scratch_shapes=[pltpu.VMEM((ncb * T, V7X_LANES), F32)] * (G * C // V7X_LANES),
        compiler_params=pltpu.CompilerParams(
            dimension_semantics=("parallel",), vmem_limit_bytes=_vmem_limit(est)),
        name="ssm_tok",
    )(yt)


def _s5_mixer_gelu(xb, wt, tables, a_re, a_im, log_dt, *, n_unit_seqs):
    T, G, P = SSM_CHUNK, SSM_GROUPS, SSM_STATE
    nc = xb.shape[0] // T
    mt, gm, cs = tables
    ut = _ssm_in(xb.reshape(nc, T * xb.shape[1]), wt, T=T)
    s_loc = _ssm_state(ut, gm)
    s_in = _ssm_scan(s_loc.reshape(nc, G, 4 * P), a_re, a_im, log_dt, T=T, n_unit_seqs=n_unit_seqs)
    yt = _ssm_out(ut, mt, s_in.reshape(nc, G * 4 * P), cs)
    return _ssm_tok(yt)


def _kv_kernel(m_ref, g_ref, b_ref, w_ref, o_ref, mb_ref):
    @pl.when(pl.program_id(0) == 0)
    def _():
        mb_ref[...] = _layernorm_rows(m_ref[...], g_ref[...], b_ref[...]).astype(BF16)

    o_ref[...] = _dot(mb_ref[...], w_ref[...]).astype(o_ref.dtype)


def _kv_proj(mem, ln_g, ln_b, w_kv):
    n, d = mem.shape
    m = w_kv.shape[1]
    tn = KV_TN
    est = 2 * n * d * 4 + n * d * 2 + 2 * d * tn * 2 + 2 * n * tn * 2 + 3 * n * d * 4
    return pl.pallas_call(
        _kv_kernel,
        grid=(m // tn,),
        in_specs=[pl.BlockSpec((n, d), lambda j: (0, 0)),
                  pl.BlockSpec((1, d), lambda j: (0, 0)),
                  pl.BlockSpec((1, d), lambda j: (0, 0)),
                  pl.BlockSpec((d, tn), lambda j: (0, j))],
        out_specs=pl.BlockSpec((n, tn), lambda j: (0, j)),
        out_shape=jax.ShapeDtypeStruct((n, m), BF16),
        scratch_shapes=[pltpu.VMEM((n, d), BF16)],
        compiler_params=pltpu.CompilerParams(
            dimension_semantics=("arbitrary",), vmem_limit_bytes=_vmem_limit(est)),
        name="kv_proj",
    )(mem, ln_g, ln_b, w_kv)


def _mixer_kernel(ya_ref, ug_ref, vg_ref, q_ref, g0_ref, g1_ref, g2_ref, kv_ref,
                  ws_ref, bs_ref, lng_ref, lnb_ref, wa_ref, wb_ref, wgp_ref, wap_ref,
                  o_ref, ub_ref, ob_ref, *, tm):
    n = pl.program_id(1)

    @pl.when(n == 0)
    def _():
        hd = GMLP_HEAD_DIM

        def chunk_body(ci, carry):
            r0 = pl.multiple_of(ci * GMLP_CHUNK, GMLP_CHUNK)
            rows = pl.ds(r0, GMLP_CHUNK)
            u = _gelu_tanh(ug_ref[rows, :].astype(F32))
            v = _gelu_tanh(vg_ref[rows, :].astype(F32))
            v = _layernorm_rows(v, lng_ref[...], lnb_ref[...]).astype(BF16)
            for h in range(GMLP_HEADS):
                cols = slice(h * hd, (h + 1) * hd)
                mixed = _dot(ws_ref[h], v[:, cols]) + bs_ref[:, h:h + 1]
                ub_ref[rows, cols] = (u[:, cols] * mixed).astype(BF16)
            return carry

        lax.fori_loop(0, tm // GMLP_CHUNK, chunk_body, 0)

        ad = ATTN_HEAD_DIM
        scale = ad ** -0.5
        for h in range(ATTN_HEADS):
            cols = slice(h * ad, (h + 1) * ad)
            vcols = slice(ATTN_WIDTH + h * ad, ATTN_WIDTH + (h + 1) * ad)
            s = _dot_nt(q_ref[:, cols], kv_ref[:, cols]) * scale
            s = s - jnp.max(s, axis=-1, keepdims=True)
            p = jnp.exp(s)
            p = p / jnp.sum(p, axis=-1, keepdims=True)
            ob_ref[:, cols] = _dot(p.astype(BF16), kv_ref[:, vcols]).astype(BF16)

    ya = ya_ref[...]
    y_a = _dot(ya, wa_ref[...]) * _sigmoid(_dot(ya, wb_ref[...]))
    y_b = _dot(ub_ref[...], wgp_ref[...])
    y_c = _dot(ob_ref[...], wap_ref[...])
    merged = (_sigmoid(g0_ref[...].astype(F32)) * y_a
              + _sigmoid(g1_ref[...].astype(F32)) * y_b
              + _sigmoid(g2_ref[...].astype(F32)) * y_c)
    o_ref[...] = merged.astype(o_ref.dtype)


def _mixer(ya, proj, kv, w_s, b_s, gln_g, gln_b, w_glu, w_gproj, w_aproj):
    n = ya.shape[0]
    d = D_MODEL
    tm, tn = MIX_TM, MIX_TN
    nn = d // tn
    wid = GMLP_WIDTH
    blocks_per_seq = SEQ_BLOCK // tm
    n_mem_batches = kv.shape[0] // N_MEM

    def kv_map(i, j):
        return (jnp.minimum(i // blocks_per_seq, n_mem_batches - 1), 0)

    def gate_spec(br):
        base = (OFF_GATE + br * d) // tn
        return pl.BlockSpec((tm, tn), lambda i, j: (i, base + j))

    row_blk = lambda c: pl.BlockSpec((tm, wid), lambda i, j: (i, c))
    full2 = lambda a: pl.BlockSpec(a.shape, lambda i, j: (0, 0))
    est = (2 * 4 * tm * wid * 2 + 2 * 3 * tm * tn * 2 + 2 * N_MEM * 2 * ATTN_WIDTH * 2
           + 2 * 4 * wid * tn * 2 + 2 * tm * tn * 2 + 2 * tm * wid * 2
           + 6 * tm * tn * 4 + 4 * GMLP_CHUNK * wid * 4 + 3 * tm * N_MEM * 4)
    return pl.pallas_call(
        functools.partial(_mixer_kernel, tm=tm),
        grid=(n // tm, nn),
        in_specs=[
            pl.BlockSpec((tm, wid), lambda i, j: (i, 0)),
            row_blk(OFF_GU // wid), row_blk(OFF_GV // wid), row_blk(OFF_Q // wid),
            gate_spec(0), gate_spec(1), gate_spec(2),
            pl.BlockSpec((N_MEM, 2 * ATTN_WIDTH), kv_map),
            pl.BlockSpec(w_s.shape, lambda i, j: (0, 0, 0)),
            full2(b_s), full2(gln_g), full2(gln_b),
            pl.BlockSpec((wid, tn), lambda i, j: (0, j)),
            pl.BlockSpec((wid, tn), lambda i, j: (0, nn + j)),
            pl.BlockSpec((wid, tn), lambda i, j: (0, j)),
            pl.BlockSpec((wid, tn), lambda i, j: (0, j)),
        ],
        out_specs=pl.BlockSpec((tm, tn), lambda i, j: (i, j)),
        out_shape=jax.ShapeDtypeStruct((n, d), BF16),
        scratch_shapes=[pltpu.VMEM((tm, wid), BF16), pltpu.VMEM((tm, wid), BF16)],
        compiler_params=pltpu.CompilerParams(
            dimension_semantics=("parallel", "arbitrary"),
            vmem_limit_bytes=_vmem_limit(est)),
        name="mixer",
    )(ya, proj, proj, proj, proj, proj, proj, kv, w_s, b_s, gln_g, gln_b,
      w_glu, w_glu, w_gproj, w_aproj)


def _outproj_ln_kernel(m_ref, w_ref, x_ref, g_ref, b_ref, o_ref):
    y = ALPHA * x_ref[...] + _dot(m_ref[...], w_ref[...])
    o_ref[...] = _layernorm_rows(y, g_ref[...], b_ref[...])


def _outproj_ln(merged, w_out, x, ln_g, ln_b):
    n, d = x.shape
    tm = OUT_TM
    est = 2 * tm * d * 2 + 2 * d * d * 2 + 4 * tm * d * 4 + 2 * tm * d * 4
    return pl.pallas_call(
        _outproj_ln_kernel,
        grid=(n // tm,),
        in_specs=[pl.BlockSpec((tm, d), lambda i: (i, 0)),
                  pl.BlockSpec((d, d), lambda i: (0, 0)),
                  pl.BlockSpec((tm, d), lambda i: (i, 0)),
                  pl.BlockSpec((1, d), lambda i: (0, 0)),
                  pl.BlockSpec((1, d), lambda i: (0, 0))],
        out_specs=pl.BlockSpec((tm, d), lambda i: (i, 0)),
        out_shape=jax.ShapeDtypeStruct((n, d), F32),
        compiler_params=pltpu.CompilerParams(
            dimension_semantics=("parallel",), vmem_limit_bytes=_vmem_limit(est)),
        name="outproj_ln",
    )(merged, w_out, x, ln_g, ln_b)


def _layer(groups, ffn1_w_gu, ffn1_w_down, ln1_g, ln1_b, w_in,
           ssm_a_re, ssm_a_im, ssm_log_dt, ssm_b_re, ssm_b_im, ssm_c_re, ssm_c_im, ssm_d, ssm_w_glu,
           gmlp_ln_g, gmlp_ln_b, gmlp_w_s, gmlp_b_s, gmlp_w_proj,
           mem_ln_g, mem_ln_b, attn_w_kv, attn_w_proj,
           w_out, ln2_g, ln2_b, ffn2_w_gu, ffn2_w_down, ln3_g, ln3_b):
    row = lambda v: v.reshape(1, -1)
    ffn1 = _prep_ffn_weights(ffn1_w_gu, ffn1_w_down)
    ffn2 = _prep_ffn_weights(ffn2_w_gu, ffn2_w_down)
    w_in_b = w_in.astype(BF16)
    wt_ssm = w_in[:, :SSM_WIDTH].T.astype(BF16)
    tables = _ssm_tables(ssm_a_re, ssm_a_im, ssm_log_dt, ssm_b_re, ssm_b_im, ssm_c_re, ssm_c_im,
                         ssm_d, T=SSM_CHUNK)
    w_kv, w_s = attn_w_kv.astype(BF16), gmlp_w_s.astype(BF16)
    w_glu, w_gproj, w_aproj = ssm_w_glu.astype(BF16), gmlp_w_proj.astype(BF16), attn_w_proj.astype(BF16)
    w_out_b = w_out.astype(BF16)
    outs = []
    for x, mem, n_unit_seqs in groups:
        x1 = _ffn_ln(x, ffn1, row(ln1_g), row(ln1_b))
        proj, x1b = _in_proj(x1, w_in_b)
        ya = _s5_mixer_gelu(x1b, wt_ssm, tables, ssm_a_re, ssm_a_im, ssm_log_dt,
                            n_unit_seqs=n_unit_seqs)
        kv = _kv_proj(mem, row(mem_ln_g), row(mem_ln_b), w_kv)
        merged = _mixer(ya, proj, kv, w_s, gmlp_b_s, row(gmlp_ln_g), row(gmlp_ln_b),
                        w_glu, w_gproj, w_aproj)
        x2 = _outproj_ln(merged, w_out_b, x1, row(ln2_g), row(ln2_b))
        outs.append(_ffn_ln(x2, ffn2, row(ln3_g), row(ln3_b)))
    return outs


def kernel(x_prompt, x_sample, mem_prompt, mem_sample, ffn1_w_gu, ffn1_w_down, ln1_g, ln1_b, w_in,
           ssm_a_re, ssm_a_im, ssm_log_dt, ssm_b_re, ssm_b_im, ssm_c_re, ssm_c_im, ssm_d, ssm_w_glu,
           gmlp_ln_g, gmlp_ln_b, gmlp_w_s, gmlp_b_s, gmlp_w_proj, mem_ln_g, mem_ln_b, attn_w_kv,
           attn_w_proj, w_out, ln2_g, ln2_b, ffn2_w_gu, ffn2_w_down, ln3_g, ln3_b):
    d = x_prompt.shape[-1]
    assert x_prompt.shape[1] == SEQ_BLOCK and x_sample.shape[0] == 1
    assert x_sample.shape[1] % SEQ_BLOCK == 0
    xs = [x_prompt.reshape(-1, d), x_sample.reshape(-1, d)]
    mems = [mem_prompt.reshape(-1, d), mem_sample.reshape(-1, d)]
    n_unit = [x_prompt.shape[0], 0]
    for l in range(DEPTH):
        xs = _layer(list(zip(xs, mems, n_unit)),
                    ffn1_w_gu[l], ffn1_w_down[l], ln1_g[l], ln1_b[l], w_in[l],
                    ssm_a_re[l], ssm_a_im[l], ssm_log_dt[l], ssm_b_re[l], ssm_b_im[l],
                    ssm_c_re[l], ssm_c_im[l], ssm_d[l], ssm_w_glu[l],
                    gmlp_ln_g[l], gmlp_ln_b[l], gmlp_w_s[l], gmlp_b_s[l], gmlp_w_proj[l],
                    mem_ln_g[l], mem_ln_b[l], attn_w_kv[l], attn_w_proj[l],
                    w_out[l], ln2_g[l], ln2_b[l], ffn2_w_gu[l], ffn2_w_down[l], ln3_g[l], ln3_b[l])
    return (xs[0].reshape(x_prompt.shape), xs[1].reshape(x_sample.shape))
```

```python
import functools
import math

import jax
import jax.numpy as jnp
from jax import lax
from jax.experimental import pallas as pl
from jax.experimental.pallas import tpu as pltpu

F32 = jnp.float32
BF16 = jnp.bfloat16

D_MODEL = 2048
DEPTH = 1
SEQ_BLOCK = 4096
N_MEM = 256
SSM_WIDTH = D_MODEL // 2
SSM_GROUP = 16
SSM_GROUPS = SSM_WIDTH // SSM_GROUP
SSM_STATE = 64
GMLP_WIDTH = D_MODEL // 2
GMLP_CHUNK = 128
GMLP_HEADS = 8
GMLP_HEAD_DIM = GMLP_WIDTH // GMLP_HEADS
ATTN_HEADS = 4
ATTN_HEAD_DIM = D_MODEL // 8
ATTN_WIDTH = ATTN_HEADS * ATTN_HEAD_DIM
D_FF = 5504
ALPHA = (2.0 * DEPTH) ** 0.25
LN_EPS = 1e-5

V7X_LANES = 128
V7X_VMEM_BYTES = 64 * 1024 * 1024

SSM_CHUNK = 32
SSM_GROUPS_PER_STEP = 8
SSM_TOK_CHUNKS = 128
FFN_TM = 512
FFN_TF = 1024
PROJ_TM = 1024
PROJ_TN = 1024
BRANCH_TM = 512
MIX_TM = 512
MIX_TN = 1024
OUT_TM = 512
KV_TN = 512


def _vmem_limit(nbytes):
    return int(min(nbytes + (16 << 20), V7X_VMEM_BYTES - (4 << 20)))


def _layernorm_rows(y, g, b):
    mu = jnp.mean(y, axis=-1, keepdims=True)
    yc = y - mu
    var = jnp.mean(yc * yc, axis=-1, keepdims=True)
    return yc * lax.rsqrt(var + LN_EPS) * g + b


def _gelu_tanh(x):
    c = math.sqrt(2.0 / math.pi)
    return 0.5 * x * (1.0 + jnp.tanh(c * (x + 0.044715 * (x * x * x))))


def _sigmoid(x):
    return 1.0 / (1.0 + jnp.exp(-x))


def _dot(a, b):
    return jnp.dot(a, b, preferred_element_type=F32)


def _dot_nt(a, b):
    return lax.dot_general(a, b, (((1,), (1,)), ((), ())), preferred_element_type=F32)


def _swiglu_down(xb, wg, wu, wd):
    gate = _dot(xb, wg)
    up = _dot(xb, wu)
    act = (gate * _sigmoid(gate) * up).astype(BF16)
    return _dot(act, wd)


def _ffn_ln_kernel(x_ref, wg_ref, wu_ref, wd_ref, wgt_ref, wut_ref, wdt_ref, g_ref, b_ref,
                   o_ref, xb_ref):
    j = pl.program_id(1)

    @pl.when(j == 0)
    def _():
        xb_ref[...] = x_ref[...].astype(BF16)
        o_ref[...] = jnp.zeros_like(o_ref)

    o_ref[...] += _swiglu_down(xb_ref[...], wg_ref[...], wu_ref[...], wd_ref[...])

    @pl.when(j == pl.num_programs(1) - 1)
    def _():
        acc = o_ref[...] + _swiglu_down(xb_ref[...], wgt_ref[...], wut_ref[...], wdt_ref[...])
        y = ALPHA * x_ref[...] + 0.5 * acc
        o_ref[...] = _layernorm_rows(y, g_ref[...], b_ref[...])


def _ffn_ln(x, weights, ln_g, ln_b):
    wgu, wd, wgt, wut, wdt = weights
    n, d = x.shape
    tm, tf = FFN_TM, FFN_TF
    nf = wd.shape[0] // tf
    ft = wdt.shape[0]
    once = dict(pipeline_mode=pl.Buffered(1))
    est = (2 * tm * d * 4 + 2 * tm * d * 4 + tm * d * 2
           + 2 * 3 * (d * tf * 2) + 3 * (d * ft * 2) + 3 * tm * tf * 4)
    return pl.pallas_call(
        _ffn_ln_kernel,
        grid=(n // tm, nf),
        in_specs=[
            pl.BlockSpec((tm, d), lambda i, j: (i, 0)),
            pl.BlockSpec((d, tf), lambda i, j: (0, j)),
            pl.BlockSpec((pl.Element(d), pl.Element(tf)),
                         lambda i, j: (0, pl.multiple_of(D_FF + j * tf, V7X_LANES))),
            pl.BlockSpec((tf, d), lambda i, j: (j, 0)),
            pl.BlockSpec((d, ft), lambda i, j: (0, 0), **once),
            pl.BlockSpec((d, ft), lambda i, j: (0, 0), **once),
            pl.BlockSpec((ft, d), lambda i, j: (0, 0), **once),
            pl.BlockSpec((1, d), lambda i, j: (0, 0)),
            pl.BlockSpec((1, d), lambda i, j: (0, 0)),
        ],
        out_specs=pl.BlockSpec((tm, d), lambda i, j: (i, 0)),
        out_shape=jax.ShapeDtypeStruct((n, d), F32),
        scratch_shapes=[pltpu.VMEM((tm, d), BF16)],
        compiler_params=pltpu.CompilerParams(
            dimension_semantics=("parallel", "arbitrary"),
            vmem_limit_bytes=_vmem_limit(est)),
        name="ffn_ln",
    )(x, wgu, wgu, wd, wgt, wut, wdt, ln_g, ln_b)


def _prep_ffn_weights(w_gu, w_down):
    wgu = w_gu.astype(BF16)
    wd = w_down.astype(BF16)
    full = (D_FF // FFN_TF) * FFN_TF
    return wgu, wd, wgu[:, full:D_FF], wgu[:, D_FF + full:], wd[full:]


def _in_proj_kernel(x_ref, w_ref, o_ref, xb_ref):
    @pl.when(pl.program_id(1) == 0)
    def _():
        xb_ref[...] = x_ref[...].astype(BF16)

    o_ref[...] = _dot(xb_ref[...], w_ref[...]).astype(o_ref.dtype)


def _in_proj(x, w):
    n, k = x.shape
    tm, tn = PROJ_TM, PROJ_TN
    skip = SSM_WIDTH // tn
    m = w.shape[1] - SSM_WIDTH
    est = 2 * tm * k * 4 + 2 * tm * k * 2 + 2 * k * tn * 2 + 2 * tm * tn * 2 + tm * tn * 4
    return pl.pallas_call(
        _in_proj_kernel,
        grid=(n // tm, m // tn),
        in_specs=[pl.BlockSpec((tm, k), lambda i, j: (i, 0)),
                  pl.BlockSpec((k, tn), lambda i, j: (0, skip + j))],
        out_specs=[pl.BlockSpec((tm, tn), lambda i, j: (i, j)),
                   pl.BlockSpec((tm, k), lambda i, j: (i, 0))],
        out_shape=[jax.ShapeDtypeStruct((n, m), BF16),
                   jax.ShapeDtypeStruct((n, k), BF16)],
        compiler_params=pltpu.CompilerParams(
            dimension_semantics=("parallel", "arbitrary"),
            vmem_limit_bytes=_vmem_limit(est)),
        name="in_proj",
    )(x, w)


def _branches_kernel(ug_ref, vg_ref, q_ref, kv_ref, ws_ref, bs_ref, lng_ref, lnb_ref,
                     ub_ref, ob_ref, *, tm):
    hd = GMLP_HEAD_DIM

    def chunk_body(ci, carry):
        rows = pl.ds(pl.multiple_of(ci * GMLP_CHUNK, GMLP_CHUNK), GMLP_CHUNK)
        u = _gelu_tanh(ug_ref[rows, :].astype(F32))
        v = _gelu_tanh(vg_ref[rows, :].astype(F32))
        v = _layernorm_rows(v, lng_ref[...], lnb_ref[...]).astype(BF16)
        for h in range(GMLP_HEADS):
            cols = slice(h * hd, (h + 1) * hd)
            mixed = _dot(ws_ref[h], v[:, cols]) + bs_ref[:, h:h + 1]
            ub_ref[rows, cols] = (u[:, cols] * mixed).astype(BF16)
        return carry

    lax.fori_loop(0, tm // GMLP_CHUNK, chunk_body, 0)

    ad = ATTN_HEAD_DIM
    scale = ad ** -0.5
    for h in range(ATTN_HEADS):
        cols = slice(h * ad, (h + 1) * ad)
        vcols = slice(ATTN_WIDTH + h * ad, ATTN_WIDTH + (h + 1) * ad)
        s = _dot_nt(q_ref[:, cols], kv_ref[:, cols]) * scale
        p = jnp.exp(s - jnp.max(s, axis=-1, keepdims=True))
        p = p * (1.0 / jnp.sum(p, axis=-1, keepdims=True))
        ob_ref[:, cols] = _dot(p.astype(BF16), kv_ref[:, vcols]).astype(BF16)


def _branches(proj, kv, w_s, b_s, gln_g, gln_b):
    n = proj.shape[0]
    tm = BRANCH_TM
    wid = GMLP_WIDTH
    assert wid == ATTN_WIDTH
    blocks_per_seq = SEQ_BLOCK // tm
    n_mem_batches = kv.shape[0] // N_MEM
    col = lambda c: pl.BlockSpec((tm, wid), lambda i: (i, c))
    full2 = lambda a: pl.BlockSpec(a.shape, lambda i: (0, 0))
    out = pl.BlockSpec((tm, wid), lambda i: (i, 0))
    return pl.pallas_call(
        functools.partial(_branches_kernel, tm=tm),
        grid=(n // tm,),
        in_specs=[col(0), col(1), col(2),
                  pl.BlockSpec((N_MEM, 2 * ATTN_WIDTH),
                               lambda i: (jnp.minimum(i // blocks_per_seq, n_mem_batches - 1), 0)),
                  pl.BlockSpec(w_s.shape, lambda i: (0, 0, 0)),
                  full2(b_s), full2(gln_g), full2(gln_b)],
        out_specs=[out, out],
        out_shape=[jax.ShapeDtypeStruct((n, wid), BF16), jax.ShapeDtypeStruct((n, wid), BF16)],
        compiler_params=pltpu.CompilerParams(dimension_semantics=("parallel",)),
        name="branches",
    )(proj, proj, proj, kv, w_s, b_s, gln_g, gln_b)


def _ssm_weight_t_kernel(w_ref, o_ref):
    o_ref[...] = w_ref[...].T.astype(o_ref.dtype)


def _ssm_weight_t(w_in):
    d = w_in.shape[0]
    tn = 2 * V7X_LANES
    return pl.pallas_call(
        _ssm_weight_t_kernel,
        grid=(SSM_WIDTH // tn,),
        in_specs=[pl.BlockSpec((d, tn), lambda j: (0, j))],
        out_specs=pl.BlockSpec((tn, d), lambda j: (j, 0)),
        out_shape=jax.ShapeDtypeStruct((SSM_WIDTH, d), BF16),
        compiler_params=pltpu.CompilerParams(dimension_semantics=("parallel",)),
        name="ssm_weight_t",
    )(w_in)


def _ssm_in_kernel(x_ref, wt_ref, o_ref):
    ut = _dot_nt(wt_ref[...], x_ref[...])
    o_ref[...] = ut.astype(BF16).reshape(o_ref.shape)


def _ssm_in(xc, wt, *, T):
    nc = xc.shape[0]
    d = xc.shape[1] // T
    G, C = SSM_GROUPS, SSM_GROUP
    est = 2 * nc * d * 2 + 2 * G * C * d * 2 + 2 * G * C * nc * 2 + G * C * nc * 4
    return pl.pallas_call(
        _ssm_in_kernel,
        grid=(T,),
        in_specs=[pl.BlockSpec((nc, d), lambda r: (0, r)),
                  pl.BlockSpec((G * C, d), lambda r: (0, 0))],
        out_specs=pl.BlockSpec((G, None, C, nc), lambda r: (0, r, 0, 0)),
        out_shape=jax.ShapeDtypeStruct((G, T, C, nc), BF16),
        compiler_params=pltpu.CompilerParams(
            dimension_semantics=("parallel",), vmem_limit_bytes=_vmem_limit(est)),
        name="ssm_in",
    )(xc, wt)


def _cmul(x, y):
    return x[0] * y[0] - x[1] * y[1], x[0] * y[1] + x[1] * y[0]


def _split_hi_lo(x):
    hi = x.astype(BF16)
    lo = (x - hi.astype(F32)).astype(BF16)
    return hi, lo


def _dot_hi_lo(a, b):
    ah, al = _split_hi_lo(a)
    bh, bl = _split_hi_lo(b)
    return _dot(ah, bh) + _dot(al, bh) + _dot(ah, bl)


def _ssm_tables_kernel(pcol_ref, prow_ref, bcat_ref, cre_ref, cim_ref, dtile_ref,
                       mt_ref, gm_ref, cs_ref, q_ref, *, T):
    P, C = SSM_STATE, SSM_GROUP
    rpt = V7X_LANES // C
    nt = T // rpt
    wide = 2 * T * C

    lane = lax.broadcasted_iota(jnp.int32, (1, V7X_LANES), 1)
    rr = lax.shift_right_logical(lane, 4).astype(F32)
    lanef = lane.astype(F32)
    expand = (lax.broadcasted_iota(jnp.int32, (C, V7X_LANES), 0)
              == (lax.broadcasted_iota(jnp.int32, (C, V7X_LANES), 1) & (C - 1))).astype(F32)

    pc = pcol_ref[...]

    def powers(d, k):
        dt = jnp.exp(pc[:, 4 + d:5 + d])
        zr = pc[:, d:d + 1] * dt
        zi = pc[:, 2 + d:3 + d] * dt
        mag = jnp.exp(k * zr)
        ang = k * zi
        return mag * jnp.cos(ang), mag * jnp.sin(ang)

    def zoh(d, pt):
        are = pc[:, d:d + 1]
        aim = pc[:, 2 + d:3 + d]
        nr = pt[0][:, 1:2] - 1.0
        ni = pt[1][:, 1:2]
        den = are * are + aim * aim
        return (nr * are + ni * aim) / den, (ni * are - nr * aim) / den

    def col(pt, m):
        return pt[0][:, m:m + 1], pt[1][:, m:m + 1]

    dsc_f, pt_f = powers(0, (rpt - 1.0) - rr), powers(0, lanef)
    asc_b, pt_b = powers(1, rr), powers(1, lanef)
    btile = jnp.dot(bcat_ref[...], expand, precision=lax.Precision.HIGHEST,
                    preferred_element_type=F32)
    bt = (btile[0:P], btile[P:2 * P])
    bb_f = _cmul(zoh(0, pt_f), bt)
    bd_f = _cmul(dsc_f, bb_f)
    ba_b = _cmul(asc_b, _cmul(zoh(1, pt_b), bt))

    for j in range(nt):
        cols = slice(j * V7X_LANES, (j + 1) * V7X_LANES)
        xf = _cmul(bd_f, col(pt_f, T - rpt - rpt * j))
        xb = _cmul(ba_b, col(pt_b, rpt * j))
        gm_ref[0:P, cols] = xf[0].astype(BF16)
        gm_ref[P:2 * P, cols] = xf[1].astype(BF16)
        gm_ref[2 * P:3 * P, cols] = xb[0].astype(BF16)
        gm_ref[3 * P:4 * P, cols] = xb[1].astype(BF16)

    zeros = jnp.zeros((2 * P, V7X_LANES), F32)
    for j in range(2 * nt):
        cols = slice(j * V7X_LANES, (j + 1) * V7X_LANES)
        if j < nt:
            qf = _cmul(bd_f, col(pt_f, T - rpt * j - (rpt - 1)))
            q_ref[0:P, cols] = qf[0]
            q_ref[P:2 * P, cols] = qf[1]
            q_ref[2 * P:4 * P, cols] = zeros
        else:
            qb = _cmul(ba_b, col(pt_b, rpt * j - T))
            q_ref[2 * P:3 * P, cols] = qb[0]
            q_ref[3 * P:4 * P, cols] = qb[1]
            if j == nt:
                center = rr == 0.0
                q_ref[0:P, cols] = jnp.where(center, bb_f[0], 0.0)
                q_ref[P:2 * P, cols] = jnp.where(center, bb_f[1], 0.0)
            else:
                q_ref[0:2 * P, cols] = zeros

    lane4 = lax.broadcasted_iota(jnp.int32, (1, 4 * P), 1)
    is_re = (lax.shift_right_logical(lane4, 6) & 1) == 0
    is_f = lane4 < 2 * P
    cre = cre_ref[...]
    cim = cim_ref[...]
    pr = prow_ref[...]
    dt = jnp.exp(pr[2:3])
    zr, zi = pr[0:1] * dt, pr[1:2] * dt

    rowi = lax.broadcasted_iota(jnp.int32, (T, 4 * P), 0)
    kmat = jnp.where(is_f, rowi + 1, T - rowi).astype(F32)
    mag = jnp.exp(kmat * zr)
    pw_r, pw_i = mag * jnp.cos(kmat * zi), mag * jnp.sin(kmat * zi)
    for r in range(T):
        prr, pii = pw_r[r:r + 1], pw_i[r:r + 1]
        blk = jnp.where(is_re, cre * prr - cim * pii, -(cre * pii + cim * prr))
        cs_ref[r * C:(r + 1) * C, :] = blk.astype(BF16)

    lhs = jnp.where(is_re, cre, -cim)
    zt = _dot_hi_lo(lhs, q_ref[...])
    lanew = lax.broadcasted_iota(jnp.int32, (C, wide), 1)
    roww = lax.broadcasted_iota(jnp.int32, (C, wide), 0)
    diag = (lax.shift_right_logical(lanew, 4) == T) & ((lanew & (C - 1)) == roww)
    zt = zt + jnp.where(diag, dtile_ref[...], 0.0)

    for r in range(T):
        off = (T - r) * C
        shifted = pltpu.roll(zt, (wide - off) % wide, 1)
        mt_ref[r * C:(r + 1) * C, :] = shifted[:, :T * C].astype(BF16)


def _ssm_tables(a_re, a_im, log_dt, b_re, b_im, c_re, c_im, d_skip, *, T):
    G, P, C = SSM_GROUPS, SSM_STATE, SSM_GROUP
    ldt = jnp.broadcast_to(log_dt[:, :, None], (2, G, P))
    zc = jnp.zeros((G, P), F32)
    pcol = jnp.stack([a_re[0], a_re[1], a_im[0], a_im[1], ldt[0], ldt[1], zc, zc], axis=-1)

    def row4(x):
        return jnp.concatenate([x[0], x[0], x[1], x[1]], axis=-1)

    zr4 = jnp.zeros((G, 4 * P), F32)
    prow = jnp.stack([row4(a_re), row4(a_im), row4(ldt), zr4, zr4, zr4, zr4, zr4], axis=1)
    bcat = jnp.concatenate([b_re, b_im], axis=1)
    cre4 = row4(c_re)
    cim4 = row4(c_im)
    dtile = jnp.tile(d_skip, (1, 2 * T))[:, None, :]

    tc = T * C
    return pl.pallas_call(
        functools.partial(_ssm_tables_kernel, T=T),
        grid=(G,),
        in_specs=[
            pl.BlockSpec((None, P, 8), lambda g: (g, 0, 0)),
            pl.BlockSpec((None, 8, 4 * P), lambda g: (g, 0, 0)),
            pl.BlockSpec((None, 2 * P, C), lambda g: (g, 0, 0)),
            pl.BlockSpec((None, C, 4 * P), lambda g: (g, 0, 0)),
            pl.BlockSpec((None, C, 4 * P), lambda g: (g, 0, 0)),
            pl.BlockSpec((None, 1, 2 * tc), lambda g: (g, 0, 0)),
        ],
        out_specs=[
            pl.BlockSpec((None, tc, tc), lambda g: (g, 0, 0)),
            pl.BlockSpec((None, 4 * P, tc), lambda g: (g, 0, 0)),
            pl.BlockSpec((None, tc, 4 * P), lambda g: (g, 0, 0)),
        ],
        out_shape=[
            jax.ShapeDtypeStruct((G, tc, tc), BF16),
            jax.ShapeDtypeStruct((G, 4 * P, tc), BF16),
            jax.ShapeDtypeStruct((G, tc, 4 * P), BF16),
        ],
        scratch_shapes=[pltpu.VMEM((4 * P, 2 * tc), F32)],
        compiler_params=pltpu.CompilerParams(dimension_semantics=("parallel",)),
        name="ssm_tables",
    )(pcol, prow, bcat, cre4, cim4, dtile)


def _ssm_state_kernel(u_ref, gm_ref, o_ref, *, gb, ns):
    nc = u_ref.shape[-1]
    for gi in range(gb):
        u = u_ref[gi].reshape(-1, nc)
        o_ref[:, gi * ns:(gi + 1) * ns] = _dot(gm_ref[gi], u).T


def _ssm_state(ut, gm):
    G, T, C, nc = ut.shape
    ns = gm.shape[1]
    gb = SSM_GROUPS_PER_STEP
    return pl.pallas_call(
        functools.partial(_ssm_state_kernel, gb=gb, ns=ns),
        grid=(G // gb,),
        in_specs=[pl.BlockSpec((gb, T, C, nc), lambda s: (s, 0, 0, 0)),
                  pl.BlockSpec((gb, ns, T * C), lambda s: (s, 0, 0))],
        out_specs=pl.BlockSpec((nc, gb * ns), lambda s: (0, s)),
        out_shape=jax.ShapeDtypeStruct((nc, G * ns), F32),
        compiler_params=pltpu.CompilerParams(dimension_semantics=("parallel",)),
        name="ssm_state",
    )(ut, gm)


def _ssm_scan_kernel(are_ref, aim_ref, ldt_ref, s_ref, o_ref, x_ref, xs_ref, *, T, n0, seq_starts, seq_ends):
    d = pl.program_id(0)
    j = pl.program_id(1)
    nblk = pl.num_programs(1)
    blk = j + d * (nblk - 1 - 2 * j)
    P = SSM_STATE

    is_start = functools.reduce(jnp.logical_or, [blk == s for s in seq_starts])
    is_end = functools.reduce(jnp.logical_or, [blk == e for e in seq_ends])
    reset = jnp.where(d == 0, is_start, is_end)

    @pl.when(reset)
    def _():
        x_ref[...] = jnp.zeros_like(x_ref)
        xs_ref[...] = jnp.zeros_like(xs_ref)

    dt = jnp.exp(ldt_ref[...])
    zr = are_ref[...] * dt
    zi = aim_ref[...] * dt
    mag = jnp.exp(float(T) * zr)
    mr = mag * jnp.cos(float(T) * zi)
    mi = mag * jnp.sin(float(T) * zi)
    lane = lax.broadcasted_iota(jnp.int32, mr.shape, 1)
    m2 = jnp.where(lane < P, -mi, mi)
    m2s = -m2

    def body(k, carry):
        x, xs = carry
        row = k + d * (n0 - 1 - 2 * k)
        loc = s_ref[row]
        o_ref[row] = x.astype(o_ref.dtype)
        locs = pltpu.roll(loc, P, 1)
        return x * mr + xs * m2 + loc, xs * mr + x * m2s + locs

    x, xs = lax.fori_loop(0, n0, body, (x_ref[...], xs_ref[...]), unroll=4)
    x_ref[...] = x
    xs_ref[...] = xs


def _ssm_scan(s_loc, a_re, a_im, log_dt, *, T, n_unit_seqs):
    G, P = SSM_GROUPS, SSM_STATE
    nc = s_loc.shape[0]
    n0 = SEQ_BLOCK // T
    nblk = nc // n0
    seq_starts = tuple(range(n_unit_seqs + 1))
    seq_ends = tuple(range(n_unit_seqs)) + (nblk - 1,)
    dup = lambda x: jnp.concatenate([x, x], axis=-1)
    are2, aim2 = dup(a_re), dup(a_im)
    ldt2 = jnp.broadcast_to(log_dt[:, :, None], (2, G, 2 * P))

    def blk_map(d, j):
        return (j + d * (nblk - 1 - 2 * j), 0, d)

    par = pl.BlockSpec((None, G, 2 * P), lambda d, j: (d, 0, 0))
    return pl.pallas_call(
        functools.partial(_ssm_scan_kernel, T=T, n0=n0, seq_starts=seq_starts, seq_ends=seq_ends),
        grid=(2, nblk),
        in_specs=[par, par, par, pl.BlockSpec((n0, G, 2 * P), blk_map)],
        out_specs=pl.BlockSpec((n0, G, 2 * P), blk_map),
        out_shape=jax.ShapeDtypeStruct((nc, G, 4 * P), BF16),
        scratch_shapes=[pltpu.VMEM((G, 2 * P), F32), pltpu.VMEM((G, 2 * P), F32)],
        compiler_params=pltpu.CompilerParams(dimension_semantics=("arbitrary", "arbitrary")),
        name="ssm_scan",
    )(are2, aim2, ldt2, s_loc)


def _ssm_out_kernel(u_ref, mt_ref, s_ref, cs_ref, y_ref, *, gb, ns):
    nc = u_ref.shape[-1]
    for gi in range(gb):
        u = u_ref[gi].reshape(-1, nc)
        y = _dot(mt_ref[gi], u) + _dot_nt(cs_ref[gi], s_ref[:, gi * ns:(gi + 1) * ns])
        y_ref[gi] = _gelu_tanh(y).astype(y_ref.dtype).reshape(y_ref.shape[1:])


def _ssm_out(ut, mt, s_in, cs):
    G, T, C, nc = ut.shape
    ns = cs.shape[2]
    gb = SSM_GROUPS_PER_STEP
    return pl.pallas_call(
        functools.partial(_ssm_out_kernel, gb=gb, ns=ns),
        grid=(G // gb,),
        in_specs=[pl.BlockSpec((gb, T, C, nc), lambda s: (s, 0, 0, 0)),
                  pl.BlockSpec((gb, T * C, T * C), lambda s: (s, 0, 0)),
                  pl.BlockSpec((nc, gb * ns), lambda s: (0, s)),
                  pl.BlockSpec((gb, T * C, ns), lambda s: (s, 0, 0))],
        out_specs=pl.BlockSpec((gb, T, C, nc), lambda s: (s, 0, 0, 0)),
        out_shape=jax.ShapeDtypeStruct((G, T, C, nc), BF16),
        compiler_params=pltpu.CompilerParams(dimension_semantics=("parallel",)),
        name="ssm_out",
    )(ut, mt, s_in, cs)


def _ssm_tok_kernel(y_ref, o_ref, *stage_refs):
    T, ncb = y_ref.shape[1], y_ref.shape[-1]
    eye = (lax.broadcasted_iota(jnp.int32, (ncb, ncb), 0)
           == lax.broadcasted_iota(jnp.int32, (ncb, ncb), 1)).astype(BF16)
    for r in range(T):
        yt = y_ref[:, r].reshape(-1, ncb)
        tok = _dot_nt(eye, yt)
        for s, stage in enumerate(stage_refs):
            stage[pl.ds(r, ncb, stride=T), :] = tok[:, s * V7X_LANES:(s + 1) * V7X_LANES]
    for s, stage in enumerate(stage_refs):
        o_ref[:, s * V7X_LANES:(s + 1) * V7X_LANES] = stage[...].astype(o_ref.dtype)


def _ssm_tok(yt):
    G, T, C, nc = yt.shape
    ncb = SSM_TOK_CHUNKS
    est = 2 * G * T * C * ncb * 2 + ncb * T * G * C * 4 + 2 * ncb * T * G * C * 2
    return pl.pallas_call(
        _ssm_tok_kernel,
        grid=(nc // ncb,),
        in_specs=[pl.BlockSpec((G, T, C, ncb), lambda s: (0, 0, 0, s))],
        out_specs=pl.BlockSpec((ncb * T, G * C), lambda s: (s, 0)),
        out_shape=jax.ShapeDtypeStruct((nc * T, G * C), BF16),
        scratch_shapes=[pltpu.VMEM((ncb * T, V7X_LANES), F32)] * (G * C // V7X_LANES),
        compiler_params=pltpu.CompilerParams(
            dimension_semantics=("parallel",), vmem_limit_bytes=_vmem_limit(est)),
        name="ssm_tok",
    )(yt)


def _s5_mixer_gelu(xb, wt, tables, a_re, a_im, log_dt, *, n_unit_seqs):
    T, G, P = SSM_CHUNK, SSM_GROUPS, SSM_STATE
    nc = xb.shape[0] // T
    mt, gm, cs = tables
    ut = _ssm_in(xb.reshape(nc, T * xb.shape[1]), wt, T=T)
    s_loc = _ssm_state(ut, gm)
    s_in = _ssm_scan(s_loc.reshape(nc, G, 4 * P), a_re, a_im, log_dt, T=T, n_unit_seqs=n_unit_seqs)
    yt = _ssm_out(ut, mt, s_in.reshape(nc, G * 4 * P), cs)
    return _ssm_tok(yt)


def _kv_kernel(m_ref, g_ref, b_ref, w_ref, o_ref, mb_ref):
    @pl.when(pl.program_id(0) == 0)
    def _():
        mb_ref[...] = _layernorm_rows(m_ref[...], g_ref[...], b_ref[...]).astype(BF16)

    o_ref[...] = _dot(mb_ref[...], w_ref[...]).astype(o_ref.dtype)


def _kv_proj(mem, ln_g, ln_b, w_kv):
    n, d = mem.shape
    m = w_kv.shape[1]
    tn = KV_TN
    est = 2 * n * d * 4 + n * d * 2 + 2 * d * tn * 2 + 2 * n * tn * 2 + 3 * n * d * 4
    return pl.pallas_call(
        _kv_kernel,
        grid=(m // tn,),
        in_specs=[pl.BlockSpec((n, d), lambda j: (0, 0)),
                  pl.BlockSpec((1, d), lambda j: (0, 0)),
                  pl.BlockSpec((1, d), lambda j: (0, 0)),
                  pl.BlockSpec((d, tn), lambda j: (0, j))],
        out_specs=pl.BlockSpec((n, tn), lambda j: (0, j)),
        out_shape=jax.ShapeDtypeStruct((n, m), BF16),
        scratch_shapes=[pltpu.VMEM((n, d), BF16)],
        compiler_params=pltpu.CompilerParams(
            dimension_semantics=("arbitrary",), vmem_limit_bytes=_vmem_limit(est)),
        name="kv_proj",
    )(mem, ln_g, ln_b, w_kv)


def _mixer_kernel(ya_ref, ub_ref, ob_ref, g0_ref, g1_ref, g2_ref,
                  wa_ref, wb_ref, wgp_ref, wap_ref, o_ref):
    ya = ya_ref[...]
    y_a = _dot(ya, wa_ref[...]) * _sigmoid(_dot(ya, wb_ref[...]))
    y_b = _dot(ub_ref[...], wgp_ref[...])
    y_c = _dot(ob_ref[...], wap_ref[...])
    merged = (_sigmoid(g0_ref[...].astype(F32)) * y_a
              + _sigmoid(g1_ref[...].astype(F32)) * y_b
              + _sigmoid(g2_ref[...].astype(F32)) * y_c)
    o_ref[...] = merged.astype(o_ref.dtype)


def _mixer(ya, ub, ob, proj, w_glu, w_gproj, w_aproj):
    n, wid = ya.shape
    d = D_MODEL
    tm, tn = MIX_TM, MIX_TN
    nn = d // tn
    gate0 = (proj.shape[1] - 3 * d) // tn
    gate_spec = lambda br: pl.BlockSpec((tm, tn), lambda i, j: (i, gate0 + br * nn + j))
    act_spec = pl.BlockSpec((tm, wid), lambda i, j: (i, 0))
    est = (2 * 3 * tm * wid * 2 + 2 * 3 * tm * tn * 2 + 2 * 4 * wid * tn * 2 + 2 * tm * tn * 2
           + 8 * tm * tn * 4)
    return pl.pallas_call(
        _mixer_kernel,
        grid=(n // tm, nn),
        in_specs=[
            act_spec, act_spec, act_spec,
            gate_spec(0), gate_spec(1), gate_spec(2),
            pl.BlockSpec((wid, tn), lambda i, j: (0, j)),
            pl.BlockSpec((wid, tn), lambda i, j: (0, nn + j)),
            pl.BlockSpec((wid, tn), lambda i, j: (0, j)),
            pl.BlockSpec((wid, tn), lambda i, j: (0, j)),
        ],
        out_specs=pl.BlockSpec((tm, tn), lambda i, j: (i, j)),
        out_shape=jax.ShapeDtypeStruct((n, d), BF16),
        compiler_params=pltpu.CompilerParams(
            dimension_semantics=("parallel", "parallel"),
            vmem_limit_bytes=_vmem_limit(est)),
        name="mixer",
    )(ya, ub, ob, proj, proj, proj, w_glu, w_glu, w_gproj, w_aproj)


def _outproj_ln_kernel(m_ref, w_ref, x_ref, g_ref, b_ref, o_ref):
    y = ALPHA * x_ref[...] + _dot(m_ref[...], w_ref[...])
    o_ref[...] = _layernorm_rows(y, g_ref[...], b_ref[...])


def _outproj_ln(merged, w_out, x, ln_g, ln_b):
    n, d = x.shape
    tm = OUT_TM
    est = 2 * tm * d * 2 + 2 * d * d * 2 + 4 * tm * d * 4 + 2 * tm * d * 4
    return pl.pallas_call(
        _outproj_ln_kernel,
        grid=(n // tm,),
        in_specs=[pl.BlockSpec((tm, d), lambda i: (i, 0)),
                  pl.BlockSpec((d, d), lambda i: (0, 0)),
                  pl.BlockSpec((tm, d), lambda i: (i, 0)),
                  pl.BlockSpec((1, d), lambda i: (0, 0)),
                  pl.BlockSpec((1, d), lambda i: (0, 0))],
        out_specs=pl.BlockSpec((tm, d), lambda i: (i, 0)),
        out_shape=jax.ShapeDtypeStruct((n, d), F32),
        compiler_params=pltpu.CompilerParams(
            dimension_semantics=("parallel",), vmem_limit_bytes=_vmem_limit(est)),
        name="outproj_ln",
    )(merged, w_out, x, ln_g, ln_b)


def _layer(groups, ffn1_w_gu, ffn1_w_down, ln1_g, ln1_b, w_in,
           ssm_a_re, ssm_a_im, ssm_log_dt, ssm_b_re, ssm_b_im, ssm_c_re, ssm_c_im, ssm_d, ssm_w_glu,
           gmlp_ln_g, gmlp_ln_b, gmlp_w_s, gmlp_b_s, gmlp_w_proj,
           mem_ln_g, mem_ln_b, attn_w_kv, attn_w_proj,
           w_out, ln2_g, ln2_b, ffn2_w_gu, ffn2_w_down, ln3_g, ln3_b):
    row = lambda v: v.reshape(1, -1)
    ffn1 = _prep_ffn_weights(ffn1_w_gu, ffn1_w_down)
    ffn2 = _prep_ffn_weights(ffn2_w_gu, ffn2_w_down)
    w_in_b = w_in.astype(BF16)
    wt_ssm = _ssm_weight_t(w_in)
    tables = _ssm_tables(ssm_a_re, ssm_a_im, ssm_log_dt, ssm_b_re, ssm_b_im, ssm_c_re, ssm_c_im,
                         ssm_d, T=SSM_CHUNK)
    w_kv, w_s = attn_w_kv.astype(BF16), gmlp_w_s.astype(BF16)
    w_glu, w_gproj, w_aproj = ssm_w_glu.astype(BF16), gmlp_w_proj.astype(BF16), attn_w_proj.astype(BF16)
    w_out_b = w_out.astype(BF16)
    outs = []
    for x, mem, n_unit_seqs in groups:
        x1 = _ffn_ln(x, ffn1, row(ln1_g), row(ln1_b))
        kv = _kv_proj(mem, row(mem_ln_g), row(mem_ln_b), w_kv)
        proj, x1b = _in_proj(x1, w_in_b)
        ub, ob = _branches(proj, kv, w_s, gmlp_b_s, row(gmlp_ln_g), row(gmlp_ln_b))
        ya = _s5_mixer_gelu(x1b, wt_ssm, tables, ssm_a_re, ssm_a_im, ssm_log_dt,
                            n_unit_seqs=n_unit_seqs)
        merged = _mixer(ya, ub, ob, proj, w_glu, w_gproj, w_aproj)
        x2 = _outproj_ln(merged, w_out_b, x1, row(ln2_g), row(ln2_b))
        outs.append(_ffn_ln(x2, ffn2, row(ln3_g), row(ln3_b)))
    return outs


def kernel(x_prompt, x_sample, mem_prompt, mem_sample, ffn1_w_gu, ffn1_w_down, ln1_g, ln1_b, w_in,
           ssm_a_re, ssm_a_im, ssm_log_dt, ssm_b_re, ssm_b_im, ssm_c_re, ssm_c_im, ssm_d, ssm_w_glu,
           gmlp_ln_g, gmlp_ln_b, gmlp_w_s, gmlp_b_s, gmlp_w_proj, mem_ln_g, mem_ln_b, attn_w_kv,
           attn_w_proj, w_out, ln2_g, ln2_b, ffn2_w_gu, ffn2_w_down, ln3_g, ln3_b):
    d = x_prompt.shape[-1]
    assert x_prompt.shape[1] == SEQ_BLOCK and x_sample.shape[0] == 1
    assert x_sample.shape[1] % SEQ_BLOCK == 0
    xs = [x_prompt.reshape(-1, d), x_sample.reshape(-1, d)]
    mems = [mem_prompt.reshape(-1, d), mem_sample.reshape(-1, d)]
    n_unit = [x_prompt.shape[0], 0]
    for l in range(DEPTH):
        xs = _layer(list(zip(xs, mems, n_unit)),
                    ffn1_w_gu[l], ffn1_w_down[l], ln1_g[l], ln1_b[l], w_in[l],
                    ssm_a_re[l], ssm_a_im[l], ssm_log_dt[l], ssm_b_re[l], ssm_b_im[l],
                    ssm_c_re[l], ssm_c_im[l], ssm_d[l], ssm_w_glu[l],
                    gmlp_ln_g[l], gmlp_ln_b[l], gmlp_w_s[l], gmlp_b_s[l], gmlp_w_proj[l],
                    mem_ln_g[l], mem_ln_b[l], attn_w_kv[l], attn_w_proj[l],
                    w_out[l], ln2_g[l], ln2_b[l], ffn2_w_gu[l], ffn2_w_down[l], ln3_g[l], ln3_b[l])
    return (xs[0].reshape(x_prompt.shape), xs[1].reshape(x_sample.shape))
```

```python
import functools
import math

import jax
import jax.numpy as jnp
from jax import lax
from jax.experimental import pallas as pl
from jax.experimental.pallas import tpu as pltpu

F32 = jnp.float32
BF16 = jnp.bfloat16

D_MODEL = 2048
DEPTH = 1
SEQ_BLOCK = 4096
N_MEM = 256
SSM_WIDTH = D_MODEL // 2
SSM_GROUP = 16
SSM_GROUPS = SSM_WIDTH // SSM_GROUP
SSM_STATE = 64
GMLP_WIDTH = D_MODEL // 2
GMLP_CHUNK = 128
GMLP_HEADS = 8
GMLP_HEAD_DIM = GMLP_WIDTH // GMLP_HEADS
ATTN_HEADS = 4
ATTN_HEAD_DIM = D_MODEL // 8
ATTN_WIDTH = ATTN_HEADS * ATTN_HEAD_DIM
D_FF = 5504
ALPHA = (2.0 * DEPTH) ** 0.25
LN_EPS = 1e-5

V7X_LANES = 128
V7X_VMEM_BYTES = 64 * 1024 * 1024

SSM_CHUNK = 32
SSM_GROUPS_PER_STEP = 8
SSM_TABLE_GROUPS = 4
SSM_TOK_CHUNKS = 128
FFN_TM = 512
FFN_TF = 1024
PROJ_TM = 1024
PROJ_TN = 1024
BRANCH_TM = 512
MIX_TM = 512
MIX_TN = 1024
OUT_TM = 512
KV_TN = 512


def _vmem_limit(nbytes):
    return int(min(nbytes + (16 << 20), V7X_VMEM_BYTES - (4 << 20)))


def _layernorm_rows(y, g, b):
    mu = jnp.mean(y, axis=-1, keepdims=True)
    yc = y - mu
    var = jnp.mean(yc * yc, axis=-1, keepdims=True)
    return yc * lax.rsqrt(var + LN_EPS) * g + b


def _gelu_tanh(x):
    c = math.sqrt(2.0 / math.pi)
    return 0.5 * x * (1.0 + jnp.tanh(c * (x + 0.044715 * (x * x * x))))


def _sigmoid(x):
    return 1.0 / (1.0 + jnp.exp(-x))


def _dot(a, b):
    return jnp.dot(a, b, preferred_element_type=F32)


def _dot_nt(a, b):
    return lax.dot_general(a, b, (((1,), (1,)), ((), ())), preferred_element_type=F32)


def _swiglu_down(xb, wg, wu, wd):
    gate = _dot(xb, wg)
    up = _dot(xb, wu)
    act = (gate * _sigmoid(gate) * up).astype(BF16)
    return _dot(act, wd)


def _ffn_ln_kernel(x_ref, wg_ref, wu_ref, wd_ref, wgt_ref, wut_ref, wdt_ref, g_ref, b_ref,
                   o_ref, xb_ref):
    j = pl.program_id(1)

    @pl.when(j == 0)
    def _():
        xb_ref[...] = x_ref[...].astype(BF16)
        o_ref[...] = jnp.zeros_like(o_ref)

    o_ref[...] += _swiglu_down(xb_ref[...], wg_ref[...], wu_ref[...], wd_ref[...])

    @pl.when(j == pl.num_programs(1) - 1)
    def _():
        acc = o_ref[...] + _swiglu_down(xb_ref[...], wgt_ref[...], wut_ref[...], wdt_ref[...])
        y = ALPHA * x_ref[...] + 0.5 * acc
        o_ref[...] = _layernorm_rows(y, g_ref[...], b_ref[...])


def _ffn_ln(x, weights, ln_g, ln_b):
    wgu, wd, wgt, wut, wdt = weights
    n, d = x.shape
    tm, tf = FFN_TM, FFN_TF
    nf = wd.shape[0] // tf
    ft = wdt.shape[0]
    once = dict(pipeline_mode=pl.Buffered(1))
    est = (2 * tm * d * 4 + 2 * tm * d * 4 + tm * d * 2
           + 2 * 3 * (d * tf * 2) + 3 * (d * ft * 2) + 3 * tm * tf * 4)
    return pl.pallas_call(
        _ffn_ln_kernel,
        grid=(n // tm, nf),
        in_specs=[
            pl.BlockSpec((tm, d), lambda i, j: (i, 0)),
            pl.BlockSpec((d, tf), lambda i, j: (0, j)),
            pl.BlockSpec((pl.Element(d), pl.Element(tf)),
                         lambda i, j: (0, pl.multiple_of(D_FF + j * tf, V7X_LANES))),
            pl.BlockSpec((tf, d), lambda i, j: (j, 0)),
            pl.BlockSpec((d, ft), lambda i, j: (0, 0), **once),
            pl.BlockSpec((d, ft), lambda i, j: (0, 0), **once),
            pl.BlockSpec((ft, d), lambda i, j: (0, 0), **once),
            pl.BlockSpec((1, d), lambda i, j: (0, 0)),
            pl.BlockSpec((1, d), lambda i, j: (0, 0)),
        ],
        out_specs=pl.BlockSpec((tm, d), lambda i, j: (i, 0)),
        out_shape=jax.ShapeDtypeStruct((n, d), F32),
        scratch_shapes=[pltpu.VMEM((tm, d), BF16)],
        compiler_params=pltpu.CompilerParams(
            dimension_semantics=("parallel", "arbitrary"),
            vmem_limit_bytes=_vmem_limit(est)),
        name="ffn_ln",
    )(x, wgu, wgu, wd, wgt, wut, wdt, ln_g, ln_b)


def _prep_ffn_weights(w_gu, w_down):
    wgu = w_gu.astype(BF16)
    wd = w_down.astype(BF16)
    full = (D_FF // FFN_TF) * FFN_TF
    return wgu, wd, wgu[:, full:D_FF], wgu[:, D_FF + full:], wd[full:]


def _in_proj_kernel(x_ref, w_ref, lng_ref, lnb_ref, o_ref, xb_ref):
    j = pl.program_id(1)

    @pl.when(j == 0)
    def _():
        xb_ref[...] = x_ref[...].astype(BF16)

    @pl.when(j == 0)
    def _():
        o_ref[...] = _gelu_tanh(_dot(xb_ref[...], w_ref[...])).astype(o_ref.dtype)

    @pl.when(j == 1)
    def _():
        v = _gelu_tanh(_dot(xb_ref[...], w_ref[...]))
        o_ref[...] = _layernorm_rows(v, lng_ref[...], lnb_ref[...]).astype(o_ref.dtype)

    @pl.when(j >= 2)
    def _():
        o_ref[...] = _dot(xb_ref[...], w_ref[...]).astype(o_ref.dtype)


def _in_proj(x, w, gln_g, gln_b):
    n, k = x.shape
    tm, tn = PROJ_TM, PROJ_TN
    assert tn == GMLP_WIDTH
    skip = SSM_WIDTH // tn
    m = w.shape[1] - SSM_WIDTH
    est = 2 * tm * k * 4 + 2 * tm * k * 2 + 2 * k * tn * 2 + 2 * tm * tn * 2 + 3 * tm * tn * 4
    return pl.pallas_call(
        _in_proj_kernel,
        grid=(n // tm, m // tn),
        in_specs=[pl.BlockSpec((tm, k), lambda i, j: (i, 0)),
                  pl.BlockSpec((k, tn), lambda i, j: (0, skip + j)),
                  pl.BlockSpec((1, tn), lambda i, j: (0, 0)),
                  pl.BlockSpec((1, tn), lambda i, j: (0, 0))],
        out_specs=[pl.BlockSpec((tm, tn), lambda i, j: (i, j)),
                   pl.BlockSpec((tm, k), lambda i, j: (i, 0))],
        out_shape=[jax.ShapeDtypeStruct((n, m), BF16),
                   jax.ShapeDtypeStruct((n, k), BF16)],
        compiler_params=pltpu.CompilerParams(
            dimension_semantics=("parallel", "arbitrary"),
            vmem_limit_bytes=_vmem_limit(est)),
        name="in_proj",
    )(x, w, gln_g, gln_b)


def _branches_kernel(u_ref, v_ref, q_ref, kv_ref, ws_ref, bs_ref, ub_ref, ob_ref, *, tm):
    hd = GMLP_HEAD_DIM

    def chunk_body(ci, carry):
        rows = pl.ds(pl.multiple_of(ci * GMLP_CHUNK, GMLP_CHUNK), GMLP_CHUNK)
        for h in range(GMLP_HEADS):
            cols = slice(h * hd, (h + 1) * hd)
            mixed = _dot(ws_ref[h], v_ref[rows, cols]) + bs_ref[:, h:h + 1]
            ub_ref[rows, cols] = (u_ref[rows, cols].astype(F32) * mixed).astype(BF16)
        return carry

    lax.fori_loop(0, tm // GMLP_CHUNK, chunk_body, 0)

    ad = ATTN_HEAD_DIM
    scale = ad ** -0.5
    for h in range(ATTN_HEADS):
        cols = slice(h * ad, (h + 1) * ad)
        vcols = slice(ATTN_WIDTH + h * ad, ATTN_WIDTH + (h + 1) * ad)
        s = _dot_nt(q_ref[:, cols], kv_ref[:, cols]) * scale
        p = jnp.exp(s - jnp.max(s, axis=-1, keepdims=True))
        p = p * (1.0 / jnp.sum(p, axis=-1, keepdims=True))
        ob_ref[:, cols] = _dot(p.astype(BF16), kv_ref[:, vcols]).astype(BF16)


def _branches(proj, kv, w_s, b_s):
    n = proj.shape[0]
    tm = BRANCH_TM
    wid = GMLP_WIDTH
    assert wid == ATTN_WIDTH
    blocks_per_seq = SEQ_BLOCK // tm
    n_mem_batches = kv.shape[0] // N_MEM
    col = lambda c: pl.BlockSpec((tm, wid), lambda i: (i, c))
    full2 = lambda a: pl.BlockSpec(a.shape, lambda i: (0, 0))
    out = pl.BlockSpec((tm, wid), lambda i: (i, 0))
    return pl.pallas_call(
        functools.partial(_branches_kernel, tm=tm),
        grid=(n // tm,),
        in_specs=[col(0), col(1), col(2),
                  pl.BlockSpec((N_MEM, 2 * ATTN_WIDTH),
                               lambda i: (jnp.minimum(i // blocks_per_seq, n_mem_batches - 1), 0)),
                  pl.BlockSpec(w_s.shape, lambda i: (0, 0, 0)),
                  full2(b_s)],
        out_specs=[out, out],
        out_shape=[jax.ShapeDtypeStruct((n, wid), BF16), jax.ShapeDtypeStruct((n, wid), BF16)],
        compiler_params=pltpu.CompilerParams(dimension_semantics=("parallel",)),
        name="branches",
    )(proj, proj, proj, kv, w_s, b_s)


def _ssm_weight_t_kernel(w_ref, o_ref):
    o_ref[...] = w_ref[...].T.astype(o_ref.dtype)


def _ssm_weight_t(w_in):
    d = w_in.shape[0]
    tn = 2 * V7X_LANES
    return pl.pallas_call(
        _ssm_weight_t_kernel,
        grid=(SSM_WIDTH // tn,),
        in_specs=[pl.BlockSpec((d, tn), lambda j: (0, j))],
        out_specs=pl.BlockSpec((tn, d), lambda j: (j, 0)),
        out_shape=jax.ShapeDtypeStruct((SSM_WIDTH, d), BF16),
        compiler_params=pltpu.CompilerParams(dimension_semantics=("parallel",)),
        name="ssm_weight_t",
    )(w_in)


def _ssm_in_kernel(x_ref, wt_ref, o_ref):
    ut = _dot_nt(wt_ref[...], x_ref[...])
    o_ref[...] = ut.astype(BF16).reshape(o_ref.shape)


def _ssm_in(xc, wt, *, T):
    nc = xc.shape[0]
    d = xc.shape[1] // T
    G, C = SSM_GROUPS, SSM_GROUP
    est = 2 * nc * d * 2 + 2 * G * C * d * 2 + 2 * G * C * nc * 2 + G * C * nc * 4
    return pl.pallas_call(
        _ssm_in_kernel,
        grid=(T,),
        in_specs=[pl.BlockSpec((nc, d), lambda r: (0, r)),
                  pl.BlockSpec((G * C, d), lambda r: (0, 0))],
        out_specs=pl.BlockSpec((G, None, C, nc), lambda r: (0, r, 0, 0)),
        out_shape=jax.ShapeDtypeStruct((G, T, C, nc), BF16),
        compiler_params=pltpu.CompilerParams(
            dimension_semantics=("parallel",), vmem_limit_bytes=_vmem_limit(est)),
        name="ssm_in",
    )(xc, wt)


def _cmul(x, y):
    return x[0] * y[0] - x[1] * y[1], x[0] * y[1] + x[1] * y[0]


def _split_hi_lo(x):
    hi = x.astype(BF16)
    lo = (x - hi.astype(F32)).astype(BF16)
    return hi, lo


def _dot_hi_lo(a, b):
    ah, al = _split_hi_lo(a)
    bh, bl = _split_hi_lo(b)
    return _dot(ah, bh) + _dot(al, bh) + _dot(ah, bl)


def _ssm_tables_kernel(*refs, T, gb):
    for gi in range(gb):
        _ssm_tables_group(*[r.at[gi] for r in refs], T=T)


def _ssm_tables_group(pcol_ref, prow_ref, bcat_ref, cre_ref, cim_ref, dtile_ref,
                      mt_ref, gm_ref, cs_ref, q_ref, *, T):
    P, C = SSM_STATE, SSM_GROUP
    rpt = V7X_LANES // C
    nt = T // rpt
    wide = 2 * T * C

    lane = lax.broadcasted_iota(jnp.int32, (1, V7X_LANES), 1)
    rr = lax.shift_right_logical(lane, 4)
    expand = (lax.broadcasted_iota(jnp.int32, (C, V7X_LANES), 0)
              == (lax.broadcasted_iota(jnp.int32, (C, V7X_LANES), 1) & (C - 1))).astype(F32)

    pc = pcol_ref[...]

    lane4 = lax.broadcasted_iota(jnp.int32, (1, 4 * P), 1)
    is_re = (lax.shift_right_logical(lane4, 6) & 1) == 0
    is_f = lane4 < 2 * P
    pr = prow_ref[...]
    dt4 = jnp.exp(pr[2:3])
    zr4, zi4 = pr[0:1] * dt4, pr[1:2] * dt4

    kp = ((T + 1 + 7) // 8) * 8
    krow = lax.broadcasted_iota(jnp.int32, (kp, 4 * P), 0).astype(F32)
    mag = jnp.exp(krow * zr4)
    pw_r, pw_i = mag * jnp.cos(krow * zi4), mag * jnp.sin(krow * zi4)

    def states_on_rows(tab, lo):
        slab = tab[:, lo:lo + V7X_LANES]
        padded = jnp.concatenate([slab, jnp.zeros((V7X_LANES - kp, V7X_LANES), F32)], axis=0)
        return padded.T[0:P]

    pt_f = states_on_rows(pw_r, 0), states_on_rows(pw_i, 0)
    pt_b = states_on_rows(pw_r, 2 * P), states_on_rows(pw_i, 2 * P)

    def stair(pt, descending):
        re = jnp.zeros((P, V7X_LANES), F32)
        im = re
        for k in range(rpt):
            m = rpt - 1 - k if descending else k
            re = jnp.where(rr == k, pt[0][:, m:m + 1], re)
            im = jnp.where(rr == k, pt[1][:, m:m + 1], im)
        return re, im

    def zoh(d, pt):
        are = pc[:, d:d + 1]
        aim = pc[:, 2 + d:3 + d]
        nr = pt[0][:, 1:2] - 1.0
        ni = pt[1][:, 1:2]
        den = are * are + aim * aim
        return (nr * are + ni * aim) / den, (ni * are - nr * aim) / den

    def col(pt, m):
        return pt[0][:, m:m + 1], pt[1][:, m:m + 1]

    dsc_f = stair(pt_f, True)
    asc_b = stair(pt_b, False)
    btile = jnp.dot(bcat_ref[...], expand, precision=lax.Precision.HIGHEST,
                    preferred_element_type=F32)
    bt = (btile[0:P], btile[P:2 * P])
    bb_f = _cmul(zoh(0, pt_f), bt)
    bd_f = _cmul(dsc_f, bb_f)
    ba_b = _cmul(asc_b, _cmul(zoh(1, pt_b), bt))

    for j in range(nt):
        cols = slice(j * V7X_LANES, (j + 1) * V7X_LANES)
        xf = _cmul(bd_f, col(pt_f, T - rpt - rpt * j))
        xb = _cmul(ba_b, col(pt_b, rpt * j))
        gm_ref[0:P, cols] = xf[0].astype(BF16)
        gm_ref[P:2 * P, cols] = xf[1].astype(BF16)
        gm_ref[2 * P:3 * P, cols] = xb[0].astype(BF16)
        gm_ref[3 * P:4 * P, cols] = xb[1].astype(BF16)

    zeros = jnp.zeros((2 * P, V7X_LANES), F32)
    for j in range(2 * nt):
        cols = slice(j * V7X_LANES, (j + 1) * V7X_LANES)
        if j < nt:
            qf = _cmul(bd_f, col(pt_f, T - rpt * j - (rpt - 1)))
            q_ref[0:P, cols] = qf[0]
            q_ref[P:2 * P, cols] = qf[1]
            q_ref[2 * P:4 * P, cols] = zeros
        else:
            qb = _cmul(ba_b, col(pt_b, rpt * j - T))
            q_ref[2 * P:3 * P, cols] = qb[0]
            q_ref[3 * P:4 * P, cols] = qb[1]
            if j == nt:
                center = rr == 0
                q_ref[0:P, cols] = jnp.where(center, bb_f[0], 0.0)
                q_ref[P:2 * P, cols] = jnp.where(center, bb_f[1], 0.0)
            else:
                q_ref[0:2 * P, cols] = zeros

    cre = cre_ref[...]
    cim = cim_ref[...]

    for r in range(T):
        prr = jnp.where(is_f, pw_r[r + 1:r + 2], pw_r[T - r:T - r + 1])
        pii = jnp.where(is_f, pw_i[r + 1:r + 2], pw_i[T - r:T - r + 1])
        blk = jnp.where(is_re, cre * prr - cim * pii, -(cre * pii + cim * prr))
        cs_ref[r * C:(r + 1) * C, :] = blk.astype(BF16)

    lhs = jnp.where(is_re, cre, -cim)
    zt = _dot_hi_lo(lhs, q_ref[...])
    lanew = lax.broadcasted_iota(jnp.int32, (C, wide), 1)
    roww = lax.broadcasted_iota(jnp.int32, (C, wide), 0)
    diag = (lax.shift_right_logical(lanew, 4) == T) & ((lanew & (C - 1)) == roww)
    zt = zt + jnp.where(diag, dtile_ref[...], 0.0)

    for r in range(T):
        off = (T - r) * C
        shifted = pltpu.roll(zt, (wide - off) % wide, 1)
        mt_ref[r * C:(r + 1) * C, :] = shifted[:, :T * C].astype(BF16)


def _ssm_tables(a_re, a_im, log_dt, b_re, b_im, c_re, c_im, d_skip, *, T):
    G, P, C = SSM_GROUPS, SSM_STATE, SSM_GROUP
    ldt = jnp.broadcast_to(log_dt[:, :, None], (2, G, P))
    zc = jnp.zeros((G, P), F32)
    pcol = jnp.stack([a_re[0], a_re[1], a_im[0], a_im[1], ldt[0], ldt[1], zc, zc], axis=-1)

    def row4(x):
        return jnp.concatenate([x[0], x[0], x[1], x[1]], axis=-1)

    zr4 = jnp.zeros((G, 4 * P), F32)
    prow = jnp.stack([row4(a_re), row4(a_im), row4(ldt), zr4, zr4, zr4, zr4, zr4], axis=1)
    bcat = jnp.concatenate([b_re, b_im], axis=1)
    cre4 = row4(c_re)
    cim4 = row4(c_im)
    dtile = jnp.tile(d_skip, (1, 2 * T))[:, None, :]

    tc = T * C
    gb = SSM_TABLE_GROUPS
    blk = lambda *shape: pl.BlockSpec((gb,) + shape, lambda g: (g,) + (0,) * len(shape))
    return pl.pallas_call(
        functools.partial(_ssm_tables_kernel, T=T, gb=gb),
        grid=(G // gb,),
        in_specs=[blk(P, 8), blk(8, 4 * P), blk(2 * P, C), blk(C, 4 * P), blk(C, 4 * P),
                  blk(1, 2 * tc)],
        out_specs=[blk(tc, tc), blk(4 * P, tc), blk(tc, 4 * P)],
        out_shape=[
            jax.ShapeDtypeStruct((G, tc, tc), BF16),
            jax.ShapeDtypeStruct((G, 4 * P, tc), BF16),
            jax.ShapeDtypeStruct((G, tc, 4 * P), BF16),
        ],
        scratch_shapes=[pltpu.VMEM((gb, 4 * P, 2 * tc), F32)],
        compiler_params=pltpu.CompilerParams(dimension_semantics=("parallel",)),
        name="ssm_tables",
    )(pcol, prow, bcat, cre4, cim4, dtile)


def _ssm_state_kernel(u_ref, gm_ref, o_ref, *, gb, ns):
    nc = u_ref.shape[-1]
    for gi in range(gb):
        u = u_ref[gi].reshape(-1, nc)
        o_ref[:, gi * ns:(gi + 1) * ns] = _dot(gm_ref[gi], u).T


def _ssm_state(ut, gm):
    G, T, C, nc = ut.shape
    ns = gm.shape[1]
    gb = SSM_GROUPS_PER_STEP
    return pl.pallas_call(
        functools.partial(_ssm_state_kernel, gb=gb, ns=ns),
        grid=(G // gb,),
        in_specs=[pl.BlockSpec((gb, T, C, nc), lambda s: (s, 0, 0, 0)),
                  pl.BlockSpec((gb, ns, T * C), lambda s: (s, 0, 0))],
        out_specs=pl.BlockSpec((nc, gb * ns), lambda s: (0, s)),
        out_shape=jax.ShapeDtypeStruct((nc, G * ns), F32),
        compiler_params=pltpu.CompilerParams(dimension_semantics=("parallel",)),
        name="ssm_state",
    )(ut, gm)


def _ssm_scan_kernel(are_ref, aim_ref, ldt_ref, s_ref, o_ref, x_ref, xs_ref, *, T, n0, seq_starts, seq_ends):
    d = pl.program_id(0)
    j = pl.program_id(1)
    nblk = pl.num_programs(1)
    blk = j + d * (nblk - 1 - 2 * j)
    P = SSM_STATE

    is_start = functools.reduce(jnp.logical_or, [blk == s for s in seq_starts])
    is_end = functools.reduce(jnp.logical_or, [blk == e for e in seq_ends])
    reset = jnp.where(d == 0, is_start, is_end)

    @pl.when(reset)
    def _():
        x_ref[...] = jnp.zeros_like(x_ref)
        xs_ref[...] = jnp.zeros_like(xs_ref)

    dt = jnp.exp(ldt_ref[...])
    zr = are_ref[...] * dt
    zi = aim_ref[...] * dt
    mag = jnp.exp(float(T) * zr)
    mr = mag * jnp.cos(float(T) * zi)
    mi = mag * jnp.sin(float(T) * zi)
    lane = lax.broadcasted_iota(jnp.int32, mr.shape, 1)
    m2 = jnp.where(lane < P, -mi, mi)
    m2s = -m2

    def body(k, carry):
        x, xs = carry
        row = k + d * (n0 - 1 - 2 * k)
        loc = s_ref[row]
        o_ref[row] = x.astype(o_ref.dtype)
        locs = pltpu.roll(loc, P, 1)
        return x * mr + xs * m2 + loc, xs * mr + x * m2s + locs

    x, xs = lax.fori_loop(0, n0, body, (x_ref[...], xs_ref[...]), unroll=4)
    x_ref[...] = x
    xs_ref[...] = xs


def _ssm_scan(s_loc, a_re, a_im, log_dt, *, T, n_unit_seqs):
    G, P = SSM_GROUPS, SSM_STATE
    nc = s_loc.shape[0]
    n0 = SEQ_BLOCK // T
    nblk = nc // n0
    seq_starts = tuple(range(n_unit_seqs + 1))
    seq_ends = tuple(range(n_unit_seqs)) + (nblk - 1,)
    dup = lambda x: jnp.concatenate([x, x], axis=-1)
    are2, aim2 = dup(a_re), dup(a_im)
    ldt2 = jnp.broadcast_to(log_dt[:, :, None], (2, G, 2 * P))

    def blk_map(d, j):
        return (j + d * (nblk - 1 - 2 * j), 0, d)

    par = pl.BlockSpec((None, G, 2 * P), lambda d, j: (d, 0, 0))
    return pl.pallas_call(
        functools.partial(_ssm_scan_kernel, T=T, n0=n0, seq_starts=seq_starts, seq_ends=seq_ends),
        grid=(2, nblk),
        in_specs=[par, par, par, pl.BlockSpec((n0, G, 2 * P), blk_map)],
        out_specs=pl.BlockSpec((n0, G, 2 * P), blk_map),
        out_shape=jax.ShapeDtypeStruct((nc, G, 4 * P), BF16),
        scratch_shapes=[pltpu.VMEM((G, 2 * P), F32), pltpu.VMEM((G, 2 * P), F32)],
        compiler_params=pltpu.CompilerParams(dimension_semantics=("arbitrary", "arbitrary")),
        name="ssm_scan",
    )(are2, aim2, ldt2, s_loc)


def _ssm_out_kernel(u_ref, mt_ref, s_ref, cs_ref, y_ref, *, gb, ns):
    nc = u_ref.shape[-1]
    for gi in range(gb):
        u = u_ref[gi].reshape(-1, nc)
        y = _dot(mt_ref[gi], u) + _dot_nt(cs_ref[gi], s_ref[:, gi * ns:(gi + 1) * ns])
        y_ref[gi] = _gelu_tanh(y).astype(y_ref.dtype).reshape(y_ref.shape[1:])


def _ssm_out(ut, mt, s_in, cs):
    G, T, C, nc = ut.shape
    ns = cs.shape[2]
    gb = SSM_GROUPS_PER_STEP
    return pl.pallas_call(
        functools.partial(_ssm_out_kernel, gb=gb, ns=ns),
        grid=(G // gb,),
        in_specs=[pl.BlockSpec((gb, T, C, nc), lambda s: (s, 0, 0, 0)),
                  pl.BlockSpec((gb, T * C, T * C), lambda s: (s, 0, 0)),
                  pl.BlockSpec((nc, gb * ns), lambda s: (0, s)),
                  pl.BlockSpec((gb, T * C, ns), lambda s: (s, 0, 0))],
        out_specs=pl.BlockSpec((gb, T, C, nc), lambda s: (s, 0, 0, 0)),
        out_shape=jax.ShapeDtypeStruct((G, T, C, nc), BF16),
        compiler_params=pltpu.CompilerParams(dimension_semantics=("parallel",)),
        name="ssm_out",
    )(ut, mt, s_in, cs)


def _ssm_tok_kernel(y_ref, o_ref, *stage_refs):
    T, ncb = y_ref.shape[1], y_ref.shape[-1]
    eye = (lax.broadcasted_iota(jnp.int32, (ncb, ncb), 0)
           == lax.broadcasted_iota(jnp.int32, (ncb, ncb), 1)).astype(BF16)
    for r in range(T):
        yt = y_ref[:, r].reshape(-1, ncb)
        tok = _dot_nt(eye, yt)
        for s, stage in enumerate(stage_refs):
            stage[pl.ds(r, ncb, stride=T), :] = tok[:, s * V7X_LANES:(s + 1) * V7X_LANES]
    for s, stage in enumerate(stage_refs):
        o_ref[:, s * V7X_LANES:(s + 1) * V7X_LANES] = stage[...].astype(o_ref.dtype)


def _ssm_tok(yt):
    G, T, C, nc = yt.shape
    ncb = SSM_TOK_CHUNKS
    est = 2 * G * T * C * ncb * 2 + ncb * T * G * C * 4 + 2 * ncb * T * G * C * 2
    return pl.pallas_call(
        _ssm_tok_kernel,
        grid=(nc // ncb,),
        in_specs=[pl.BlockSpec((G, T, C, ncb), lambda s: (0, 0, 0, s))],
        out_specs=pl.BlockSpec((ncb * T, G * C), lambda s: (s, 0)),
        out_shape=jax.ShapeDtypeStruct((nc * T, G * C), BF16),
        scratch_shapes=[pltpu.VMEM((ncb * T, V7X_LANES), F32)] * (G * C // V7X_LANES),
        compiler_params=pltpu.CompilerParams(
            dimension_semantics=("parallel",), vmem_limit_bytes=_vmem_limit(est)),
        name="ssm_tok",
    )(yt)


def _s5_mixer_gelu(xb, wt, tables, a_re, a_im, log_dt, *, n_unit_seqs):
    T, G, P = SSM_CHUNK, SSM_GROUPS, SSM_STATE
    nc = xb.shape[0] // T
    mt, gm, cs = tables
    ut = _ssm_in(xb.reshape(nc, T * xb.shape[1]), wt, T=T)
    s_loc = _ssm_state(ut, gm)
    s_in = _ssm_scan(s_loc.reshape(nc, G, 4 * P), a_re, a_im, log_dt, T=T, n_unit_seqs=n_unit_seqs)
    yt = _ssm_out(ut, mt, s_in.reshape(nc, G * 4 * P), cs)
    return _ssm_tok(yt)


def _kv_kernel(m_ref, g_ref, b_ref, w_ref, o_ref, mb_ref):
    @pl.when(pl.program_id(0) == 0)
    def _():
        mb_ref[...] = _layernorm_rows(m_ref[...], g_ref[...], b_ref[...]).astype(BF16)

    o_ref[...] = _dot(mb_ref[...], w_ref[...]).astype(o_ref.dtype)


def _kv_proj(mem, ln_g, ln_b, w_kv):
    n, d = mem.shape
    m = w_kv.shape[1]
    tn = KV_TN
    est = 2 * n * d * 4 + n * d * 2 + 2 * d * tn * 2 + 2 * n * tn * 2 + 3 * n * d * 4
    return pl.pallas_call(
        _kv_kernel,
        grid=(m // tn,),
        in_specs=[pl.BlockSpec((n, d), lambda j: (0, 0)),
                  pl.BlockSpec((1, d), lambda j: (0, 0)),
                  pl.BlockSpec((1, d), lambda j: (0, 0)),
                  pl.BlockSpec((d, tn), lambda j: (0, j))],
        out_specs=pl.BlockSpec((n, tn), lambda j: (0, j)),
        out_shape=jax.ShapeDtypeStruct((n, m), BF16),
        scratch_shapes=[pltpu.VMEM((n, d), BF16)],
        compiler_params=pltpu.CompilerParams(
            dimension_semantics=("arbitrary",), vmem_limit_bytes=_vmem_limit(est)),
        name="kv_proj",
    )(mem, ln_g, ln_b, w_kv)


def _mixer_kernel(ya_ref, ub_ref, ob_ref, g0_ref, g1_ref, g2_ref,
                  wa_ref, wb_ref, wgp_ref, wap_ref, o_ref):
    ya = ya_ref[...]
    y_a = _dot(ya, wa_ref[...]) * _sigmoid(_dot(ya, wb_ref[...]))
    y_b = _dot(ub_ref[...], wgp_ref[...])
    y_c = _dot(ob_ref[...], wap_ref[...])
    merged = (_sigmoid(g0_ref[...].astype(F32)) * y_a
              + _sigmoid(g1_ref[...].astype(F32)) * y_b
              + _sigmoid(g2_ref[...].astype(F32)) * y_c)
    o_ref[...] = merged.astype(o_ref.dtype)


def _mixer(ya, ub, ob, proj, w_glu, w_gproj, w_aproj):
    n, wid = ya.shape
    d = D_MODEL
    tm, tn = MIX_TM, MIX_TN
    nn = d // tn
    gate0 = (proj.shape[1] - 3 * d) // tn
    gate_spec = lambda br: pl.BlockSpec((tm, tn), lambda i, j: (i, gate0 + br * nn + j))
    act_spec = pl.BlockSpec((tm, wid), lambda i, j: (i, 0))
    est = (2 * 3 * tm * wid * 2 + 2 * 3 * tm * tn * 2 + 2 * 4 * wid * tn * 2 + 2 * tm * tn * 2
           + 8 * tm * tn * 4)
    return pl.pallas_call(
        _mixer_kernel,
        grid=(n // tm, nn),
        in_specs=[
            act_spec, act_spec, act_spec,
            gate_spec(0), gate_spec(1), gate_spec(2),
            pl.BlockSpec((wid, tn), lambda i, j: (0, j)),
            pl.BlockSpec((wid, tn), lambda i, j: (0, nn + j)),
            pl.BlockSpec((wid, tn), lambda i, j: (0, j)),
            pl.BlockSpec((wid, tn), lambda i, j: (0, j)),
        ],
        out_specs=pl.BlockSpec((tm, tn), lambda i, j: (i, j)),
        out_shape=jax.ShapeDtypeStruct((n, d), BF16),
        compiler_params=pltpu.CompilerParams(
            dimension_semantics=("parallel", "parallel"),
            vmem_limit_bytes=_vmem_limit(est)),
        name="mixer",
    )(ya, ub, ob, proj, proj, proj, w_glu, w_glu, w_gproj, w_aproj)


def _outproj_ln_kernel(m_ref, w_ref, x_ref, g_ref, b_ref, o_ref):
    y = ALPHA * x_ref[...] + _dot(m_ref[...], w_ref[...])
    o_ref[...] = _layernorm_rows(y, g_ref[...], b_ref[...])


def _outproj_ln(merged, w_out, x, ln_g, ln_b):
    n, d = x.shape
    tm = OUT_TM
    est = 2 * tm * d * 2 + 2 * d * d * 2 + 4 * tm * d * 4 + 2 * tm * d * 4
    return pl.pallas_call(
        _outproj_ln_kernel,
        grid=(n // tm,),
        in_specs=[pl.BlockSpec((tm, d), lambda i: (i, 0)),
                  pl.BlockSpec((d, d), lambda i: (0, 0)),
                  pl.BlockSpec((tm, d), lambda i: (i, 0)),
                  pl.BlockSpec((1, d), lambda i: (0, 0)),
                  pl.BlockSpec((1, d), lambda i: (0, 0))],
        out_specs=pl.BlockSpec((tm, d), lambda i: (i, 0)),
        out_shape=jax.ShapeDtypeStruct((n, d), F32),
        compiler_params=pltpu.CompilerParams(
            dimension_semantics=("parallel",), vmem_limit_bytes=_vmem_limit(est)),
        name="outproj_ln",
    )(merged, w_out, x, ln_g, ln_b)


def _layer(groups, ffn1_w_gu, ffn1_w_down, ln1_g, ln1_b, w_in,
           ssm_a_re, ssm_a_im, ssm_log_dt, ssm_b_re, ssm_b_im, ssm_c_re, ssm_c_im, ssm_d, ssm_w_glu,
           gmlp_ln_g, gmlp_ln_b, gmlp_w_s, gmlp_b_s, gmlp_w_proj,
           mem_ln_g, mem_ln_b, attn_w_kv, attn_w_proj,
           w_out, ln2_g, ln2_b, ffn2_w_gu, ffn2_w_down, ln3_g, ln3_b):
    row = lambda v: v.reshape(1, -1)
    ffn1 = _prep_ffn_weights(ffn1_w_gu, ffn1_w_down)
    ffn2 = _prep_ffn_weights(ffn2_w_gu, ffn2_w_down)
    w_in_b = w_in.astype(BF16)
    wt_ssm = _ssm_weight_t(w_in)
    tables = _ssm_tables(ssm_a_re, ssm_a_im, ssm_log_dt, ssm_b_re, ssm_b_im, ssm_c_re, ssm_c_im,
                         ssm_d, T=SSM_CHUNK)
    w_kv, w_s = attn_w_kv.astype(BF16), gmlp_w_s.astype(BF16)
    w_glu, w_gproj, w_aproj = ssm_w_glu.astype(BF16), gmlp_w_proj.astype(BF16), attn_w_proj.astype(BF16)
    w_out_b = w_out.astype(BF16)
    outs = []
    for x, mem, n_unit_seqs in groups:
        x1 = _ffn_ln(x, ffn1, row(ln1_g), row(ln1_b))
        kv = _kv_proj(mem, row(mem_ln_g), row(mem_ln_b), w_kv)
        proj, x1b = _in_proj(x1, w_in_b, row(gmlp_ln_g), row(gmlp_ln_b))
        ub, ob = _branches(proj, kv, w_s, gmlp_b_s)
        ya = _s5_mixer_gelu(x1b, wt_ssm, tables, ssm_a_re, ssm_a_im, ssm_log_dt,
                            n_unit_seqs=n_unit_seqs)
        merged = _mixer(ya, ub, ob, proj, w_glu, w_gproj, w_aproj)
        x2 = _outproj_ln(merged, w_out_b, x1, row(ln2_g), row(ln2_b))
        outs.append(_ffn_ln(x2, ffn2, row(ln3_g), row(ln3_b)))
    return outs


def kernel(x_prompt, x_sample, mem_prompt, mem_sample, ffn1_w_gu, ffn1_w_down, ln1_g, ln1_b, w_in,
           ssm_a_re, ssm_a_im, ssm_log_dt, ssm_b_re, ssm_b_im, ssm_c_re, ssm_c_im, ssm_d, ssm_w_glu,
           gmlp_ln_g, gmlp_ln_b, gmlp_w_s, gmlp_b_s, gmlp_w_proj, mem_ln_g, mem_ln_b, attn_w_kv,
           attn_w_proj, w_out, ln2_g, ln2_b, ffn2_w_gu, ffn2_w_down, ln3_g, ln3_b):
    d = x_prompt.shape[-1]
    assert x_prompt.shape[1] == SEQ_BLOCK and x_sample.shape[0] == 1
    assert x_sample.shape[1] % SEQ_BLOCK == 0
    xs = [x_prompt.reshape(-1, d), x_sample.reshape(-1, d)]
    mems = [mem_prompt.reshape(-1, d), mem_sample.reshape(-1, d)]
    n_unit = [x_prompt.shape[0], 0]
    for l in range(DEPTH):
        xs = _layer(list(zip(xs, mems, n_unit)),
                    ffn1_w_gu[l], ffn1_w_down[l], ln1_g[l], ln1_b[l], w_in[l],
                    ssm_a_re[l], ssm_a_im[l], ssm_log_dt[l], ssm_b_re[l], ssm_b_im[l],
                    ssm_c_re[l], ssm_c_im[l], ssm_d[l], ssm_w_glu[l],
                    gmlp_ln_g[l], gmlp_ln_b[l], gmlp_w_s[l], gmlp_b_s[l], gmlp_w_proj[l],
                    mem_ln_g[l], mem_ln_b[l], attn_w_kv[l], attn_w_proj[l],
                    w_out[l], ln2_g[l], ln2_b[l], ffn2_w_gu[l], ffn2_w_down[l], ln3_g[l], ln3_b[l])
    return (xs[0].reshape(x_prompt.shape), xs[1].reshape(x_sample.shape))
```

```python
import functools
import math

import jax
import jax.numpy as jnp
from jax import lax
from jax.experimental import pallas as pl
from jax.experimental.pallas import tpu as pltpu

F32 = jnp.float32
BF16 = jnp.bfloat16

D_MODEL = 2048
DEPTH = 1
SEQ_BLOCK = 4096
N_MEM = 256
SSM_WIDTH = D_MODEL // 2
SSM_GROUP = 16
SSM_GROUPS = SSM_WIDTH // SSM_GROUP
SSM_STATE = 64
GMLP_WIDTH = D_MODEL // 2
GMLP_CHUNK = 128
GMLP_HEADS = 8
GMLP_HEAD_DIM = GMLP_WIDTH // GMLP_HEADS
ATTN_HEADS = 4
ATTN_HEAD_DIM = D_MODEL // 8
ATTN_WIDTH = ATTN_HEADS * ATTN_HEAD_DIM
D_FF = 5504
ALPHA = (2.0 * DEPTH) ** 0.25
LN_EPS = 1e-5

V7X_LANES = 128
V7X_SUBLANES = 8
V7X_VMEM_BYTES = 64 * 1024 * 1024

SSM_CHUNK = 32
SSM_GROUPS_PER_STEP = 8
SSM_TABLE_GROUPS = 4
SSM_IN_POSITIONS = 4
SSM_TOK_CHUNKS = 128
FFN_TM = 512
FFN_TF = 1024
PROJ_TM = 1024
PROJ_TN = 1024
BRANCH_TM = 512
MIX_TM = 512
MIX_TN = 1024
OUT_TM = 512
KV_TN = 512


def _vmem_limit(nbytes):
    return int(min(nbytes + (16 << 20), V7X_VMEM_BYTES - (4 << 20)))


def _layernorm_rows(y, g, b):
    mu = jnp.mean(y, axis=-1, keepdims=True)
    yc = y - mu
    var = jnp.mean(yc * yc, axis=-1, keepdims=True)
    return yc * lax.rsqrt(var + LN_EPS) * g + b


def _gelu_tanh(x):
    c = math.sqrt(2.0 / math.pi)
    return 0.5 * x * (1.0 + jnp.tanh(c * (x + 0.044715 * (x * x * x))))


def _sigmoid(x):
    return 1.0 / (1.0 + jnp.exp(-x))


def _dot(a, b):
    return jnp.dot(a, b, preferred_element_type=F32)


def _dot_nt(a, b):
    return lax.dot_general(a, b, (((1,), (1,)), ((), ())), preferred_element_type=F32)


def _swiglu_down(xb, wg, wu, wd):
    gate = _dot(xb, wg)
    up = _dot(xb, wu)
    act = (gate * _sigmoid(gate) * up).astype(BF16)
    return _dot(act, wd)


def _ffn_ln_kernel(x_ref, wg_ref, wu_ref, wd_ref, wgt_ref, wut_ref, wdt_ref, g_ref, b_ref,
                   o_ref, xb_ref):
    j = pl.program_id(1)

    @pl.when(j == 0)
    def _():
        xb_ref[...] = x_ref[...].astype(BF16)
        o_ref[...] = jnp.zeros_like(o_ref)

    o_ref[...] += _swiglu_down(xb_ref[...], wg_ref[...], wu_ref[...], wd_ref[...])

    @pl.when(j == pl.num_programs(1) - 1)
    def _():
        acc = o_ref[...] + _swiglu_down(xb_ref[...], wgt_ref[...], wut_ref[...], wdt_ref[...])
        y = ALPHA * x_ref[...] + 0.5 * acc
        o_ref[...] = _layernorm_rows(y, g_ref[...], b_ref[...])


def _ffn_ln(x, weights, ln_g, ln_b):
    wg, wu, wd, wgt, wut, wdt = weights
    n, d = x.shape
    tm, tf = FFN_TM, FFN_TF
    nf = wd.shape[0] // tf
    ft = wdt.shape[0]
    once = dict(pipeline_mode=pl.Buffered(1))
    est = (2 * tm * d * 4 + 2 * tm * d * 4 + tm * d * 2
           + 2 * 3 * (d * tf * 2) + 3 * (d * ft * 2) + 3 * tm * tf * 4)
    return pl.pallas_call(
        _ffn_ln_kernel,
        grid=(n // tm, nf),
        in_specs=[
            pl.BlockSpec((tm, d), lambda i, j: (i, 0)),
            pl.BlockSpec((d, tf), lambda i, j: (0, j)),
            pl.BlockSpec((d, tf), lambda i, j: (0, j)),
            pl.BlockSpec((tf, d), lambda i, j: (j, 0)),
            pl.BlockSpec((d, ft), lambda i, j: (0, 0), **once),
            pl.BlockSpec((d, ft), lambda i, j: (0, 0), **once),
            pl.BlockSpec((ft, d), lambda i, j: (0, 0), **once),
            pl.BlockSpec((1, d), lambda i, j: (0, 0)),
            pl.BlockSpec((1, d), lambda i, j: (0, 0)),
        ],
        out_specs=pl.BlockSpec((tm, d), lambda i, j: (i, 0)),
        out_shape=jax.ShapeDtypeStruct((n, d), F32),
        scratch_shapes=[pltpu.VMEM((tm, d), BF16)],
        compiler_params=pltpu.CompilerParams(
            dimension_semantics=("parallel", "arbitrary"),
            vmem_limit_bytes=_vmem_limit(est)),
        name="ffn_ln",
    )(x, wg, wu, wd, wgt, wut, wdt, ln_g, ln_b)


def _ffn_weight_set(wg, wu, wd):
    full = (D_FF // FFN_TF) * FFN_TF
    return wg, wu, wd, wg[:, full:], wu[:, full:], wd[full:]


def _in_proj_kernel(x_ref, w_ref, lng_ref, lnb_ref, *refs, casts):
    n_src = len(casts)
    src_refs = refs[:n_src]
    o_ref, xb_ref = refs[n_src:n_src + 2]
    dst_refs = list(refs[n_src + 2:])
    j = pl.program_id(1)

    @pl.when(j == 0)
    def _():
        xb_ref[...] = x_ref[...].astype(BF16)

    def convert_weights():
        dsts = iter(dst_refs)
        for src, splits in zip(src_refs, casts):
            for lo, hi in splits:
                next(dsts)[...] = src[:, lo:hi].astype(BF16)

    @pl.when(j == 0)
    def _():
        convert_weights()
        o_ref[...] = _gelu_tanh(_dot(xb_ref[...], w_ref[...])).astype(o_ref.dtype)

    @pl.when(j == 1)
    def _():
        convert_weights()
        v = _gelu_tanh(_dot(xb_ref[...], w_ref[...]))
        o_ref[...] = _layernorm_rows(v, lng_ref[...], lnb_ref[...]).astype(o_ref.dtype)

    @pl.when(j >= 2)
    def _():
        convert_weights()
        o_ref[...] = _dot(xb_ref[...], w_ref[...]).astype(o_ref.dtype)


def _in_proj(x, w, gln_g, gln_b, cast_jobs):
    n, k = x.shape
    tm, tn = PROJ_TM, PROJ_TN
    assert tn == GMLP_WIDTH
    skip = SSM_WIDTH // tn
    m = w.shape[1] - SSM_WIDTH
    ncol = m // tn
    steps = (n // tm) * ncol
    est = 2 * tm * k * 4 + 2 * tm * k * 2 + 2 * k * tn * 2 + 2 * tm * tn * 2 + 3 * tm * tn * 4
    src_specs, dst_specs, dst_shapes, casts = [], [], [], []
    for src, rows, splits in cast_jobs:
        nblk = src.shape[0] // rows
        assert src.shape[0] % rows == 0 and nblk <= steps
        blk_map = lambda i, j, nblk=nblk: (jnp.minimum(i * ncol + j, nblk - 1), 0)
        src_specs.append(pl.BlockSpec((rows, src.shape[1]), blk_map))
        for lo, hi in splits:
            dst_specs.append(pl.BlockSpec((rows, hi - lo), blk_map))
            dst_shapes.append(jax.ShapeDtypeStruct((src.shape[0], hi - lo), BF16))
        casts.append(tuple(splits))
        est += 2 * rows * src.shape[1] * 6
    res = pl.pallas_call(
        functools.partial(_in_proj_kernel, casts=tuple(casts)),
        grid=(n // tm, ncol),
        in_specs=[pl.BlockSpec((tm, k), lambda i, j: (i, 0)),
                  pl.BlockSpec((k, tn), lambda i, j: (0, skip + j)),
                  pl.BlockSpec((1, tn), lambda i, j: (0, 0)),
                  pl.BlockSpec((1, tn), lambda i, j: (0, 0))] + src_specs,
        out_specs=[pl.BlockSpec((tm, tn), lambda i, j: (i, j)),
                   pl.BlockSpec((tm, k), lambda i, j: (i, 0))] + dst_specs,
        out_shape=[jax.ShapeDtypeStruct((n, m), BF16),
                   jax.ShapeDtypeStruct((n, k), BF16)] + dst_shapes,
        compiler_params=pltpu.CompilerParams(
            dimension_semantics=("arbitrary", "arbitrary"),
            vmem_limit_bytes=_vmem_limit(est)),
        name="in_proj",
    )(x, w, gln_g, gln_b, *[job[0] for job in cast_jobs])
    return res[0], res[1], res[2:]


def _branches_kernel(u_ref, v_ref, q_ref, kv_ref, ws_ref, bs_ref, ub_ref, ob_ref, *, tm):
    hd = GMLP_HEAD_DIM

    def chunk_body(ci, carry):
        rows = pl.ds(pl.multiple_of(ci * GMLP_CHUNK, GMLP_CHUNK), GMLP_CHUNK)
        for h in range(GMLP_HEADS):
            cols = slice(h * hd, (h + 1) * hd)
            mixed = _dot(ws_ref[h], v_ref[rows, cols]) + bs_ref[:, h:h + 1]
            ub_ref[rows, cols] = (u_ref[rows, cols].astype(F32) * mixed).astype(BF16)
        return carry

    lax.fori_loop(0, tm // GMLP_CHUNK, chunk_body, 0)

    ad = ATTN_HEAD_DIM
    scale = ad ** -0.5
    for h in range(ATTN_HEADS):
        cols = slice(h * ad, (h + 1) * ad)
        vcols = slice(ATTN_WIDTH + h * ad, ATTN_WIDTH + (h + 1) * ad)
        s = _dot_nt(q_ref[:, cols], kv_ref[:, cols]) * scale
        p = jnp.exp(s - jnp.max(s, axis=-1, keepdims=True))
        p = p * (1.0 / jnp.sum(p, axis=-1, keepdims=True))
        ob_ref[:, cols] = _dot(p.astype(BF16), kv_ref[:, vcols]).astype(BF16)


def _branches(proj, kv, w_s, b_s):
    n = proj.shape[0]
    tm = BRANCH_TM
    wid = GMLP_WIDTH
    assert wid == ATTN_WIDTH
    blocks_per_seq = SEQ_BLOCK // tm
    n_mem_batches = kv.shape[0] // N_MEM
    col = lambda c: pl.BlockSpec((tm, wid), lambda i: (i, c))
    full2 = lambda a: pl.BlockSpec(a.shape, lambda i: (0, 0))
    out = pl.BlockSpec((tm, wid), lambda i: (i, 0))
    return pl.pallas_call(
        functools.partial(_branches_kernel, tm=tm),
        grid=(n // tm,),
        in_specs=[col(0), col(1), col(2),
                  pl.BlockSpec((N_MEM, 2 * ATTN_WIDTH),
                               lambda i: (jnp.minimum(i // blocks_per_seq, n_mem_batches - 1), 0)),
                  pl.BlockSpec(w_s.shape, lambda i: (0, 0, 0)),
                  full2(b_s)],
        out_specs=[out, out],
        out_shape=[jax.ShapeDtypeStruct((n, wid), BF16), jax.ShapeDtypeStruct((n, wid), BF16)],
        compiler_params=pltpu.CompilerParams(dimension_semantics=("parallel",)),
        name="branches",
    )(proj, proj, proj, kv, w_s, b_s)


def _ssm_weight_t_kernel(w_ref, o_ref):
    o_ref[...] = w_ref[...].T.astype(o_ref.dtype)


def _ssm_weight_t(w_in):
    d = w_in.shape[0]
    tn = 2 * V7X_LANES
    return pl.pallas_call(
        _ssm_weight_t_kernel,
        grid=(SSM_WIDTH // tn,),
        in_specs=[pl.BlockSpec((d, tn), lambda j: (0, j))],
        out_specs=pl.BlockSpec((tn, d), lambda j: (j, 0)),
        out_shape=jax.ShapeDtypeStruct((SSM_WIDTH, d), BF16),
        compiler_params=pltpu.CompilerParams(dimension_semantics=("parallel",)),
        name="ssm_weight_t",
    )(w_in)


def _ssm_in_kernel(x_ref, wt_ref, o_ref, *, rb):
    d = wt_ref.shape[1]
    for k in range(rb):
        ut = _dot_nt(wt_ref[...], x_ref[:, k * d:(k + 1) * d])
        o_ref[:, k] = ut.astype(BF16).reshape(o_ref.shape[0], o_ref.shape[2], o_ref.shape[3])


def _ssm_in(xc, wt, *, T):
    nc = xc.shape[0]
    d = xc.shape[1] // T
    G, C = SSM_GROUPS, SSM_GROUP
    rb = SSM_IN_POSITIONS
    est = 2 * nc * rb * d * 2 + 2 * G * C * d * 2 + 2 * rb * G * C * nc * 2 + 2 * G * C * nc * 4
    return pl.pallas_call(
        functools.partial(_ssm_in_kernel, rb=rb),
        grid=(T // rb,),
        in_specs=[pl.BlockSpec((nc, rb * d), lambda r: (0, r)),
                  pl.BlockSpec((G * C, d), lambda r: (0, 0))],
        out_specs=pl.BlockSpec((G, rb, C, nc), lambda r: (0, r, 0, 0)),
        out_shape=jax.ShapeDtypeStruct((G, T, C, nc), BF16),
        compiler_params=pltpu.CompilerParams(
            dimension_semantics=("parallel",), vmem_limit_bytes=_vmem_limit(est)),
        name="ssm_in",
    )(xc, wt)


def _cmul(x, y):
    return x[0] * y[0] - x[1] * y[1], x[0] * y[1] + x[1] * y[0]


def _split_hi_lo(x):
    hi = x.astype(BF16)
    lo = (x - hi.astype(F32)).astype(BF16)
    return hi, lo


def _dot_hi_lo(a, b):
    ah, al = _split_hi_lo(a)
    bh, bl = _split_hi_lo(b)
    return _dot(ah, bh) + _dot(al, bh) + _dot(ah, bl)


def _ssm_tables_kernel(*refs, T, gb):
    for gi in range(gb):
        _ssm_tables_group(*[r.at[gi] for r in refs], T=T)


def _ssm_tables_group(pcol_ref, prow_ref, bcat_ref, cre_ref, cim_ref, dtile_ref,
                      mt_ref, gm_ref, cs_ref, q_ref, *, T):
    P, C = SSM_STATE, SSM_GROUP
    rpt = V7X_LANES // C
    nt = T // rpt
    wide = 2 * T * C

    lane = lax.broadcasted_iota(jnp.int32, (1, V7X_LANES), 1)
    rr = lax.shift_right_logical(lane, 4)
    expand = (lax.broadcasted_iota(jnp.int32, (C, V7X_LANES), 0)
              == (lax.broadcasted_iota(jnp.int32, (C, V7X_LANES), 1) & (C - 1))).astype(F32)

    pc = pcol_ref[...]

    lane4 = lax.broadcasted_iota(jnp.int32, (1, 4 * P), 1)
    is_re = (lax.shift_right_logical(lane4, 6) & 1) == 0
    is_f = lane4 < 2 * P
    pr = prow_ref[...]
    dt4 = jnp.exp(pr[2:3])
    zr4, zi4 = pr[0:1] * dt4, pr[1:2] * dt4

    kp = ((T + 1 + 7) // 8) * 8
    krow = lax.broadcasted_iota(jnp.int32, (kp, 4 * P), 0).astype(F32)
    mag = jnp.exp(krow * zr4)
    pw_r, pw_i = mag * jnp.cos(krow * zi4), mag * jnp.sin(krow * zi4)

    def states_on_rows(tab, lo):
        slab = tab[:, lo:lo + V7X_LANES]
        padded = jnp.concatenate([slab, jnp.zeros((V7X_LANES - kp, V7X_LANES), F32)], axis=0)
        return padded.T[0:P]

    pt_f = states_on_rows(pw_r, 0), states_on_rows(pw_i, 0)
    pt_b = states_on_rows(pw_r, 2 * P), states_on_rows(pw_i, 2 * P)

    def stair(pt, descending):
        re = jnp.zeros((P, V7X_LANES), F32)
        im = re
        for k in range(rpt):
            m = rpt - 1 - k if descending else k
            re = jnp.where(rr == k, pt[0][:, m:m + 1], re)
            im = jnp.where(rr == k, pt[1][:, m:m + 1], im)
        return re, im

    def zoh(d, pt):
        are = pc[:, d:d + 1]
        aim = pc[:, 2 + d:3 + d]
        nr = pt[0][:, 1:2] - 1.0
        ni = pt[1][:, 1:2]
        den = are * are + aim * aim
        return (nr * are + ni * aim) / den, (ni * are - nr * aim) / den

    def col(pt, m):
        return pt[0][:, m:m + 1], pt[1][:, m:m + 1]

    dsc_f = stair(pt_f, True)
    asc_b = stair(pt_b, False)
    btile = jnp.dot(bcat_ref[...], expand, precision=lax.Precision.HIGHEST,
                    preferred_element_type=F32)
    bt = (btile[0:P], btile[P:2 * P])
    bb_f = _cmul(zoh(0, pt_f), bt)
    bd_f = _cmul(dsc_f, bb_f)
    ba_b = _cmul(asc_b, _cmul(zoh(1, pt_b), bt))

    for j in range(nt):
        cols = slice(j * V7X_LANES, (j + 1) * V7X_LANES)
        xf = _cmul(bd_f, col(pt_f, T - rpt - rpt * j))
        xb = _cmul(ba_b, col(pt_b, rpt * j))
        gm_ref[0:P, cols] = xf[0].astype(BF16)
        gm_ref[P:2 * P, cols] = xf[1].astype(BF16)
        gm_ref[2 * P:3 * P, cols] = xb[0].astype(BF16)
        gm_ref[3 * P:4 * P, cols] = xb[1].astype(BF16)

    zeros = jnp.zeros((2 * P, V7X_LANES), F32)
    for j in range(2 * nt):
        cols = slice(j * V7X_LANES, (j + 1) * V7X_LANES)
        if j < nt:
            qf = _cmul(bd_f, col(pt_f, T - rpt * j - (rpt - 1)))
            q_ref[0:P, cols] = qf[0]
            q_ref[P:2 * P, cols] = qf[1]
            q_ref[2 * P:4 * P, cols] = zeros
        else:
            qb = _cmul(ba_b, col(pt_b, rpt * j - T))
            q_ref[2 * P:3 * P, cols] = qb[0]
            q_ref[3 * P:4 * P, cols] = qb[1]
            if j == nt:
                center = rr == 0
                q_ref[0:P, cols] = jnp.where(center, bb_f[0], 0.0)
                q_ref[P:2 * P, cols] = jnp.where(center, bb_f[1], 0.0)
            else:
                q_ref[0:2 * P, cols] = zeros

    cre = cre_ref[...]
    cim = cim_ref[...]

    for r in range(T):
        prr = jnp.where(is_f, pw_r[r + 1:r + 2], pw_r[T - r:T - r + 1])
        pii = jnp.where(is_f, pw_i[r + 1:r + 2], pw_i[T - r:T - r + 1])
        blk = jnp.where(is_re, cre * prr - cim * pii, -(cre * pii + cim * prr))
        cs_ref[r * C:(r + 1) * C, :] = blk.astype(BF16)

    lhs = jnp.where(is_re, cre, -cim)
    zt = _dot_hi_lo(lhs, q_ref[...])
    lanew = lax.broadcasted_iota(jnp.int32, (C, wide), 1)
    roww = lax.broadcasted_iota(jnp.int32, (C, wide), 0)
    diag = (lax.shift_right_logical(lanew, 4) == T) & ((lanew & (C - 1)) == roww)
    zt = zt + jnp.where(diag, dtile_ref[...], 0.0)

    for r in range(T):
        off = (T - r) * C
        shifted = pltpu.roll(zt, (wide - off) % wide, 1)
        mt_ref[r * C:(r + 1) * C, :] = shifted[:, :T * C].astype(BF16)


def _ssm_tables(a_re, a_im, log_dt, b_re, b_im, c_re, c_im, d_skip, *, T):
    G, P, C = SSM_GROUPS, SSM_STATE, SSM_GROUP
    ldt = jnp.broadcast_to(log_dt[:, :, None], (2, G, P))
    zc = jnp.zeros((G, P), F32)
    pcol = jnp.stack([a_re[0], a_re[1], a_im[0], a_im[1], ldt[0], ldt[1], zc, zc], axis=-1)

    def row4(x):
        return jnp.concatenate([x[0], x[0], x[1], x[1]], axis=-1)

    zr4 = jnp.zeros((G, 4 * P), F32)
    prow = jnp.stack([row4(a_re), row4(a_im), row4(ldt), zr4, zr4, zr4, zr4, zr4], axis=1)
    bcat = jnp.concatenate([b_re, b_im], axis=1)
    cre4 = row4(c_re)
    cim4 = row4(c_im)
    dtile = jnp.tile(d_skip, (1, 2 * T))[:, None, :]

    tc = T * C
    gb = SSM_TABLE_GROUPS
    blk = lambda *shape: pl.BlockSpec((gb,) + shape, lambda g: (g,) + (0,) * len(shape))
    return pl.pallas_call(
        functools.partial(_ssm_tables_kernel, T=T, gb=gb),
        grid=(G // gb,),
        in_specs=[blk(P, 8), blk(8, 4 * P), blk(2 * P, C), blk(C, 4 * P), blk(C, 4 * P),
                  blk(1, 2 * tc)],
        out_specs=[blk(tc, tc), blk(4 * P, tc), blk(tc, 4 * P)],
        out_shape=[
            jax.ShapeDtypeStruct((G, tc, tc), BF16),
            jax.ShapeDtypeStruct((G, 4 * P, tc), BF16),
            jax.ShapeDtypeStruct((G, tc, 4 * P), BF16),
        ],
        scratch_shapes=[pltpu.VMEM((gb, 4 * P, 2 * tc), F32)],
        compiler_params=pltpu.CompilerParams(dimension_semantics=("parallel",)),
        name="ssm_tables",
    )(pcol, prow, bcat, cre4, cim4, dtile)


def _ssm_state_kernel(u_ref, gm_ref, o_ref, *, gb, ns):
    nc = u_ref.shape[-1]
    for gi in range(gb):
        u = u_ref[gi].reshape(-1, nc)
        o_ref[:, gi * ns:(gi + 1) * ns] = _dot(gm_ref[gi], u).T


def _ssm_state(ut, gm):
    G, T, C, nc = ut.shape
    ns = gm.shape[1]
    gb = SSM_GROUPS_PER_STEP
    return pl.pallas_call(
        functools.partial(_ssm_state_kernel, gb=gb, ns=ns),
        grid=(G // gb,),
        in_specs=[pl.BlockSpec((gb, T, C, nc), lambda s: (s, 0, 0, 0)),
                  pl.BlockSpec((gb, ns, T * C), lambda s: (s, 0, 0))],
        out_specs=pl.BlockSpec((nc, gb * ns), lambda s: (0, s)),
        out_shape=jax.ShapeDtypeStruct((nc, G * ns), F32),
        compiler_params=pltpu.CompilerParams(dimension_semantics=("parallel",)),
        name="ssm_state",
    )(ut, gm)


def _ssm_scan_kernel(are_ref, aim_ref, ldt_ref, s_ref, o_ref, x_ref, xs_ref, *, T, n0, seq_starts, seq_ends):
    d = pl.program_id(0)
    j = pl.program_id(1)
    nblk = pl.num_programs(1)
    blk = j + d * (nblk - 1 - 2 * j)
    P = SSM_STATE

    is_start = functools.reduce(jnp.logical_or, [blk == s for s in seq_starts])
    is_end = functools.reduce(jnp.logical_or, [blk == e for e in seq_ends])
    reset = jnp.where(d == 0, is_start, is_end)

    @pl.when(reset)
    def _():
        x_ref[...] = jnp.zeros_like(x_ref)
        xs_ref[...] = jnp.zeros_like(xs_ref)

    dt = jnp.exp(ldt_ref[...])
    zr = are_ref[...] * dt
    zi = aim_ref[...] * dt
    mag = jnp.exp(float(T) * zr)
    mr = mag * jnp.cos(float(T) * zi)
    mi = mag * jnp.sin(float(T) * zi)
    lane = lax.broadcasted_iota(jnp.int32, mr.shape, 1)
    m2 = jnp.where(lane < P, -mi, mi)
    m2s = -m2

    def body(k, carry):
        x, xs = carry
        row = k + d * (n0 - 1 - 2 * k)
        loc = s_ref[row]
        o_ref[row] = x.astype(o_ref.dtype)
        locs = pltpu.roll(loc, P, 1)
        return x * mr + xs * m2 + loc, xs * mr + x * m2s + locs

    x, xs = lax.fori_loop(0, n0, body, (x_ref[...], xs_ref[...]), unroll=4)
    x_ref[...] = x
    xs_ref[...] = xs


def _ssm_scan(s_loc, a_re, a_im, log_dt, *, T, n_unit_seqs):
    G, P = SSM_GROUPS, SSM_STATE
    nc = s_loc.shape[0]
    n0 = SEQ_BLOCK // T
    nblk = nc // n0
    seq_starts = tuple(range(n_unit_seqs + 1))
    seq_ends = tuple(range(n_unit_seqs)) + (nblk - 1,)
    dup = lambda x: jnp.concatenate([x, x], axis=-1)
    are2, aim2 = dup(a_re), dup(a_im)
    ldt2 = jnp.broadcast_to(log_dt[:, :, None], (2, G, 2 * P))

    def blk_map(d, j):
        return (j + d * (nblk - 1 - 2 * j), 0, d)

    par = pl.BlockSpec((None, G, 2 * P), lambda d, j: (d, 0, 0))
    return pl.pallas_call(
        functools.partial(_ssm_scan_kernel, T=T, n0=n0, seq_starts=seq_starts, seq_ends=seq_ends),
        grid=(2, nblk),
        in_specs=[par, par, par, pl.BlockSpec((n0, G, 2 * P), blk_map)],
        out_specs=pl.BlockSpec((n0, G, 2 * P), blk_map),
        out_shape=jax.ShapeDtypeStruct((nc, G, 4 * P), BF16),
        scratch_shapes=[pltpu.VMEM((G, 2 * P), F32), pltpu.VMEM((G, 2 * P), F32)],
        compiler_params=pltpu.CompilerParams(dimension_semantics=("arbitrary", "arbitrary")),
        name="ssm_scan",
    )(are2, aim2, ldt2, s_loc)


def _ssm_out_kernel(u_ref, mt_ref, s_ref, cs_ref, y_ref, *, gb, ns):
    nc = u_ref.shape[-1]
    for gi in range(gb):
        u = u_ref[gi].reshape(-1, nc)
        y = _dot(mt_ref[gi], u) + _dot_nt(cs_ref[gi], s_ref[:, gi * ns:(gi + 1) * ns])
        y_ref[gi] = _gelu_tanh(y).astype(y_ref.dtype).reshape(y_ref.shape[1:])


def _ssm_out(ut, mt, s_in, cs):
    G, T, C, nc = ut.shape
    ns = cs.shape[2]
    gb = SSM_GROUPS_PER_STEP
    return pl.pallas_call(
        functools.partial(_ssm_out_kernel, gb=gb, ns=ns),
        grid=(G // gb,),
        in_specs=[pl.BlockSpec((gb, T, C, nc), lambda s: (s, 0, 0, 0)),
                  pl.BlockSpec((gb, T * C, T * C), lambda s: (s, 0, 0)),
                  pl.BlockSpec((nc, gb * ns), lambda s: (0, s)),
                  pl.BlockSpec((gb, T * C, ns), lambda s: (s, 0, 0))],
        out_specs=pl.BlockSpec((gb, T, C, nc), lambda s: (s, 0, 0, 0)),
        out_shape=jax.ShapeDtypeStruct((G, T, C, nc), BF16),
        compiler_params=pltpu.CompilerParams(dimension_semantics=("parallel",)),
        name="ssm_out",
    )(ut, mt, s_in, cs)


def _ssm_tok_kernel(y_ref, o_ref, stage_ref):
    T, ncb = y_ref.shape[1], y_ref.shape[-1]
    R = stage_ref.shape[2]
    row = lax.broadcasted_iota(jnp.int32, (R * ncb, R * ncb), 0)
    col = lax.broadcasted_iota(jnp.int32, (R * ncb, R * ncb), 1)
    sel = ((row // R == col % ncb) & (row % R == col // ncb)).astype(BF16)
    for k in range(T // R):
        ycat = jnp.concatenate([y_ref[:, k * R + rr].reshape(-1, ncb) for rr in range(R)], axis=1)
        tok = _dot_nt(sel, ycat)
        stage_ref[:, k] = tok.reshape(ncb, R, tok.shape[-1])
    o_ref[...] = stage_ref[...].reshape(o_ref.shape).astype(o_ref.dtype)


def _ssm_tok(yt):
    G, T, C, nc = yt.shape
    ncb = SSM_TOK_CHUNKS
    est = 2 * G * T * C * ncb * 2 + ncb * T * G * C * 4 + 2 * ncb * T * G * C * 2
    return pl.pallas_call(
        _ssm_tok_kernel,
        grid=(nc // ncb,),
        in_specs=[pl.BlockSpec((G, T, C, ncb), lambda s: (0, 0, 0, s))],
        out_specs=pl.BlockSpec((ncb * T, G * C), lambda s: (s, 0)),
        out_shape=jax.ShapeDtypeStruct((nc * T, G * C), BF16),
        scratch_shapes=[pltpu.VMEM((ncb, T // V7X_SUBLANES, V7X_SUBLANES, G * C), F32)],
        compiler_params=pltpu.CompilerParams(
            dimension_semantics=("parallel",), vmem_limit_bytes=_vmem_limit(est)),
        name="ssm_tok",
    )(yt)


def _s5_mixer_gelu(xb, wt, tables, a_re, a_im, log_dt, *, n_unit_seqs):
    T, G, P = SSM_CHUNK, SSM_GROUPS, SSM_STATE
    nc = xb.shape[0] // T
    mt, gm, cs = tables
    ut = _ssm_in(xb.reshape(nc, T * xb.shape[1]), wt, T=T)
    s_loc = _ssm_state(ut, gm)
    s_in = _ssm_scan(s_loc.reshape(nc, G, 4 * P), a_re, a_im, log_dt, T=T, n_unit_seqs=n_unit_seqs)
    yt = _ssm_out(ut, mt, s_in.reshape(nc, G * 4 * P), cs)
    return _ssm_tok(yt)


def _kv_kernel(m_ref, g_ref, b_ref, w_ref, o_ref, mb_ref):
    @pl.when(pl.program_id(0) == 0)
    def _():
        mb_ref[...] = _layernorm_rows(m_ref[...], g_ref[...], b_ref[...]).astype(BF16)

    o_ref[...] = _dot(mb_ref[...], w_ref[...]).astype(o_ref.dtype)


def _kv_proj(mem, ln_g, ln_b, w_kv):
    n, d = mem.shape
    m = w_kv.shape[1]
    tn = KV_TN
    est = 2 * n * d * 4 + n * d * 2 + 2 * d * tn * 2 + 2 * n * tn * 2 + 3 * n * d * 4
    return pl.pallas_call(
        _kv_kernel,
        grid=(m // tn,),
        in_specs=[pl.BlockSpec((n, d), lambda j: (0, 0)),
                  pl.BlockSpec((1, d), lambda j: (0, 0)),
                  pl.BlockSpec((1, d), lambda j: (0, 0)),
                  pl.BlockSpec((d, tn), lambda j: (0, j))],
        out_specs=pl.BlockSpec((n, tn), lambda j: (0, j)),
        out_shape=jax.ShapeDtypeStruct((n, m), BF16),
        scratch_shapes=[pltpu.VMEM((n, d), BF16)],
        compiler_params=pltpu.CompilerParams(
            dimension_semantics=("arbitrary",), vmem_limit_bytes=_vmem_limit(est)),
        name="kv_proj",
    )(mem, ln_g, ln_b, w_kv)


def _mixer_kernel(ya_ref, ub_ref, ob_ref, g0_ref, g1_ref, g2_ref,
                  wa_ref, wb_ref, wgp_ref, wap_ref, o_ref):
    ya = ya_ref[...]
    y_a = _dot(ya, wa_ref[...]) * _sigmoid(_dot(ya, wb_ref[...]))
    y_b = _dot(ub_ref[...], wgp_ref[...])
    y_c = _dot(ob_ref[...], wap_ref[...])
    merged = (_sigmoid(g0_ref[...].astype(F32)) * y_a
              + _sigmoid(g1_ref[...].astype(F32)) * y_b
              + _sigmoid(g2_ref[...].astype(F32)) * y_c)
    o_ref[...] = merged.astype(o_ref.dtype)


def _mixer(ya, ub, ob, proj, w_glu, w_gproj, w_aproj):
    n, wid = ya.shape
    d = D_MODEL
    tm, tn = MIX_TM, MIX_TN
    nn = d // tn
    gate0 = (proj.shape[1] - 3 * d) // tn
    gate_spec = lambda br: pl.BlockSpec((tm, tn), lambda i, j: (i, gate0 + br * nn + j))
    act_spec = pl.BlockSpec((tm, wid), lambda i, j: (i, 0))
    est = (2 * 3 * tm * wid * 2 + 2 * 3 * tm * tn * 2 + 2 * 4 * wid * tn * 2 + 2 * tm * tn * 2
           + 8 * tm * tn * 4)
    return pl.pallas_call(
        _mixer_kernel,
        grid=(n // tm, nn),
        in_specs=[
            act_spec, act_spec, act_spec,
            gate_spec(0), gate_spec(1), gate_spec(2),
            pl.BlockSpec((wid, tn), lambda i, j: (0, j)),
            pl.BlockSpec((wid, tn), lambda i, j: (0, nn + j)),
            pl.BlockSpec((wid, tn), lambda i, j: (0, j)),
            pl.BlockSpec((wid, tn), lambda i, j: (0, j)),
        ],
        out_specs=pl.BlockSpec((tm, tn), lambda i, j: (i, j)),
        out_shape=jax.ShapeDtypeStruct((n, d), BF16),
        compiler_params=pltpu.CompilerParams(
            dimension_semantics=("parallel", "parallel"),
            vmem_limit_bytes=_vmem_limit(est)),
        name="mixer",
    )(ya, ub, ob, proj, proj, proj, w_glu, w_glu, w_gproj, w_aproj)


def _outproj_ln_kernel(m_ref, w_ref, x_ref, g_ref, b_ref, o_ref):
    y = ALPHA * x_ref[...] + _dot(m_ref[...], w_ref[...])
    o_ref[...] = _layernorm_rows(y, g_ref[...], b_ref[...])


def _outproj_ln(merged, w_out, x, ln_g, ln_b):
    n, d = x.shape
    tm = OUT_TM
    est = 2 * tm * d * 2 + 2 * d * d * 2 + 4 * tm * d * 4 + 2 * tm * d * 4
    return pl.pallas_call(
        _outproj_ln_kernel,
        grid=(n // tm,),
        in_specs=[pl.BlockSpec((tm, d), lambda i: (i, 0)),
                  pl.BlockSpec((d, d), lambda i: (0, 0)),
                  pl.BlockSpec((tm, d), lambda i: (i, 0)),
                  pl.BlockSpec((1, d), lambda i: (0, 0)),
                  pl.BlockSpec((1, d), lambda i: (0, 0))],
        out_specs=pl.BlockSpec((tm, d), lambda i: (i, 0)),
        out_shape=jax.ShapeDtypeStruct((n, d), F32),
        compiler_params=pltpu.CompilerParams(
            dimension_semantics=("parallel",), vmem_limit_bytes=_vmem_limit(est)),
        name="outproj_ln",
    )(merged, w_out, x, ln_g, ln_b)


def _layer(groups, ffn1_w_gu, ffn1_w_down, ln1_g, ln1_b, w_in,
           ssm_a_re, ssm_a_im, ssm_log_dt, ssm_b_re, ssm_b_im, ssm_c_re, ssm_c_im, ssm_d, ssm_w_glu,
           gmlp_ln_g, gmlp_ln_b, gmlp_w_s, gmlp_b_s, gmlp_w_proj,
           mem_ln_g, mem_ln_b, attn_w_kv, attn_w_proj,
           w_out, ln2_g, ln2_b, ffn2_w_gu, ffn2_w_down, ln3_g, ln3_b):
    row = lambda v: v.reshape(1, -1)
    ffn1 = _ffn_weight_set(ffn1_w_gu[:, :D_FF].astype(BF16), ffn1_w_gu[:, D_FF:].astype(BF16),
                           ffn1_w_down.astype(BF16))
    w_in_b = w_in.astype(BF16)
    wt_ssm = _ssm_weight_t(w_in)
    tables = _ssm_tables(ssm_a_re, ssm_a_im, ssm_log_dt, ssm_b_re, ssm_b_im, ssm_c_re, ssm_c_im,
                         ssm_d, T=SSM_CHUNK)
    w_kv, w_s = attn_w_kv.astype(BF16), gmlp_w_s.astype(BF16)

    whole = lambda a: ((0, a.shape[1]),)
    jobs = [("ffn2_gu", ffn2_w_gu, 32, ((0, D_FF), (D_FF, 2 * D_FF))),
            ("ffn2_down", ffn2_w_down, GMLP_CHUNK, whole(ffn2_w_down)),
            ("glu", ssm_w_glu, 16, whole(ssm_w_glu)),
            ("out", w_out, 32, whole(w_out)),
            ("gproj", gmlp_w_proj, 16, whole(gmlp_w_proj)),
            ("aproj", attn_w_proj, 16, whole(attn_w_proj))]
    bf16_w = {}
    stage1 = []
    for gi, (x, mem, n_unit_seqs) in enumerate(groups):
        mine = jobs[gi::len(groups)]
        x1 = _ffn_ln(x, ffn1, row(ln1_g), row(ln1_b))
        proj, x1b, conv = _in_proj(x1, w_in_b, row(gmlp_ln_g), row(gmlp_ln_b),
                                   [job[1:] for job in mine])
        conv = iter(conv)
        for name, _, _, splits in mine:
            bf16_w[name] = [next(conv) for _ in splits]
        stage1.append((x1, proj, x1b))
    ffn2 = _ffn_weight_set(bf16_w["ffn2_gu"][0], bf16_w["ffn2_gu"][1], bf16_w["ffn2_down"][0])
    w_glu, w_out_b = bf16_w["glu"][0], bf16_w["out"][0]
    w_gproj, w_aproj = bf16_w["gproj"][0], bf16_w["aproj"][0]

    outs = []
    for (x, mem, n_unit_seqs), (x1, proj, x1b) in zip(groups, stage1):
        kv = _kv_proj(mem, row(mem_ln_g), row(mem_ln_b), w_kv)
        ub, ob = _branches(proj, kv, w_s, gmlp_b_s)
        ya = _s5_mixer_gelu(x1b, wt_ssm, tables, ssm_a_re, ssm_a_im, ssm_log_dt,
                            n_unit_seqs=n_unit_seqs)
        merged = _mixer(ya, ub, ob, proj, w_glu, w_gproj, w_aproj)
        x2 = _outproj_ln(merged, w_out_b, x1, row(ln2_g), row(ln2_b))
        outs.append(_ffn_ln(x2, ffn2, row(ln3_g), row(ln3_b)))
    return outs


def kernel(x_prompt, x_sample, mem_prompt, mem_sample, ffn1_w_gu, ffn1_w_down, ln1_g, ln1_b, w_in,
           ssm_a_re, ssm_a_im, ssm_log_dt, ssm_b_re, ssm_b_im, ssm_c_re, ssm_c_im, ssm_d, ssm_w_glu,
           gmlp_ln_g, gmlp_ln_b, gmlp_w_s, gmlp_b_s, gmlp_w_proj, mem_ln_g, mem_ln_b, attn_w_kv,
           attn_w_proj, w_out, ln2_g, ln2_b, ffn2_w_gu, ffn2_w_down, ln3_g, ln3_b):
    d = x_prompt.shape[-1]
    assert x_prompt.shape[1] == SEQ_BLOCK and x_sample.shape[0] == 1
    assert x_sample.shape[1] % SEQ_BLOCK == 0
    xs = [x_prompt.reshape(-1, d), x_sample.reshape(-1, d)]
    mems = [mem_prompt.reshape(-1, d), mem_sample.reshape(-1, d)]
    n_unit = [x_prompt.shape[0], 0]
    for l in range(DEPTH):
        xs = _layer(list(zip(xs, mems, n_unit)),
                    ffn1_w_gu[l], ffn1_w_down[l], ln1_g[l], ln1_b[l], w_in[l],
                    ssm_a_re[l], ssm_a_im[l], ssm_log_dt[l], ssm_b_re[l], ssm_b_im[l],
                    ssm_c_re[l], ssm_c_im[l], ssm_d[l], ssm_w_glu[l],
                    gmlp_ln_g[l], gmlp_ln_b[l], gmlp_w_s[l], gmlp_b_s[l], gmlp_w_proj[l],
                    mem_ln_g[l], mem_ln_b[l], attn_w_kv[l], attn_w_proj[l],
                    w_out[l], ln2_g[l], ln2_b[l], ffn2_w_gu[l], ffn2_w_down[l], ln3_g[l], ln3_b[l])
    return (xs[0].reshape(x_prompt.shape), xs[1].reshape(x_sample.shape))
```

```python
import functools
import math

import jax
import jax.numpy as jnp
from jax import lax
from jax.experimental import pallas as pl
from jax.experimental.pallas import tpu as pltpu

F32 = jnp.float32
BF16 = jnp.bfloat16

D_MODEL = 2048
DEPTH = 1
SEQ_BLOCK = 4096
N_MEM = 256
SSM_WIDTH = D_MODEL // 2
SSM_GROUP = 16
SSM_GROUPS = SSM_WIDTH // SSM_GROUP
SSM_STATE = 64
GMLP_WIDTH = D_MODEL // 2
GMLP_CHUNK = 128
GMLP_HEADS = 8
GMLP_HEAD_DIM = GMLP_WIDTH // GMLP_HEADS
ATTN_HEADS = 4
ATTN_HEAD_DIM = D_MODEL // 8
ATTN_WIDTH = ATTN_HEADS * ATTN_HEAD_DIM
D_FF = 5504
ALPHA = (2.0 * DEPTH) ** 0.25
LN_EPS = 1e-5

V7X_LANES = 128
V7X_SUBLANES = 8
V7X_VMEM_BYTES = 64 * 1024 * 1024

SSM_CHUNK = 32
SSM_GROUPS_PER_STEP = 8
SSM_TABLE_GROUPS = 1
SSM_IN_POSITIONS = 4
SSM_TOK_CHUNKS = 128
FFN_TM = 512
FFN_TF = 1024
PROJ_TM = 1024
PROJ_TN = 1024
BRANCH_TM = 512
MIX_TM = 512
MIX_TN = 1024
OUT_TM = 512
KV_TN = 512


def _vmem_limit(nbytes):
    return int(min(nbytes + (16 << 20), V7X_VMEM_BYTES - (4 << 20)))


def _layernorm_rows(y, g, b):
    mu = jnp.mean(y, axis=-1, keepdims=True)
    yc = y - mu
    var = jnp.mean(yc * yc, axis=-1, keepdims=True)
    return yc * lax.rsqrt(var + LN_EPS) * g + b


def _gelu_tanh(x):
    c = math.sqrt(2.0 / math.pi)
    return 0.5 * x * (1.0 + jnp.tanh(c * (x + 0.044715 * (x * x * x))))


def _sigmoid(x):
    return 1.0 / (1.0 + jnp.exp(-x))


def _dot(a, b):
    return jnp.dot(a, b, preferred_element_type=F32)


def _dot_nt(a, b):
    return lax.dot_general(a, b, (((1,), (1,)), ((), ())), preferred_element_type=F32)


def _swiglu_down(xb, wg, wu, wd):
    gate = _dot(xb, wg)
    up = _dot(xb, wu)
    act = (gate * _sigmoid(gate) * up).astype(BF16)
    return _dot(act, wd)


def _ffn_ln_kernel(x_ref, wg_ref, wu_ref, wd_ref, wgt_ref, wut_ref, wdt_ref, g_ref, b_ref,
                   o_ref, xb_ref):
    j = pl.program_id(1)

    @pl.when(j == 0)
    def _():
        xb_ref[...] = x_ref[...].astype(BF16)
        o_ref[...] = jnp.zeros_like(o_ref)

    o_ref[...] += _swiglu_down(xb_ref[...], wg_ref[...], wu_ref[...], wd_ref[...])

    @pl.when(j == pl.num_programs(1) - 1)
    def _():
        acc = o_ref[...] + _swiglu_down(xb_ref[...], wgt_ref[...], wut_ref[...], wdt_ref[...])
        y = ALPHA * x_ref[...] + 0.5 * acc
        o_ref[...] = _layernorm_rows(y, g_ref[...], b_ref[...])


def _ffn_ln(x, weights, ln_g, ln_b):
    wg, wu, wd, wgt, wut, wdt = weights
    n, d = x.shape
    tm, tf = FFN_TM, FFN_TF
    nf = wd.shape[0] // tf
    ft = wdt.shape[0]
    once = dict(pipeline_mode=pl.Buffered(1))
    est = (2 * tm * d * 4 + 2 * tm * d * 4 + tm * d * 2
           + 2 * 3 * (d * tf * 2) + 3 * (d * ft * 2) + 3 * tm * tf * 4)
    return pl.pallas_call(
        _ffn_ln_kernel,
        grid=(n // tm, nf),
        in_specs=[
            pl.BlockSpec((tm, d), lambda i, j: (i, 0)),
            pl.BlockSpec((d, tf), lambda i, j: (0, j)),
            pl.BlockSpec((d, tf), lambda i, j: (0, j)),
            pl.BlockSpec((tf, d), lambda i, j: (j, 0)),
            pl.BlockSpec((d, ft), lambda i, j: (0, 0), **once),
            pl.BlockSpec((d, ft), lambda i, j: (0, 0), **once),
            pl.BlockSpec((ft, d), lambda i, j: (0, 0), **once),
            pl.BlockSpec((1, d), lambda i, j: (0, 0)),
            pl.BlockSpec((1, d), lambda i, j: (0, 0)),
        ],
        out_specs=pl.BlockSpec((tm, d), lambda i, j: (i, 0)),
        out_shape=jax.ShapeDtypeStruct((n, d), F32),
        scratch_shapes=[pltpu.VMEM((tm, d), BF16)],
        compiler_params=pltpu.CompilerParams(
            dimension_semantics=("parallel", "arbitrary"),
            vmem_limit_bytes=_vmem_limit(est)),
        name="ffn_ln",
    )(x, wg, wu, wd, wgt, wut, wdt, ln_g, ln_b)


def _ffn_weight_set(wg, wu, wd):
    full = (D_FF // FFN_TF) * FFN_TF
    return wg, wu, wd, wg[:, full:], wu[:, full:], wd[full:]


def _cast_job_specs(cast_jobs, steps, step_index):
    src_specs, dst_specs, dst_shapes, casts, nbytes = [], [], [], [], 0
    for src, rows, splits in cast_jobs:
        nblk = src.shape[0] // rows
        assert src.shape[0] % rows == 0 and nblk <= steps
        blk_map = lambda *idx, nblk=nblk: (jnp.minimum(step_index(*idx), nblk - 1), 0)
        src_specs.append(pl.BlockSpec((rows, src.shape[1]), blk_map))
        for lo, hi in splits:
            dst_specs.append(pl.BlockSpec((rows, hi - lo), blk_map))
            dst_shapes.append(jax.ShapeDtypeStruct((src.shape[0], hi - lo), BF16))
        casts.append(tuple(splits))
        nbytes += 2 * rows * src.shape[1] * 6
    return src_specs, dst_specs, dst_shapes, tuple(casts), nbytes


def _cast_rows(src_refs, dst_refs, casts):
    dsts = iter(dst_refs)
    for src, splits in zip(src_refs, casts):
        for lo, hi in splits:
            next(dsts)[...] = src[:, lo:hi].astype(BF16)


def _in_proj_kernel(x_ref, w_ref, lng_ref, lnb_ref, *refs, casts):
    n_src = len(casts)
    src_refs = refs[:n_src]
    o_ref, xb_ref = refs[n_src:n_src + 2]
    dst_refs = refs[n_src + 2:]
    j = pl.program_id(1)

    @pl.when(j == 0)
    def _():
        xb_ref[...] = x_ref[...].astype(BF16)

    def convert_weights():
        _cast_rows(src_refs, dst_refs, casts)

    @pl.when(j == 0)
    def _():
        convert_weights()
        o_ref[...] = _gelu_tanh(_dot(xb_ref[...], w_ref[...])).astype(o_ref.dtype)

    @pl.when(j == 1)
    def _():
        convert_weights()
        v = _gelu_tanh(_dot(xb_ref[...], w_ref[...]))
        o_ref[...] = _layernorm_rows(v, lng_ref[...], lnb_ref[...]).astype(o_ref.dtype)

    @pl.when(j == 2)
    def _():
        convert_weights()
        o_ref[...] = _dot(xb_ref[...], w_ref[...]).astype(o_ref.dtype)

    @pl.when(j > 2)
    def _():
        convert_weights()
        o_ref[...] = _sigmoid(_dot(xb_ref[...], w_ref[...])).astype(o_ref.dtype)


def _in_proj(x, w, gln_g, gln_b, cast_jobs):
    n, k = x.shape
    tm, tn = PROJ_TM, PROJ_TN
    assert tn == GMLP_WIDTH
    skip = SSM_WIDTH // tn
    m = w.shape[1] - SSM_WIDTH
    ncol = m // tn
    est = 2 * tm * k * 4 + 2 * tm * k * 2 + 2 * k * tn * 2 + 2 * tm * tn * 2 + 3 * tm * tn * 4
    src_specs, dst_specs, dst_shapes, casts, cast_bytes = _cast_job_specs(
        cast_jobs, (n // tm) * ncol, lambda i, j: i * ncol + j)
    est += cast_bytes
    res = pl.pallas_call(
        functools.partial(_in_proj_kernel, casts=casts),
        grid=(n // tm, ncol),
        in_specs=[pl.BlockSpec((tm, k), lambda i, j: (i, 0)),
                  pl.BlockSpec((k, tn), lambda i, j: (0, skip + j)),
                  pl.BlockSpec((1, tn), lambda i, j: (0, 0)),
                  pl.BlockSpec((1, tn), lambda i, j: (0, 0))] + src_specs,
        out_specs=[pl.BlockSpec((tm, tn), lambda i, j: (i, j)),
                   pl.BlockSpec((tm, k), lambda i, j: (i, 0))] + dst_specs,
        out_shape=[jax.ShapeDtypeStruct((n, m), BF16),
                   jax.ShapeDtypeStruct((n, k), BF16)] + dst_shapes,
        compiler_params=pltpu.CompilerParams(
            dimension_semantics=("arbitrary", "arbitrary"),
            vmem_limit_bytes=_vmem_limit(est)),
        name="in_proj",
    )(x, w, gln_g, gln_b, *[job[0] for job in cast_jobs])
    return res[0], res[1], res[2:]


def _branches_kernel(u_ref, v_ref, q_ref, kv_ref, ws_ref, bs_ref, ub_ref, ob_ref, *, tm):
    hd = GMLP_HEAD_DIM

    def chunk_body(ci, carry):
        rows = pl.ds(pl.multiple_of(ci * GMLP_CHUNK, GMLP_CHUNK), GMLP_CHUNK)
        for h in range(GMLP_HEADS):
            cols = slice(h * hd, (h + 1) * hd)
            mixed = _dot(ws_ref[h], v_ref[rows, cols]) + bs_ref[:, h:h + 1]
            ub_ref[rows, cols] = (u_ref[rows, cols].astype(F32) * mixed).astype(BF16)
        return carry

    lax.fori_loop(0, tm // GMLP_CHUNK, chunk_body, 0)

    ad = ATTN_HEAD_DIM
    scale = ad ** -0.5
    for h in range(ATTN_HEADS):
        cols = slice(h * ad, (h + 1) * ad)
        vcols = slice(ATTN_WIDTH + h * ad, ATTN_WIDTH + (h + 1) * ad)
        s = _dot_nt(q_ref[:, cols], kv_ref[:, cols]) * scale
        p = jnp.exp(s - jnp.max(s, axis=-1, keepdims=True))
        p = p * (1.0 / jnp.sum(p, axis=-1, keepdims=True))
        ob_ref[:, cols] = _dot(p.astype(BF16), kv_ref[:, vcols]).astype(BF16)


def _branches(proj, kv, w_s, b_s):
    n = proj.shape[0]
    tm = BRANCH_TM
    wid = GMLP_WIDTH
    assert wid == ATTN_WIDTH
    blocks_per_seq = SEQ_BLOCK // tm
    n_mem_batches = kv.shape[0] // N_MEM
    col = lambda c: pl.BlockSpec((tm, wid), lambda i: (i, c))
    full2 = lambda a: pl.BlockSpec(a.shape, lambda i: (0, 0))
    out = pl.BlockSpec((tm, wid), lambda i: (i, 0))
    return pl.pallas_call(
        functools.partial(_branches_kernel, tm=tm),
        grid=(n // tm,),
        in_specs=[col(0), col(1), col(2),
                  pl.BlockSpec((N_MEM, 2 * ATTN_WIDTH),
                               lambda i: (jnp.minimum(i // blocks_per_seq, n_mem_batches - 1), 0)),
                  pl.BlockSpec(w_s.shape, lambda i: (0, 0, 0)),
                  full2(b_s)],
        out_specs=[out, out],
        out_shape=[jax.ShapeDtypeStruct((n, wid), BF16), jax.ShapeDtypeStruct((n, wid), BF16)],
        compiler_params=pltpu.CompilerParams(dimension_semantics=("parallel",)),
        name="branches",
    )(proj, proj, proj, kv, w_s, b_s)


def _ssm_weight_t_kernel(w_ref, o_ref):
    o_ref[...] = w_ref[...].T.astype(o_ref.dtype)


def _ssm_weight_t(w_in):
    d = w_in.shape[0]
    tn = 2 * V7X_LANES
    return pl.pallas_call(
        _ssm_weight_t_kernel,
        grid=(SSM_WIDTH // tn,),
        in_specs=[pl.BlockSpec((d, tn), lambda j: (0, j))],
        out_specs=pl.BlockSpec((tn, d), lambda j: (j, 0)),
        out_shape=jax.ShapeDtypeStruct((SSM_WIDTH, d), BF16),
        compiler_params=pltpu.CompilerParams(dimension_semantics=("parallel",)),
        name="ssm_weight_t",
    )(w_in)


def _ssm_in_kernel(x_ref, wt_ref, o_ref, *, rb):
    d = wt_ref.shape[1]
    for k in range(rb):
        ut = _dot_nt(wt_ref[...], x_ref[:, k * d:(k + 1) * d])
        o_ref[:, k] = ut.astype(BF16).reshape(o_ref.shape[0], o_ref.shape[2], o_ref.shape[3])


def _ssm_in(xc, wt, *, T):
    nc = xc.shape[0]
    d = xc.shape[1] // T
    G, C = SSM_GROUPS, SSM_GROUP
    rb = SSM_IN_POSITIONS
    est = 2 * nc * rb * d * 2 + 2 * G * C * d * 2 + 2 * rb * G * C * nc * 2 + 2 * G * C * nc * 4
    return pl.pallas_call(
        functools.partial(_ssm_in_kernel, rb=rb),
        grid=(T // rb,),
        in_specs=[pl.BlockSpec((nc, rb * d), lambda r: (0, r)),
                  pl.BlockSpec((G * C, d), lambda r: (0, 0))],
        out_specs=pl.BlockSpec((G, rb, C, nc), lambda r: (0, r, 0, 0)),
        out_shape=jax.ShapeDtypeStruct((G, T, C, nc), BF16),
        compiler_params=pltpu.CompilerParams(
            dimension_semantics=("parallel",), vmem_limit_bytes=_vmem_limit(est)),
        name="ssm_in",
    )(xc, wt)


def _cmul(x, y):
    return x[0] * y[0] - x[1] * y[1], x[0] * y[1] + x[1] * y[0]


def _split_hi_lo(x):
    hi = x.astype(BF16)
    lo = (x - hi.astype(F32)).astype(BF16)
    return hi, lo


def _dot_hi_lo(a, b):
    ah, al = _split_hi_lo(a)
    bh, bl = _split_hi_lo(b)
    return _dot(ah, bh) + _dot(al, bh) + _dot(ah, bl)


N_TABLE_INPUTS = 6
N_TABLE_OUTPUTS = 3


def _ssm_tables_kernel(*refs, T, gb, casts):
    n_src = len(casts)
    n_dst = sum(len(splits) for splits in casts)
    ins = refs[:N_TABLE_INPUTS]
    srcs = refs[N_TABLE_INPUTS:N_TABLE_INPUTS + n_src]
    outs = refs[N_TABLE_INPUTS + n_src:N_TABLE_INPUTS + n_src + N_TABLE_OUTPUTS]
    dsts = refs[N_TABLE_INPUTS + n_src + N_TABLE_OUTPUTS:][:n_dst]
    q_ref = refs[-1]
    _cast_rows(srcs, dsts, casts)
    for gi in range(gb):
        _ssm_tables_group(*[r.at[gi] for r in ins + outs + (q_ref,)], T=T)


def _ssm_tables_group(pcol_ref, prow_ref, bcat_ref, cre_ref, cim_ref, dtile_ref,
                      mt_ref, gm_ref, cs_ref, q_ref, *, T):
    P, C = SSM_STATE, SSM_GROUP
    rpt = V7X_LANES // C
    nt = T // rpt
    wide = 2 * T * C

    lane = lax.broadcasted_iota(jnp.int32, (1, V7X_LANES), 1)
    rr = lax.shift_right_logical(lane, 4)
    expand = (lax.broadcasted_iota(jnp.int32, (C, V7X_LANES), 0)
              == (lax.broadcasted_iota(jnp.int32, (C, V7X_LANES), 1) & (C - 1))).astype(F32)

    pc = pcol_ref[...]

    lane4 = lax.broadcasted_iota(jnp.int32, (1, 4 * P), 1)
    is_re = (lax.shift_right_logical(lane4, 6) & 1) == 0
    is_f = lane4 < 2 * P
    pr = prow_ref[...]
    dt4 = jnp.exp(pr[2:3])
    zr4, zi4 = pr[0:1] * dt4, pr[1:2] * dt4

    kp = ((T + 1 + 7) // 8) * 8
    krow = lax.broadcasted_iota(jnp.int32, (kp, 4 * P), 0).astype(F32)
    mag = jnp.exp(krow * zr4)
    pw_r, pw_i = mag * jnp.cos(krow * zi4), mag * jnp.sin(krow * zi4)

    def states_on_rows(tab, lo):
        slab = tab[:, lo:lo + V7X_LANES]
        padded = jnp.concatenate([slab, jnp.zeros((V7X_LANES - kp, V7X_LANES), F32)], axis=0)
        return padded.T[0:P]

    pt_f = states_on_rows(pw_r, 0), states_on_rows(pw_i, 0)
    pt_b = states_on_rows(pw_r, 2 * P), states_on_rows(pw_i, 2 * P)

    def stair(pt, descending):
        re = jnp.zeros((P, V7X_LANES), F32)
        im = re
        for k in range(rpt):
            m = rpt - 1 - k if descending else k
            re = jnp.where(rr == k, pt[0][:, m:m + 1], re)
            im = jnp.where(rr == k, pt[1][:, m:m + 1], im)
        return re, im

    def zoh(d, pt):
        are = pc[:, d:d + 1]
        aim = pc[:, 2 + d:3 + d]
        nr = pt[0][:, 1:2] - 1.0
        ni = pt[1][:, 1:2]
        den = are * are + aim * aim
        return (nr * are + ni * aim) / den, (ni * are - nr * aim) / den

    def col(pt, m):
        return pt[0][:, m:m + 1], pt[1][:, m:m + 1]

    dsc_f = stair(pt_f, True)
    asc_b = stair(pt_b, False)
    btile = jnp.dot(bcat_ref[...], expand, precision=lax.Precision.HIGHEST,
                    preferred_element_type=F32)
    bt = (btile[0:P], btile[P:2 * P])
    bb_f = _cmul(zoh(0, pt_f), bt)
    bd_f = _cmul(dsc_f, bb_f)
    ba_b = _cmul(asc_b, _cmul(zoh(1, pt_b), bt))

    for j in range(nt):
        cols = slice(j * V7X_LANES, (j + 1) * V7X_LANES)
        xf = _cmul(bd_f, col(pt_f, T - rpt - rpt * j))
        xb = _cmul(ba_b, col(pt_b, rpt * j))
        gm_ref[0:P, cols] = xf[0].astype(BF16)
        gm_ref[P:2 * P, cols] = xf[1].astype(BF16)
        gm_ref[2 * P:3 * P, cols] = xb[0].astype(BF16)
        gm_ref[3 * P:4 * P, cols] = xb[1].astype(BF16)

    zeros = jnp.zeros((2 * P, V7X_LANES), F32)
    for j in range(2 * nt):
        cols = slice(j * V7X_LANES, (j + 1) * V7X_LANES)
        if j < nt:
            qf = _cmul(bd_f, col(pt_f, T - rpt * j - (rpt - 1)))
            q_ref[0:P, cols] = qf[0]
            q_ref[P:2 * P, cols] = qf[1]
            q_ref[2 * P:4 * P, cols] = zeros
        else:
            qb = _cmul(ba_b, col(pt_b, rpt * j - T))
            q_ref[2 * P:3 * P, cols] = qb[0]
            q_ref[3 * P:4 * P, cols] = qb[1]
            if j == nt:
                center = rr == 0
                q_ref[0:P, cols] = jnp.where(center, bb_f[0], 0.0)
                q_ref[P:2 * P, cols] = jnp.where(center, bb_f[1], 0.0)
            else:
                q_ref[0:2 * P, cols] = zeros

    cre = cre_ref[...]
    cim = cim_ref[...]

    for r in range(T):
        prr = jnp.where(is_f, pw_r[r + 1:r + 2], pw_r[T - r:T - r + 1])
        pii = jnp.where(is_f, pw_i[r + 1:r + 2], pw_i[T - r:T - r + 1])
        blk = jnp.where(is_re, cre * prr - cim * pii, -(cre * pii + cim * prr))
        cs_ref[r * C:(r + 1) * C, :] = blk.astype(BF16)

    lhs = jnp.where(is_re, cre, -cim)
    zt = _dot_hi_lo(lhs, q_ref[...])
    lanew = lax.broadcasted_iota(jnp.int32, (C, wide), 1)
    roww = lax.broadcasted_iota(jnp.int32, (C, wide), 0)
    diag = (lax.shift_right_logical(lanew, 4) == T) & ((lanew & (C - 1)) == roww)
    zt = zt + jnp.where(diag, dtile_ref[...], 0.0)

    for r in range(T):
        off = (T - r) * C
        shifted = pltpu.roll(zt, (wide - off) % wide, 1)
        mt_ref[r * C:(r + 1) * C, :] = shifted[:, :T * C].astype(BF16)


def _ssm_tables(a_re, a_im, log_dt, b_re, b_im, c_re, c_im, d_skip, cast_jobs, *, T):
    G, P, C = SSM_GROUPS, SSM_STATE, SSM_GROUP
    ldt = jnp.broadcast_to(log_dt[:, :, None], (2, G, P))
    zc = jnp.zeros((G, P), F32)
    pcol = jnp.stack([a_re[0], a_re[1], a_im[0], a_im[1], ldt[0], ldt[1], zc, zc], axis=-1)

    def row4(x):
        return jnp.concatenate([x[0], x[0], x[1], x[1]], axis=-1)

    zr4 = jnp.zeros((G, 4 * P), F32)
    prow = jnp.stack([row4(a_re), row4(a_im), row4(ldt), zr4, zr4, zr4, zr4, zr4], axis=1)
    bcat = jnp.concatenate([b_re, b_im], axis=1)
    cre4 = row4(c_re)
    cim4 = row4(c_im)
    dtile = jnp.tile(d_skip, (1, 2 * T))[:, None, :]

    tc = T * C
    gb = SSM_TABLE_GROUPS
    blk = lambda *shape: pl.BlockSpec((gb,) + shape, lambda g: (g,) + (0,) * len(shape))
    src_specs, dst_specs, dst_shapes, casts, cast_bytes = _cast_job_specs(
        cast_jobs, G // gb, lambda g: g)
    est = 2 * gb * (2 * tc * tc * 2 + 3 * 4 * P * tc * 4) + cast_bytes
    res = pl.pallas_call(
        functools.partial(_ssm_tables_kernel, T=T, gb=gb, casts=casts),
        grid=(G // gb,),
        in_specs=[blk(P, 8), blk(8, 4 * P), blk(2 * P, C), blk(C, 4 * P), blk(C, 4 * P),
                  blk(1, 2 * tc)] + src_specs,
        out_specs=[blk(tc, tc), blk(4 * P, tc), blk(tc, 4 * P)] + dst_specs,
        out_shape=[
            jax.ShapeDtypeStruct((G, tc, tc), BF16),
            jax.ShapeDtypeStruct((G, 4 * P, tc), BF16),
            jax.ShapeDtypeStruct((G, tc, 4 * P), BF16),
        ] + dst_shapes,
        scratch_shapes=[pltpu.VMEM((gb, 4 * P, 2 * tc), F32)],
        compiler_params=pltpu.CompilerParams(
            dimension_semantics=("arbitrary",), vmem_limit_bytes=_vmem_limit(est)),
        name="ssm_tables",
    )(pcol, prow, bcat, cre4, cim4, dtile, *[job[0] for job in cast_jobs])
    return tuple(res[:N_TABLE_OUTPUTS]), res[N_TABLE_OUTPUTS:]


def _ssm_state_kernel(u_ref, gm_ref, o_ref, *, gb, ns):
    nc = u_ref.shape[-1]
    for gi in range(gb):
        u = u_ref[gi].reshape(-1, nc)
        o_ref[:, gi * ns:(gi + 1) * ns] = _dot(gm_ref[gi], u).T


def _ssm_state(ut, gm):
    G, T, C, nc = ut.shape
    ns = gm.shape[1]
    gb = SSM_GROUPS_PER_STEP
    return pl.pallas_call(
        functools.partial(_ssm_state_kernel, gb=gb, ns=ns),
        grid=(G // gb,),
        in_specs=[pl.BlockSpec((gb, T, C, nc), lambda s: (s, 0, 0, 0)),
                  pl.BlockSpec((gb, ns, T * C), lambda s: (s, 0, 0))],
        out_specs=pl.BlockSpec((nc, gb * ns), lambda s: (0, s)),
        out_shape=jax.ShapeDtypeStruct((nc, G * ns), F32),
        compiler_params=pltpu.CompilerParams(dimension_semantics=("parallel",)),
        name="ssm_state",
    )(ut, gm)


def _ssm_scan_kernel(are_ref, aim_ref, ldt_ref, s_ref, o_ref, x_ref, xs_ref, *, T, n0, seq_starts, seq_ends):
    d = pl.program_id(0)
    j = pl.program_id(1)
    nblk = pl.num_programs(1)
    blk = j + d * (nblk - 1 - 2 * j)
    P = SSM_STATE

    is_start = functools.reduce(jnp.logical_or, [blk == s for s in seq_starts])
    is_end = functools.reduce(jnp.logical_or, [blk == e for e in seq_ends])
    reset = jnp.where(d == 0, is_start, is_end)

    @pl.when(reset)
    def _():
        x_ref[...] = jnp.zeros_like(x_ref)
        xs_ref[...] = jnp.zeros_like(xs_ref)

    dt = jnp.exp(ldt_ref[...])
    zr = are_ref[...] * dt
    zi = aim_ref[...] * dt
    mag = jnp.exp(float(T) * zr)
    mr = mag * jnp.cos(float(T) * zi)
    mi = mag * jnp.sin(float(T) * zi)
    lane = lax.broadcasted_iota(jnp.int32, mr.shape, 1)
    m2 = jnp.where(lane < P, -mi, mi)
    m2s = -m2

    def body(k, carry):
        x, xs = carry
        row = k + d * (n0 - 1 - 2 * k)
        loc = s_ref[row]
        o_ref[row] = x.astype(o_ref.dtype)
        locs = pltpu.roll(loc, P, 1)
        return x * mr + xs * m2 + loc, xs * mr + x * m2s + locs

    x, xs = lax.fori_loop(0, n0, body, (x_ref[...], xs_ref[...]), unroll=4)
    x_ref[...] = x
    xs_ref[...] = xs


def _ssm_scan(s_loc, a_re, a_im, log_dt, *, T, n_unit_seqs):
    G, P = SSM_GROUPS, SSM_STATE
    nc = s_loc.shape[0]
    n0 = SEQ_BLOCK // T
    nblk = nc // n0
    seq_starts = tuple(range(n_unit_seqs + 1))
    seq_ends = tuple(range(n_unit_seqs)) + (nblk - 1,)
    dup = lambda x: jnp.concatenate([x, x], axis=-1)
    are2, aim2 = dup(a_re), dup(a_im)
    ldt2 = jnp.broadcast_to(log_dt[:, :, None], (2, G, 2 * P))

    def blk_map(d, j):
        return (j + d * (nblk - 1 - 2 * j), 0, d)

    par = pl.BlockSpec((None, G, 2 * P), lambda d, j: (d, 0, 0))
    return pl.pallas_call(
        functools.partial(_ssm_scan_kernel, T=T, n0=n0, seq_starts=seq_starts, seq_ends=seq_ends),
        grid=(2, nblk),
        in_specs=[par, par, par, pl.BlockSpec((n0, G, 2 * P), blk_map)],
        out_specs=pl.BlockSpec((n0, G, 2 * P), blk_map),
        out_shape=jax.ShapeDtypeStruct((nc, G, 4 * P), BF16),
        scratch_shapes=[pltpu.VMEM((G, 2 * P), F32), pltpu.VMEM((G, 2 * P), F32)],
        compiler_params=pltpu.CompilerParams(dimension_semantics=("arbitrary", "arbitrary")),
        name="ssm_scan",
    )(are2, aim2, ldt2, s_loc)


def _ssm_out_kernel(u_ref, mt_ref, s_ref, cs_ref, y_ref, *, gb, ns):
    nc = u_ref.shape[-1]
    for gi in range(gb):
        u = u_ref[gi].reshape(-1, nc)
        y = _dot(mt_ref[gi], u) + _dot_nt(cs_ref[gi], s_ref[:, gi * ns:(gi + 1) * ns])
        y_ref[gi] = _gelu_tanh(y).astype(y_ref.dtype).reshape(y_ref.shape[1:])


def _ssm_out(ut, mt, s_in, cs):
    G, T, C, nc = ut.shape
    ns = cs.shape[2]
    gb = SSM_GROUPS_PER_STEP
    return pl.pallas_call(
        functools.partial(_ssm_out_kernel, gb=gb, ns=ns),
        grid=(G // gb,),
        in_specs=[pl.BlockSpec((gb, T, C, nc), lambda s: (s, 0, 0, 0)),
                  pl.BlockSpec((gb, T * C, T * C), lambda s: (s, 0, 0)),
                  pl.BlockSpec((nc, gb * ns), lambda s: (0, s)),
                  pl.BlockSpec((gb, T * C, ns), lambda s: (s, 0, 0))],
        out_specs=pl.BlockSpec((gb, T, C, nc), lambda s: (s, 0, 0, 0)),
        out_shape=jax.ShapeDtypeStruct((G, T, C, nc), BF16),
        compiler_params=pltpu.CompilerParams(dimension_semantics=("parallel",)),
        name="ssm_out",
    )(ut, mt, s_in, cs)


def _ssm_tok_kernel(y_ref, o_ref, stage_ref):
    T, ncb = y_ref.shape[1], y_ref.shape[-1]
    R = stage_ref.shape[2]
    row = lax.broadcasted_iota(jnp.int32, (R * ncb, R * ncb), 0)
    col = lax.broadcasted_iota(jnp.int32, (R * ncb, R * ncb), 1)
    sel = ((row // R == col % ncb) & (row % R == col // ncb)).astype(BF16)
    for k in range(T // R):
        ycat = jnp.concatenate([y_ref[:, k * R + rr].reshape(-1, ncb) for rr in range(R)], axis=1)
        tok = _dot_nt(sel, ycat)
        stage_ref[:, k] = tok.reshape(ncb, R, tok.shape[-1])
    o_ref[...] = stage_ref[...].reshape(o_ref.shape).astype(o_ref.dtype)


def _ssm_tok(yt):
    G, T, C, nc = yt.shape
    ncb = SSM_TOK_CHUNKS
    est = 2 * G * T * C * ncb * 2 + ncb * T * G * C * 4 + 2 * ncb * T * G * C * 2
    return pl.pallas_call(
        _ssm_tok_kernel,
        grid=(nc // ncb,),
        in_specs=[pl.BlockSpec((G, T, C, ncb), lambda s: (0, 0, 0, s))],
        out_specs=pl.BlockSpec((ncb * T, G * C), lambda s: (s, 0)),
        out_shape=jax.ShapeDtypeStruct((nc * T, G * C), BF16),
        scratch_shapes=[pltpu.VMEM((ncb, T // V7X_SUBLANES, V7X_SUBLANES, G * C), F32)],
        compiler_params=pltpu.CompilerParams(
            dimension_semantics=("parallel",), vmem_limit_bytes=_vmem_limit(est)),
        name="ssm_tok",
    )(yt)


def _s5_mixer_gelu(xb, wt, tables, a_re, a_im, log_dt, *, n_unit_seqs):
    T, G, P = SSM_CHUNK, SSM_GROUPS, SSM_STATE
    nc = xb.shape[0] // T
    mt, gm, cs = tables
    ut = _ssm_in(xb.reshape(nc, T * xb.shape[1]), wt, T=T)
    s_loc = _ssm_state(ut, gm)
    s_in = _ssm_scan(s_loc.reshape(nc, G, 4 * P), a_re, a_im, log_dt, T=T, n_unit_seqs=n_unit_seqs)
    yt = _ssm_out(ut, mt, s_in.reshape(nc, G * 4 * P), cs)
    return _ssm_tok(yt)


def _kv_kernel(m_ref, g_ref, b_ref, w_ref, o_ref, mb_ref):
    @pl.when(pl.program_id(0) == 0)
    def _():
        mb_ref[...] = _layernorm_rows(m_ref[...], g_ref[...], b_ref[...]).astype(BF16)

    o_ref[...] = _dot(mb_ref[...], w_ref[...]).astype(o_ref.dtype)


def _kv_proj(mem, ln_g, ln_b, w_kv):
    n, d = mem.shape
    m = w_kv.shape[1]
    tn = KV_TN
    est = 2 * n * d * 4 + n * d * 2 + 2 * d * tn * 2 + 2 * n * tn * 2 + 3 * n * d * 4
    return pl.pallas_call(
        _kv_kernel,
        grid=(m // tn,),
        in_specs=[pl.BlockSpec((n, d), lambda j: (0, 0)),
                  pl.BlockSpec((1, d), lambda j: (0, 0)),
                  pl.BlockSpec((1, d), lambda j: (0, 0)),
                  pl.BlockSpec((d, tn), lambda j: (0, j))],
        out_specs=pl.BlockSpec((n, tn), lambda j: (0, j)),
        out_shape=jax.ShapeDtypeStruct((n, m), BF16),
        scratch_shapes=[pltpu.VMEM((n, d), BF16)],
        compiler_params=pltpu.CompilerParams(
            dimension_semantics=("arbitrary",), vmem_limit_bytes=_vmem_limit(est)),
        name="kv_proj",
    )(mem, ln_g, ln_b, w_kv)


def _mixer_kernel(ya_ref, ub_ref, ob_ref, g0_ref, g1_ref, g2_ref,
                  wa_ref, wb_ref, wgp_ref, wap_ref, o_ref):
    ya = ya_ref[...]
    y_a = _dot(ya, wa_ref[...]) * _sigmoid(_dot(ya, wb_ref[...]))
    y_b = _dot(ub_ref[...], wgp_ref[...])
    y_c = _dot(ob_ref[...], wap_ref[...])
    merged = (g0_ref[...].astype(F32) * y_a + g1_ref[...].astype(F32) * y_b
              + g2_ref[...].astype(F32) * y_c)
    o_ref[...] = merged.astype(o_ref.dtype)


def _mixer(ya, ub, ob, proj, w_glu, w_gproj, w_aproj):
    n, wid = ya.shape
    d = D_MODEL
    tm, tn = MIX_TM, MIX_TN
    nn = d // tn
    gate0 = (proj.shape[1] - 3 * d) // tn
    gate_spec = lambda br: pl.BlockSpec((tm, tn), lambda i, j: (i, gate0 + br * nn + j))
    act_spec = pl.BlockSpec((tm, wid), lambda i, j: (i, 0))
    est = (2 * 3 * tm * wid * 2 + 2 * 3 * tm * tn * 2 + 2 * 4 * wid * tn * 2 + 2 * tm * tn * 2
           + 8 * tm * tn * 4)
    return pl.pallas_call(
        _mixer_kernel,
        grid=(n // tm, nn),
        in_specs=[
            act_spec, act_spec, act_spec,
            gate_spec(0), gate_spec(1), gate_spec(2),
            pl.BlockSpec((wid, tn), lambda i, j: (0, j)),
            pl.BlockSpec((wid, tn), lambda i, j: (0, nn + j)),
            pl.BlockSpec((wid, tn), lambda i, j: (0, j)),
            pl.BlockSpec((wid, tn), lambda i, j: (0, j)),
        ],
        out_specs=pl.BlockSpec((tm, tn), lambda i, j: (i, j)),
        out_shape=jax.ShapeDtypeStruct((n, d), BF16),
        compiler_params=pltpu.CompilerParams(
            dimension_semantics=("parallel", "parallel"),
            vmem_limit_bytes=_vmem_limit(est)),
        name="mixer",
    )(ya, ub, ob, proj, proj, proj, w_glu, w_glu, w_gproj, w_aproj)


def _outproj_ln_kernel(m_ref, w_ref, x_ref, g_ref, b_ref, o_ref, y_ref):
    i = pl.program_id(0)
    nblk = pl.num_programs(0) - 1

    @pl.when(i == 0)
    def _():
        y_ref[...] = jnp.zeros_like(y_ref)

    @pl.when(i < nblk)
    def _():
        o_ref[...] = _layernorm_rows(y_ref[...], g_ref[...], b_ref[...])
        y_ref[...] = ALPHA * x_ref[...] + _dot(m_ref[...], w_ref[...])

    @pl.when(i == nblk)
    def _():
        o_ref[...] = _layernorm_rows(y_ref[...], g_ref[...], b_ref[...])


def _outproj_ln(merged, w_out, x, ln_g, ln_b):
    n, d = x.shape
    tm = OUT_TM
    nblk = n // tm
    est = 2 * tm * d * 2 + 2 * d * d * 2 + 4 * tm * d * 4 + 3 * tm * d * 4
    cur = lambda i: (jnp.minimum(i, nblk - 1), 0)
    return pl.pallas_call(
        _outproj_ln_kernel,
        grid=(nblk + 1,),
        in_specs=[pl.BlockSpec((tm, d), cur),
                  pl.BlockSpec((d, d), lambda i: (0, 0)),
                  pl.BlockSpec((tm, d), cur),
                  pl.BlockSpec((1, d), lambda i: (0, 0)),
                  pl.BlockSpec((1, d), lambda i: (0, 0))],
        out_specs=pl.BlockSpec((tm, d), lambda i: (jnp.maximum(i - 1, 0), 0)),
        out_shape=jax.ShapeDtypeStruct((n, d), F32),
        scratch_shapes=[pltpu.VMEM((tm, d), F32)],
        compiler_params=pltpu.CompilerParams(
            dimension_semantics=("arbitrary",), vmem_limit_bytes=_vmem_limit(est)),
        name="outproj_ln",
    )(merged, w_out, x, ln_g, ln_b)


def _layer(groups, ffn1_w_gu, ffn1_w_down, ln1_g, ln1_b, w_in,
           ssm_a_re, ssm_a_im, ssm_log_dt, ssm_b_re, ssm_b_im, ssm_c_re, ssm_c_im, ssm_d, ssm_w_glu,
           gmlp_ln_g, gmlp_ln_b, gmlp_w_s, gmlp_b_s, gmlp_w_proj,
           mem_ln_g, mem_ln_b, attn_w_kv, attn_w_proj,
           w_out, ln2_g, ln2_b, ffn2_w_gu, ffn2_w_down, ln3_g, ln3_b):
    row = lambda v: v.reshape(1, -1)
    whole = lambda a: ((0, a.shape[1]),)
    gate_up = ((0, D_FF), (D_FF, 2 * D_FF))
    tables, early = _ssm_tables(
        ssm_a_re, ssm_a_im, ssm_log_dt, ssm_b_re, ssm_b_im, ssm_c_re, ssm_c_im, ssm_d,
        [(ffn1_w_gu, 32, gate_up), (ffn1_w_down, GMLP_CHUNK, whole(ffn1_w_down)),
         (w_in, 32, whole(w_in))], T=SSM_CHUNK)
    ffn1 = _ffn_weight_set(early[0], early[1], early[2])
    w_in_b = early[3]
    wt_ssm = _ssm_weight_t(w_in)
    w_kv, w_s = attn_w_kv.astype(BF16), gmlp_w_s.astype(BF16)

    jobs = [("ffn2_gu", ffn2_w_gu, 32, gate_up),
            ("ffn2_down", ffn2_w_down, GMLP_CHUNK, whole(ffn2_w_down)),
            ("glu", ssm_w_glu, 16, whole(ssm_w_glu)),
            ("out", w_out, 32, whole(w_out)),
            ("gproj", gmlp_w_proj, 16, whole(gmlp_w_proj)),
            ("aproj", attn_w_proj, 16, whole(attn_w_proj))]
    bf16_w = {}
    stage1 = []
    for gi, (x, mem, n_unit_seqs) in enumerate(groups):
        mine = jobs[gi::len(groups)]
        x1 = _ffn_ln(x, ffn1, row(ln1_g), row(ln1_b))
        proj, x1b, conv = _in_proj(x1, w_in_b, row(gmlp_ln_g), row(gmlp_ln_b),
                                   [job[1:] for job in mine])
        conv = iter(conv)
        for name, _, _, splits in mine:
            bf16_w[name] = [next(conv) for _ in splits]
        stage1.append((x1, proj, x1b))
    ffn2 = _ffn_weight_set(bf16_w["ffn2_gu"][0], bf16_w["ffn2_gu"][1], bf16_w["ffn2_down"][0])
    w_glu, w_out_b = bf16_w["glu"][0], bf16_w["out"][0]
    w_gproj, w_aproj = bf16_w["gproj"][0], bf16_w["aproj"][0]

    outs = []
    for (x, mem, n_unit_seqs), (x1, proj, x1b) in zip(groups, stage1):
        kv = _kv_proj(mem, row(mem_ln_g), row(mem_ln_b), w_kv)
        ub, ob = _branches(proj, kv, w_s, gmlp_b_s)
        ya = _s5_mixer_gelu(x1b, wt_ssm, tables, ssm_a_re, ssm_a_im, ssm_log_dt,
                            n_unit_seqs=n_unit_seqs)
        merged = _mixer(ya, ub, ob, proj, w_glu, w_gproj, w_aproj)
        x2 = _outproj_ln(merged, w_out_b, x1, row(ln2_g), row(ln2_b))
        outs.append(_ffn_ln(x2, ffn2, row(ln3_g), row(ln3_b)))
    return outs


def kernel(x_prompt, x_sample, mem_prompt, mem_sample, ffn1_w_gu, ffn1_w_down, ln1_g, ln1_b, w_in,
           ssm_a_re, ssm_a_im, ssm_log_dt, ssm_b_re, ssm_b_im, ssm_c_re, ssm_c_im, ssm_d, ssm_w_glu,
           gmlp_ln_g, gmlp_ln_b, gmlp_w_s, gmlp_b_s, gmlp_w_proj, mem_ln_g, mem_ln_b, attn_w_kv,
           attn_w_proj, w_out, ln2_g, ln2_b, ffn2_w_gu, ffn2_w_down, ln3_g, ln3_b):
    d = x_prompt.shape[-1]
    assert x_prompt.shape[1] == SEQ_BLOCK and x_sample.shape[0] == 1
    assert x_sample.shape[1] % SEQ_BLOCK == 0
    xs = [x_prompt.reshape(-1, d), x_sample.reshape(-1, d)]
    mems = [mem_prompt.reshape(-1, d), mem_sample.reshape(-1, d)]
    n_unit = [x_prompt.shape[0], 0]
    for l in range(DEPTH):
        xs = _layer(list(zip(xs, mems, n_unit)),
                    ffn1_w_gu[l], ffn1_w_down[l], ln1_g[l], ln1_b[l], w_in[l],
                    ssm_a_re[l], ssm_a_im[l], ssm_log_dt[l], ssm_b_re[l], ssm_b_im[l],
                    ssm_c_re[l], ssm_c_im[l], ssm_d[l], ssm_w_glu[l],
                    gmlp_ln_g[l], gmlp_ln_b[l], gmlp_w_s[l], gmlp_b_s[l], gmlp_w_proj[l],
                    mem_ln_g[l], mem_ln_b[l], attn_w_kv[l], attn_w_proj[l],
                    w_out[l], ln2_g[l], ln2_b[l], ffn2_w_gu[l], ffn2_w_down[l], ln3_g[l], ln3_b[l])
    return (xs[0].reshape(x_prompt.shape), xs[1].reshape(x_sample.shape))
```

```python
import functools
import math

import jax
import jax.numpy as jnp
from jax import lax
from jax.experimental import pallas as pl
from jax.experimental.pallas import tpu as pltpu

F32 = jnp.float32
BF16 = jnp.bfloat16

D_MODEL = 2048
DEPTH = 1
SEQ_BLOCK = 4096
N_MEM = 256
SSM_WIDTH = D_MODEL // 2
SSM_GROUP = 16
SSM_GROUPS = SSM_WIDTH // SSM_GROUP
SSM_STATE = 64
GMLP_WIDTH = D_MODEL // 2
GMLP_CHUNK = 128
GMLP_HEADS = 8
GMLP_HEAD_DIM = GMLP_WIDTH // GMLP_HEADS
ATTN_HEADS = 4
ATTN_HEAD_DIM = D_MODEL // 8
ATTN_WIDTH = ATTN_HEADS * ATTN_HEAD_DIM
D_FF = 5504
ALPHA = (2.0 * DEPTH) ** 0.25
LN_EPS = 1e-5

V7X_LANES = 128
V7X_SUBLANES = 8
V7X_VMEM_BYTES = 64 * 1024 * 1024

SSM_CHUNK = 32
SSM_GROUPS_PER_STEP = 8
SSM_TABLE_GROUPS = 1
SSM_IN_POSITIONS = 4
SSM_TOK_CHUNKS = 128
FFN_TM = 512
FFN_TF = 1024
PROJ_TM = 1024
PROJ_TN = 1024
BRANCH_TM = 512
MIX_TM = 1024
MIX_TN = 512
OUT_TM = 512
KV_TN = 512


def _vmem_limit(nbytes):
    return int(min(nbytes + (16 << 20), V7X_VMEM_BYTES - (4 << 20)))


def _layernorm_rows(y, g, b):
    mu = jnp.mean(y, axis=-1, keepdims=True)
    yc = y - mu
    var = jnp.mean(yc * yc, axis=-1, keepdims=True)
    return yc * lax.rsqrt(var + LN_EPS) * g + b


def _gelu_tanh(x):
    c = math.sqrt(2.0 / math.pi)
    return 0.5 * x * (1.0 + jnp.tanh(c * (x + 0.044715 * (x * x * x))))


def _sigmoid(x):
    return 1.0 / (1.0 + jnp.exp(-x))


def _dot(a, b):
    return jnp.dot(a, b, preferred_element_type=F32)


def _dot_nt(a, b):
    return lax.dot_general(a, b, (((1,), (1,)), ((), ())), preferred_element_type=F32)


def _swiglu_down(xb, wg, wu, wd):
    gate = _dot(xb, wg)
    up = _dot(xb, wu)
    act = (gate * _sigmoid(gate) * up).astype(BF16)
    return _dot(act, wd)


def _ffn_ln_kernel(x_ref, wg_ref, wu_ref, wd_ref, wgt_ref, wut_ref, wdt_ref, g_ref, b_ref,
                   o_ref, xb_ref):
    j = pl.program_id(1)

    @pl.when(j == 0)
    def _():
        xb_ref[...] = x_ref[...].astype(BF16)
        o_ref[...] = jnp.zeros_like(o_ref)

    o_ref[...] += _swiglu_down(xb_ref[...], wg_ref[...], wu_ref[...], wd_ref[...])

    @pl.when(j == pl.num_programs(1) - 1)
    def _():
        acc = o_ref[...] + _swiglu_down(xb_ref[...], wgt_ref[...], wut_ref[...], wdt_ref[...])
        y = ALPHA * x_ref[...] + 0.5 * acc
        o_ref[...] = _layernorm_rows(y, g_ref[...], b_ref[...])


def _ffn_ln(x, weights, ln_g, ln_b):
    wg, wu, wd, wgt, wut, wdt = weights
    n, d = x.shape
    tm, tf = FFN_TM, FFN_TF
    nf = wd.shape[0] // tf
    ft = wdt.shape[0]
    once = dict(pipeline_mode=pl.Buffered(1))
    est = (2 * tm * d * 4 + 2 * tm * d * 4 + tm * d * 2
           + 2 * 3 * (d * tf * 2) + 3 * (d * ft * 2) + 3 * tm * tf * 4)
    return pl.pallas_call(
        _ffn_ln_kernel,
        grid=(n // tm, nf),
        in_specs=[
            pl.BlockSpec((tm, d), lambda i, j: (i, 0)),
            pl.BlockSpec((d, tf), lambda i, j: (0, j)),
            pl.BlockSpec((d, tf), lambda i, j: (0, j)),
            pl.BlockSpec((tf, d), lambda i, j: (j, 0)),
            pl.BlockSpec((d, ft), lambda i, j: (0, 0), **once),
            pl.BlockSpec((d, ft), lambda i, j: (0, 0), **once),
            pl.BlockSpec((ft, d), lambda i, j: (0, 0), **once),
            pl.BlockSpec((1, d), lambda i, j: (0, 0)),
            pl.BlockSpec((1, d), lambda i, j: (0, 0)),
        ],
        out_specs=pl.BlockSpec((tm, d), lambda i, j: (i, 0)),
        out_shape=jax.ShapeDtypeStruct((n, d), F32),
        scratch_shapes=[pltpu.VMEM((tm, d), BF16)],
        compiler_params=pltpu.CompilerParams(
            dimension_semantics=("parallel", "arbitrary"),
            vmem_limit_bytes=_vmem_limit(est)),
        name="ffn_ln",
    )(x, wg, wu, wd, wgt, wut, wdt, ln_g, ln_b)


def _ffn_weight_set(wg, wu, wd):
    full = (D_FF // FFN_TF) * FFN_TF
    return wg, wu, wd, wg[:, full:], wu[:, full:], wd[full:]


def _cast_job_specs(cast_jobs, steps, step_index):
    src_specs, dst_specs, dst_shapes, casts, nbytes = [], [], [], [], 0
    for src, rows, splits in cast_jobs:
        nblk = src.shape[0] // rows
        assert src.shape[0] % rows == 0 and nblk <= steps
        blk_map = lambda *idx, nblk=nblk: (jnp.minimum(step_index(*idx), nblk - 1), 0)
        src_specs.append(pl.BlockSpec((rows, src.shape[1]), blk_map))
        for lo, hi in splits:
            dst_specs.append(pl.BlockSpec((rows, hi - lo), blk_map))
            dst_shapes.append(jax.ShapeDtypeStruct((src.shape[0], hi - lo), BF16))
        casts.append(tuple(splits))
        nbytes += 2 * rows * src.shape[1] * 6
    return src_specs, dst_specs, dst_shapes, tuple(casts), nbytes


def _cast_rows(src_refs, dst_refs, casts):
    dsts = iter(dst_refs)
    for src, splits in zip(src_refs, casts):
        for lo, hi in splits:
            next(dsts)[...] = src[:, lo:hi].astype(BF16)


def _in_proj_kernel(x_ref, w_ref, lng_ref, lnb_ref, *refs, casts):
    n_src = len(casts)
    src_refs = refs[:n_src]
    o_ref, xb_ref = refs[n_src:n_src + 2]
    dst_refs = refs[n_src + 2:]
    j = pl.program_id(1)

    @pl.when(j == 0)
    def _():
        xb_ref[...] = x_ref[...].astype(BF16)

    def convert_weights():
        _cast_rows(src_refs, dst_refs, casts)

    @pl.when(j == 0)
    def _():
        convert_weights()
        o_ref[...] = _gelu_tanh(_dot(xb_ref[...], w_ref[...])).astype(o_ref.dtype)

    @pl.when(j == 1)
    def _():
        convert_weights()
        v = _gelu_tanh(_dot(xb_ref[...], w_ref[...]))
        o_ref[...] = _layernorm_rows(v, lng_ref[...], lnb_ref[...]).astype(o_ref.dtype)

    @pl.when(j >= 2)
    def _():
        convert_weights()
        o_ref[...] = _dot(xb_ref[...], w_ref[...]).astype(o_ref.dtype)


def _in_proj(x, w, gln_g, gln_b, cast_jobs):
    n, k = x.shape
    tm, tn = PROJ_TM, PROJ_TN
    assert tn == GMLP_WIDTH
    skip = SSM_WIDTH // tn
    m = w.shape[1] - SSM_WIDTH
    ncol = m // tn
    est = 2 * tm * k * 4 + 2 * tm * k * 2 + 2 * k * tn * 2 + 2 * tm * tn * 2 + 3 * tm * tn * 4
    src_specs, dst_specs, dst_shapes, casts, cast_bytes = _cast_job_specs(
        cast_jobs, (n // tm) * ncol, lambda i, j: i * ncol + j)
    est += cast_bytes
    res = pl.pallas_call(
        functools.partial(_in_proj_kernel, casts=casts),
        grid=(n // tm, ncol),
        in_specs=[pl.BlockSpec((tm, k), lambda i, j: (i, 0)),
                  pl.BlockSpec((k, tn), lambda i, j: (0, skip + j)),
                  pl.BlockSpec((1, tn), lambda i, j: (0, 0)),
                  pl.BlockSpec((1, tn), lambda i, j: (0, 0))] + src_specs,
        out_specs=[pl.BlockSpec((tm, tn), lambda i, j: (i, j)),
                   pl.BlockSpec((tm, k), lambda i, j: (i, 0))] + dst_specs,
        out_shape=[jax.ShapeDtypeStruct((n, m), BF16),
                   jax.ShapeDtypeStruct((n, k), BF16)] + dst_shapes,
        compiler_params=pltpu.CompilerParams(
            dimension_semantics=("arbitrary", "arbitrary"),
            vmem_limit_bytes=_vmem_limit(est)),
        name="in_proj",
    )(x, w, gln_g, gln_b, *[job[0] for job in cast_jobs])
    return res[0], res[1], res[2:]


def _branches_kernel(u_ref, v_ref, q_ref, kv_ref, ws_ref, bs_ref, ub_ref, ob_ref, *, tm):
    hd = GMLP_HEAD_DIM

    def chunk_body(ci, carry):
        rows = pl.ds(pl.multiple_of(ci * GMLP_CHUNK, GMLP_CHUNK), GMLP_CHUNK)
        for h in range(GMLP_HEADS):
            cols = slice(h * hd, (h + 1) * hd)
            mixed = _dot(ws_ref[h], v_ref[rows, cols]) + bs_ref[:, h:h + 1]
            ub_ref[rows, cols] = (u_ref[rows, cols].astype(F32) * mixed).astype(BF16)
        return carry

    lax.fori_loop(0, tm // GMLP_CHUNK, chunk_body, 0)

    ad = ATTN_HEAD_DIM
    scale = ad ** -0.5
    for h in range(ATTN_HEADS):
        cols = slice(h * ad, (h + 1) * ad)
        vcols = slice(ATTN_WIDTH + h * ad, ATTN_WIDTH + (h + 1) * ad)
        s = _dot_nt(q_ref[:, cols], kv_ref[:, cols]) * scale
        p = jnp.exp(s - jnp.max(s, axis=-1, keepdims=True))
        p = p * (1.0 / jnp.sum(p, axis=-1, keepdims=True))
        ob_ref[:, cols] = _dot(p.astype(BF16), kv_ref[:, vcols]).astype(BF16)


def _branches(proj, kv, w_s, b_s):
    n = proj.shape[0]
    tm = BRANCH_TM
    wid = GMLP_WIDTH
    assert wid == ATTN_WIDTH
    blocks_per_seq = SEQ_BLOCK // tm
    n_mem_batches = kv.shape[0] // N_MEM
    col = lambda c: pl.BlockSpec((tm, wid), lambda i: (i, c))
    full2 = lambda a: pl.BlockSpec(a.shape, lambda i: (0, 0))
    out = pl.BlockSpec((tm, wid), lambda i: (i, 0))
    return pl.pallas_call(
        functools.partial(_branches_kernel, tm=tm),
        grid=(n // tm,),
        in_specs=[col(0), col(1), col(2),
                  pl.BlockSpec((N_MEM, 2 * ATTN_WIDTH),
                               lambda i: (jnp.minimum(i // blocks_per_seq, n_mem_batches - 1), 0)),
                  pl.BlockSpec(w_s.shape, lambda i: (0, 0, 0)),
                  full2(b_s)],
        out_specs=[out, out],
        out_shape=[jax.ShapeDtypeStruct((n, wid), BF16), jax.ShapeDtypeStruct((n, wid), BF16)],
        compiler_params=pltpu.CompilerParams(dimension_semantics=("parallel",)),
        name="branches",
    )(proj, proj, proj, kv, w_s, b_s)


def _ssm_weight_t_kernel(w_ref, o_ref):
    o_ref[...] = w_ref[...].T.astype(o_ref.dtype)


def _ssm_weight_t(w_in):
    d = w_in.shape[0]
    tn = 2 * V7X_LANES
    return pl.pallas_call(
        _ssm_weight_t_kernel,
        grid=(SSM_WIDTH // tn,),
        in_specs=[pl.BlockSpec((d, tn), lambda j: (0, j))],
        out_specs=pl.BlockSpec((tn, d), lambda j: (j, 0)),
        out_shape=jax.ShapeDtypeStruct((SSM_WIDTH, d), BF16),
        compiler_params=pltpu.CompilerParams(dimension_semantics=("parallel",)),
        name="ssm_weight_t",
    )(w_in)


def _ssm_in_kernel(x_ref, wt_ref, o_ref, *, rb):
    d = wt_ref.shape[1]
    for k in range(rb):
        ut = _dot_nt(wt_ref[...], x_ref[:, k * d:(k + 1) * d])
        o_ref[:, k] = ut.astype(BF16).reshape(o_ref.shape[0], o_ref.shape[2], o_ref.shape[3])


def _ssm_in(xc, wt, *, T):
    nc = xc.shape[0]
    d = xc.shape[1] // T
    G, C = SSM_GROUPS, SSM_GROUP
    rb = SSM_IN_POSITIONS
    est = 2 * nc * rb * d * 2 + 2 * G * C * d * 2 + 2 * rb * G * C * nc * 2 + 2 * G * C * nc * 4
    return pl.pallas_call(
        functools.partial(_ssm_in_kernel, rb=rb),
        grid=(T // rb,),
        in_specs=[pl.BlockSpec((nc, rb * d), lambda r: (0, r)),
                  pl.BlockSpec((G * C, d), lambda r: (0, 0))],
        out_specs=pl.BlockSpec((G, rb, C, nc), lambda r: (0, r, 0, 0)),
        out_shape=jax.ShapeDtypeStruct((G, T, C, nc), BF16),
        compiler_params=pltpu.CompilerParams(
            dimension_semantics=("parallel",), vmem_limit_bytes=_vmem_limit(est)),
        name="ssm_in",
    )(xc, wt)


def _cmul(x, y):
    return x[0] * y[0] - x[1] * y[1], x[0] * y[1] + x[1] * y[0]


def _split_hi_lo(x):
    hi = x.astype(BF16)
    lo = (x - hi.astype(F32)).astype(BF16)
    return hi, lo


def _dot_hi_lo(a, b):
    ah, al = _split_hi_lo(a)
    bh, bl = _split_hi_lo(b)
    return _dot(ah, bh) + _dot(al, bh) + _dot(ah, bl)


N_TABLE_INPUTS = 6
N_TABLE_OUTPUTS = 3


def _ssm_tables_kernel(*refs, T, gb, casts):
    n_src = len(casts)
    n_dst = sum(len(splits) for splits in casts)
    ins = refs[:N_TABLE_INPUTS]
    srcs = refs[N_TABLE_INPUTS:N_TABLE_INPUTS + n_src]
    outs = refs[N_TABLE_INPUTS + n_src:N_TABLE_INPUTS + n_src + N_TABLE_OUTPUTS]
    dsts = refs[N_TABLE_INPUTS + n_src + N_TABLE_OUTPUTS:][:n_dst]
    q_ref = refs[-1]
    _cast_rows(srcs, dsts, casts)
    for gi in range(gb):
        _ssm_tables_group(*[r.at[gi] for r in ins + outs + (q_ref,)], T=T)


def _ssm_tables_group(pcol_ref, prow_ref, bcat_ref, cre_ref, cim_ref, dtile_ref,
                      mt_ref, gm_ref, cs_ref, q_ref, *, T):
    P, C = SSM_STATE, SSM_GROUP
    rpt = V7X_LANES // C
    nt = T // rpt
    wide = 2 * T * C

    lane = lax.broadcasted_iota(jnp.int32, (1, V7X_LANES), 1)
    rr = lax.shift_right_logical(lane, 4)
    expand = (lax.broadcasted_iota(jnp.int32, (C, V7X_LANES), 0)
              == (lax.broadcasted_iota(jnp.int32, (C, V7X_LANES), 1) & (C - 1))).astype(F32)

    pc = pcol_ref[...]

    lane4 = lax.broadcasted_iota(jnp.int32, (1, 4 * P), 1)
    is_re = (lax.shift_right_logical(lane4, 6) & 1) == 0
    is_f = lane4 < 2 * P
    pr = prow_ref[...]
    dt4 = jnp.exp(pr[2:3])
    zr4, zi4 = pr[0:1] * dt4, pr[1:2] * dt4

    kp = ((T + 1 + 7) // 8) * 8
    krow = lax.broadcasted_iota(jnp.int32, (kp, 4 * P), 0).astype(F32)
    mag = jnp.exp(krow * zr4)
    pw_r, pw_i = mag * jnp.cos(krow * zi4), mag * jnp.sin(krow * zi4)

    def states_on_rows(tab, lo):
        slab = tab[:, lo:lo + V7X_LANES]
        padded = jnp.concatenate([slab, jnp.zeros((V7X_LANES - kp, V7X_LANES), F32)], axis=0)
        return padded.T[0:P]

    pt_f = states_on_rows(pw_r, 0), states_on_rows(pw_i, 0)
    pt_b = states_on_rows(pw_r, 2 * P), states_on_rows(pw_i, 2 * P)

    def stair(pt, descending):
        re = jnp.zeros((P, V7X_LANES), F32)
        im = re
        for k in range(rpt):
            m = rpt - 1 - k if descending else k
            re = jnp.where(rr == k, pt[0][:, m:m + 1], re)
            im = jnp.where(rr == k, pt[1][:, m:m + 1], im)
        return re, im

    def zoh(d, pt):
        are = pc[:, d:d + 1]
        aim = pc[:, 2 + d:3 + d]
        nr = pt[0][:, 1:2] - 1.0
        ni = pt[1][:, 1:2]
        den = are * are + aim * aim
        return (nr * are + ni * aim) / den, (ni * are - nr * aim) / den

    def col(pt, m):
        return pt[0][:, m:m + 1], pt[1][:, m:m + 1]

    dsc_f = stair(pt_f, True)
    asc_b = stair(pt_b, False)
    btile = jnp.dot(bcat_ref[...], expand, precision=lax.Precision.HIGHEST,
                    preferred_element_type=F32)
    bt = (btile[0:P], btile[P:2 * P])
    bb_f = _cmul(zoh(0, pt_f), bt)
    bd_f = _cmul(dsc_f, bb_f)
    ba_b = _cmul(asc_b, _cmul(zoh(1, pt_b), bt))

    for j in range(nt):
        cols = slice(j * V7X_LANES, (j + 1) * V7X_LANES)
        xf = _cmul(bd_f, col(pt_f, T - rpt - rpt * j))
        xb = _cmul(ba_b, col(pt_b, rpt * j))
        gm_ref[0:P, cols] = xf[0].astype(BF16)
        gm_ref[P:2 * P, cols] = xf[1].astype(BF16)
        gm_ref[2 * P:3 * P, cols] = xb[0].astype(BF16)
        gm_ref[3 * P:4 * P, cols] = xb[1].astype(BF16)

    zeros = jnp.zeros((2 * P, V7X_LANES), F32)
    for j in range(2 * nt):
        cols = slice(j * V7X_LANES, (j + 1) * V7X_LANES)
        if j < nt:
            qf = _cmul(bd_f, col(pt_f, T - rpt * j - (rpt - 1)))
            q_ref[0:P, cols] = qf[0]
            q_ref[P:2 * P, cols] = qf[1]
            q_ref[2 * P:4 * P, cols] = zeros
        else:
            qb = _cmul(ba_b, col(pt_b, rpt * j - T))
            q_ref[2 * P:3 * P, cols] = qb[0]
            q_ref[3 * P:4 * P, cols] = qb[1]
            if j == nt:
                center = rr == 0
                q_ref[0:P, cols] = jnp.where(center, bb_f[0], 0.0)
                q_ref[P:2 * P, cols] = jnp.where(center, bb_f[1], 0.0)
            else:
                q_ref[0:2 * P, cols] = zeros

    cre = cre_ref[...]
    cim = cim_ref[...]

    for r in range(T):
        prr = jnp.where(is_f, pw_r[r + 1:r + 2], pw_r[T - r:T - r + 1])
        pii = jnp.where(is_f, pw_i[r + 1:r + 2], pw_i[T - r:T - r + 1])
        blk = jnp.where(is_re, cre * prr - cim * pii, -(cre * pii + cim * prr))
        cs_ref[r * C:(r + 1) * C, :] = blk.astype(BF16)

    lhs = jnp.where(is_re, cre, -cim)
    zt = _dot_hi_lo(lhs, q_ref[...])
    lanew = lax.broadcasted_iota(jnp.int32, (C, wide), 1)
    roww = lax.broadcasted_iota(jnp.int32, (C, wide), 0)
    diag = (lax.shift_right_logical(lanew, 4) == T) & ((lanew & (C - 1)) == roww)
    zt = zt + jnp.where(diag, dtile_ref[...], 0.0)

    for r in range(T):
        off = (T - r) * C
        shifted = pltpu.roll(zt, (wide - off) % wide, 1)
        mt_ref[r * C:(r + 1) * C, :] = shifted[:, :T * C].astype(BF16)


def _ssm_tables(a_re, a_im, log_dt, b_re, b_im, c_re, c_im, d_skip, cast_jobs, *, T):
    G, P, C = SSM_GROUPS, SSM_STATE, SSM_GROUP
    ldt = jnp.broadcast_to(log_dt[:, :, None], (2, G, P))
    zc = jnp.zeros((G, P), F32)
    pcol = jnp.stack([a_re[0], a_re[1], a_im[0], a_im[1], ldt[0], ldt[1], zc, zc], axis=-1)

    def row4(x):
        return jnp.concatenate([x[0], x[0], x[1], x[1]], axis=-1)

    zr4 = jnp.zeros((G, 4 * P), F32)
    prow = jnp.stack([row4(a_re), row4(a_im), row4(ldt), zr4, zr4, zr4, zr4, zr4], axis=1)
    bcat = jnp.concatenate([b_re, b_im], axis=1)
    cre4 = row4(c_re)
    cim4 = row4(c_im)
    dtile = jnp.tile(d_skip, (1, 2 * T))[:, None, :]

    tc = T * C
    gb = SSM_TABLE_GROUPS
    blk = lambda *shape: pl.BlockSpec((gb,) + shape, lambda g: (g,) + (0,) * len(shape))
    src_specs, dst_specs, dst_shapes, casts, cast_bytes = _cast_job_specs(
        cast_jobs, G // gb, lambda g: g)
    est = 2 * gb * (2 * tc * tc * 2 + 3 * 4 * P * tc * 4) + cast_bytes
    res = pl.pallas_call(
        functools.partial(_ssm_tables_kernel, T=T, gb=gb, casts=casts),
        grid=(G // gb,),
        in_specs=[blk(P, 8), blk(8, 4 * P), blk(2 * P, C), blk(C, 4 * P), blk(C, 4 * P),
                  blk(1, 2 * tc)] + src_specs,
        out_specs=[blk(tc, tc), blk(4 * P, tc), blk(tc, 4 * P)] + dst_specs,
        out_shape=[
            jax.ShapeDtypeStruct((G, tc, tc), BF16),
            jax.ShapeDtypeStruct((G, 4 * P, tc), BF16),
            jax.ShapeDtypeStruct((G, tc, 4 * P), BF16),
        ] + dst_shapes,
        scratch_shapes=[pltpu.VMEM((gb, 4 * P, 2 * tc), F32)],
        compiler_params=pltpu.CompilerParams(
            dimension_semantics=("arbitrary",), vmem_limit_bytes=_vmem_limit(est)),
        name="ssm_tables",
    )(pcol, prow, bcat, cre4, cim4, dtile, *[job[0] for job in cast_jobs])
    return tuple(res[:N_TABLE_OUTPUTS]), res[N_TABLE_OUTPUTS:]


def _ssm_state_kernel(u_ref, gm_ref, o_ref, *, gb, ns):
    nc = u_ref.shape[-1]
    for gi in range(gb):
        u = u_ref[gi].reshape(-1, nc)
        o_ref[:, gi * ns:(gi + 1) * ns] = _dot(gm_ref[gi], u).T


def _ssm_state(ut, gm):
    G, T, C, nc = ut.shape
    ns = gm.shape[1]
    gb = SSM_GROUPS_PER_STEP
    return pl.pallas_call(
        functools.partial(_ssm_state_kernel, gb=gb, ns=ns),
        grid=(G // gb,),
        in_specs=[pl.BlockSpec((gb, T, C, nc), lambda s: (s, 0, 0, 0)),
                  pl.BlockSpec((gb, ns, T * C), lambda s: (s, 0, 0))],
        out_specs=pl.BlockSpec((nc, gb * ns), lambda s: (0, s)),
        out_shape=jax.ShapeDtypeStruct((nc, G * ns), F32),
        compiler_params=pltpu.CompilerParams(dimension_semantics=("parallel",)),
        name="ssm_state",
    )(ut, gm)


def _ssm_scan_kernel(are_ref, aim_ref, ldt_ref, s_ref, o_ref, x_ref, xs_ref, *, T, n0, seq_starts, seq_ends):
    d = pl.program_id(0)
    j = pl.program_id(1)
    nblk = pl.num_programs(1)
    blk = j + d * (nblk - 1 - 2 * j)
    P = SSM_STATE

    is_start = functools.reduce(jnp.logical_or, [blk == s for s in seq_starts])
    is_end = functools.reduce(jnp.logical_or, [blk == e for e in seq_ends])
    reset = jnp.where(d == 0, is_start, is_end)

    @pl.when(reset)
    def _():
        x_ref[...] = jnp.zeros_like(x_ref)
        xs_ref[...] = jnp.zeros_like(xs_ref)

    dt = jnp.exp(ldt_ref[...])
    zr = are_ref[...] * dt
    zi = aim_ref[...] * dt
    mag = jnp.exp(float(T) * zr)
    mr = mag * jnp.cos(float(T) * zi)
    mi = mag * jnp.sin(float(T) * zi)
    lane = lax.broadcasted_iota(jnp.int32, mr.shape, 1)
    m2 = jnp.where(lane < P, -mi, mi)
    m2s = -m2

    def body(k, carry):
        x, xs = carry
        row = k + d * (n0 - 1 - 2 * k)
        loc = s_ref[row]
        o_ref[row] = x.astype(o_ref.dtype)
        locs = pltpu.roll(loc, P, 1)
        return x * mr + xs * m2 + loc, xs * mr + x * m2s + locs

    x, xs = lax.fori_loop(0, n0, body, (x_ref[...], xs_ref[...]), unroll=4)
    x_ref[...] = x
    xs_ref[...] = xs


def _ssm_scan(s_loc, a_re, a_im, log_dt, *, T, n_unit_seqs):
    G, P = SSM_GROUPS, SSM_STATE
    nc = s_loc.shape[0]
    n0 = SEQ_BLOCK // T
    nblk = nc // n0
    seq_starts = tuple(range(n_unit_seqs + 1))
    seq_ends = tuple(range(n_unit_seqs)) + (nblk - 1,)
    dup = lambda x: jnp.concatenate([x, x], axis=-1)
    are2, aim2 = dup(a_re), dup(a_im)
    ldt2 = jnp.broadcast_to(log_dt[:, :, None], (2, G, 2 * P))

    def blk_map(d, j):
        return (j + d * (nblk - 1 - 2 * j), 0, d)

    par = pl.BlockSpec((None, G, 2 * P), lambda d, j: (d, 0, 0))
    return pl.pallas_call(
        functools.partial(_ssm_scan_kernel, T=T, n0=n0, seq_starts=seq_starts, seq_ends=seq_ends),
        grid=(2, nblk),
        in_specs=[par, par, par, pl.BlockSpec((n0, G, 2 * P), blk_map)],
        out_specs=pl.BlockSpec((n0, G, 2 * P), blk_map),
        out_shape=jax.ShapeDtypeStruct((nc, G, 4 * P), BF16),
        scratch_shapes=[pltpu.VMEM((G, 2 * P), F32), pltpu.VMEM((G, 2 * P), F32)],
        compiler_params=pltpu.CompilerParams(dimension_semantics=("arbitrary", "arbitrary")),
        name="ssm_scan",
    )(are2, aim2, ldt2, s_loc)


def _ssm_out_kernel(u_ref, mt_ref, s_ref, cs_ref, y_ref, *, gb, ns):
    nc = u_ref.shape[-1]
    for gi in range(gb):
        u = u_ref[gi].reshape(-1, nc)
        y = _dot(mt_ref[gi], u) + _dot_nt(cs_ref[gi], s_ref[:, gi * ns:(gi + 1) * ns])
        y_ref[gi] = _gelu_tanh(y).astype(y_ref.dtype).reshape(y_ref.shape[1:])


def _ssm_out(ut, mt, s_in, cs):
    G, T, C, nc = ut.shape
    ns = cs.shape[2]
    gb = SSM_GROUPS_PER_STEP
    return pl.pallas_call(
        functools.partial(_ssm_out_kernel, gb=gb, ns=ns),
        grid=(G // gb,),
        in_specs=[pl.BlockSpec((gb, T, C, nc), lambda s: (s, 0, 0, 0)),
                  pl.BlockSpec((gb, T * C, T * C), lambda s: (s, 0, 0)),
                  pl.BlockSpec((nc, gb * ns), lambda s: (0, s)),
                  pl.BlockSpec((gb, T * C, ns), lambda s: (s, 0, 0))],
        out_specs=pl.BlockSpec((gb, T, C, nc), lambda s: (s, 0, 0, 0)),
        out_shape=jax.ShapeDtypeStruct((G, T, C, nc), BF16),
        compiler_params=pltpu.CompilerParams(dimension_semantics=("parallel",)),
        name="ssm_out",
    )(ut, mt, s_in, cs)


def _ssm_tok_kernel(y_ref, o_ref, stage_ref):
    T, ncb = y_ref.shape[1], y_ref.shape[-1]
    R = stage_ref.shape[2]
    row = lax.broadcasted_iota(jnp.int32, (R * ncb, R * ncb), 0)
    col = lax.broadcasted_iota(jnp.int32, (R * ncb, R * ncb), 1)
    sel = ((row // R == col % ncb) & (row % R == col // ncb)).astype(BF16)
    for k in range(T // R):
        ycat = jnp.concatenate([y_ref[:, k * R + rr].reshape(-1, ncb) for rr in range(R)], axis=1)
        tok = _dot_nt(sel, ycat)
        stage_ref[:, k] = tok.reshape(ncb, R, tok.shape[-1])
    o_ref[...] = stage_ref[...].reshape(o_ref.shape).astype(o_ref.dtype)


def _ssm_tok(yt):
    G, T, C, nc = yt.shape
    ncb = SSM_TOK_CHUNKS
    est = 2 * G * T * C * ncb * 2 + ncb * T * G * C * 4 + 2 * ncb * T * G * C * 2
    return pl.pallas_call(
        _ssm_tok_kernel,
        grid=(nc // ncb,),
        in_specs=[pl.BlockSpec((G, T, C, ncb), lambda s: (0, 0, 0, s))],
        out_specs=pl.BlockSpec((ncb * T, G * C), lambda s: (s, 0)),
        out_shape=jax.ShapeDtypeStruct((nc * T, G * C), BF16),
        scratch_shapes=[pltpu.VMEM((ncb, T // V7X_SUBLANES, V7X_SUBLANES, G * C), F32)],
        compiler_params=pltpu.CompilerParams(
            dimension_semantics=("parallel",), vmem_limit_bytes=_vmem_limit(est)),
        name="ssm_tok",
    )(yt)


def _s5_mixer_gelu(xb, wt, tables, a_re, a_im, log_dt, *, n_unit_seqs):
    T, G, P = SSM_CHUNK, SSM_GROUPS, SSM_STATE
    nc = xb.shape[0] // T
    mt, gm, cs = tables
    ut = _ssm_in(xb.reshape(nc, T * xb.shape[1]), wt, T=T)
    s_loc = _ssm_state(ut, gm)
    s_in = _ssm_scan(s_loc.reshape(nc, G, 4 * P), a_re, a_im, log_dt, T=T, n_unit_seqs=n_unit_seqs)
    yt = _ssm_out(ut, mt, s_in.reshape(nc, G * 4 * P), cs)
    return _ssm_tok(yt)


def _kv_kernel(m_ref, g_ref, b_ref, w_ref, o_ref, mb_ref):
    @pl.when(pl.program_id(0) == 0)
    def _():
        mb_ref[...] = _layernorm_rows(m_ref[...], g_ref[...], b_ref[...]).astype(BF16)

    o_ref[...] = _dot(mb_ref[...], w_ref[...]).astype(o_ref.dtype)


def _kv_proj(mem, ln_g, ln_b, w_kv):
    n, d = mem.shape
    m = w_kv.shape[1]
    tn = KV_TN
    est = 2 * n * d * 4 + n * d * 2 + 2 * d * tn * 2 + 2 * n * tn * 2 + 3 * n * d * 4
    return pl.pallas_call(
        _kv_kernel,
        grid=(m // tn,),
        in_specs=[pl.BlockSpec((n, d), lambda j: (0, 0)),
                  pl.BlockSpec((1, d), lambda j: (0, 0)),
                  pl.BlockSpec((1, d), lambda j: (0, 0)),
                  pl.BlockSpec((d, tn), lambda j: (0, j))],
        out_specs=pl.BlockSpec((n, tn), lambda j: (0, j)),
        out_shape=jax.ShapeDtypeStruct((n, m), BF16),
        scratch_shapes=[pltpu.VMEM((n, d), BF16)],
        compiler_params=pltpu.CompilerParams(
            dimension_semantics=("arbitrary",), vmem_limit_bytes=_vmem_limit(est)),
        name="kv_proj",
    )(mem, ln_g, ln_b, w_kv)


def _mixer_kernel(ya_ref, ub_ref, ob_ref, g0_ref, g1_ref, g2_ref,
                  wa_ref, wb_ref, wgp_ref, wap_ref, o_ref):
    ya = ya_ref[...]
    y_a = _dot(ya, wa_ref[...]) * _sigmoid(_dot(ya, wb_ref[...]))
    y_b = _dot(ub_ref[...], wgp_ref[...])
    y_c = _dot(ob_ref[...], wap_ref[...])
    merged = (_sigmoid(g0_ref[...].astype(F32)) * y_a
              + _sigmoid(g1_ref[...].astype(F32)) * y_b
              + _sigmoid(g2_ref[...].astype(F32)) * y_c)
    o_ref[...] = merged.astype(o_ref.dtype)


def _mixer(ya, ub, ob, proj, w_glu, w_gproj, w_aproj):
    n, wid = ya.shape
    d = D_MODEL
    tm, tn = MIX_TM, MIX_TN
    nn = d // tn
    gate0 = (proj.shape[1] - 3 * d) // tn
    gate_spec = lambda br: pl.BlockSpec((tm, tn), lambda i, j: (i, gate0 + br * nn + j))
    act_spec = pl.BlockSpec((tm, wid), lambda i, j: (i, 0))
    est = (2 * 3 * tm * wid * 2 + 2 * 3 * tm * tn * 2 + 2 * 4 * wid * tn * 2 + 2 * tm * tn * 2
           + 8 * tm * tn * 4)
    return pl.pallas_call(
        _mixer_kernel,
        grid=(n // tm, nn),
        in_specs=[
            act_spec, act_spec, act_spec,
            gate_spec(0), gate_spec(1), gate_spec(2),
            pl.BlockSpec((wid, tn), lambda i, j: (0, j)),
            pl.BlockSpec((wid, tn), lambda i, j: (0, nn + j)),
            pl.BlockSpec((wid, tn), lambda i, j: (0, j)),
            pl.BlockSpec((wid, tn), lambda i, j: (0, j)),
        ],
        out_specs=pl.BlockSpec((tm, tn), lambda i, j: (i, j)),
        out_shape=jax.ShapeDtypeStruct((n, d), BF16),
        compiler_params=pltpu.CompilerParams(
            dimension_semantics=("parallel", "parallel"),
            vmem_limit_bytes=_vmem_limit(est)),
        name="mixer",
    )(ya, ub, ob, proj, proj, proj, w_glu, w_glu, w_gproj, w_aproj)


def _outproj_ln_kernel(m_ref, w_ref, x_ref, g_ref, b_ref, o_ref):
    y = ALPHA * x_ref[...] + _dot(m_ref[...], w_ref[...])
    o_ref[...] = _layernorm_rows(y, g_ref[...], b_ref[...])


def _outproj_ln(merged, w_out, x, ln_g, ln_b):
    n, d = x.shape
    tm = OUT_TM
    est = 2 * tm * d * 2 + 2 * d * d * 2 + 4 * tm * d * 4 + 2 * tm * d * 4
    return pl.pallas_call(
        _outproj_ln_kernel,
        grid=(n // tm,),
        in_specs=[pl.BlockSpec((tm, d), lambda i: (i, 0)),
                  pl.BlockSpec((d, d), lambda i: (0, 0)),
                  pl.BlockSpec((tm, d), lambda i: (i, 0)),
                  pl.BlockSpec((1, d), lambda i: (0, 0)),
                  pl.BlockSpec((1, d), lambda i: (0, 0))],
        out_specs=pl.BlockSpec((tm, d), lambda i: (i, 0)),
        out_shape=jax.ShapeDtypeStruct((n, d), F32),
        compiler_params=pltpu.CompilerParams(
            dimension_semantics=("parallel",), vmem_limit_bytes=_vmem_limit(est)),
        name="outproj_ln",
    )(merged, w_out, x, ln_g, ln_b)


def _layer(groups, ffn1_w_gu, ffn1_w_down, ln1_g, ln1_b, w_in,
           ssm_a_re, ssm_a_im, ssm_log_dt, ssm_b_re, ssm_b_im, ssm_c_re, ssm_c_im, ssm_d, ssm_w_glu,
           gmlp_ln_g, gmlp_ln_b, gmlp_w_s, gmlp_b_s, gmlp_w_proj,
           mem_ln_g, mem_ln_b, attn_w_kv, attn_w_proj,
           w_out, ln2_g, ln2_b, ffn2_w_gu, ffn2_w_down, ln3_g, ln3_b):
    row = lambda v: v.reshape(1, -1)
    whole = lambda a: ((0, a.shape[1]),)
    gate_up = ((0, D_FF), (D_FF, 2 * D_FF))
    tables, early = _ssm_tables(
        ssm_a_re, ssm_a_im, ssm_log_dt, ssm_b_re, ssm_b_im, ssm_c_re, ssm_c_im, ssm_d,
        [(ffn1_w_gu, 32, gate_up), (ffn1_w_down, GMLP_CHUNK, whole(ffn1_w_down)),
         (w_in, 32, whole(w_in))], T=SSM_CHUNK)
    ffn1 = _ffn_weight_set(early[0], early[1], early[2])
    w_in_b = early[3]
    wt_ssm = _ssm_weight_t(w_in)
    w_kv, w_s = attn_w_kv.astype(BF16), gmlp_w_s.astype(BF16)

    jobs = [("ffn2_gu", ffn2_w_gu, 32, gate_up),
            ("ffn2_down", ffn2_w_down, GMLP_CHUNK, whole(ffn2_w_down)),
            ("glu", ssm_w_glu, 16, whole(ssm_w_glu)),
            ("out", w_out, 32, whole(w_out)),
            ("gproj", gmlp_w_proj, 16, whole(gmlp_w_proj)),
            ("aproj", attn_w_proj, 16, whole(attn_w_proj))]
    bf16_w = {}
    stage1 = []
    for gi, (x, mem, n_unit_seqs) in enumerate(groups):
        mine = jobs[gi::len(groups)]
        x1 = _ffn_ln(x, ffn1, row(ln1_g), row(ln1_b))
        proj, x1b, conv = _in_proj(x1, w_in_b, row(gmlp_ln_g), row(gmlp_ln_b),
                                   [job[1:] for job in mine])
        conv = iter(conv)
        for name, _, _, splits in mine:
            bf16_w[name] = [next(conv) for _ in splits]
        stage1.append((x1, proj, x1b))
    ffn2 = _ffn_weight_set(bf16_w["ffn2_gu"][0], bf16_w["ffn2_gu"][1], bf16_w["ffn2_down"][0])
    w_glu, w_out_b = bf16_w["glu"][0], bf16_w["out"][0]
    w_gproj, w_aproj = bf16_w["gproj"][0], bf16_w["aproj"][0]

    outs = []
    for (x, mem, n_unit_seqs), (x1, proj, x1b) in zip(groups, stage1):
        kv = _kv_proj(mem, row(mem_ln_g), row(mem_ln_b), w_kv)
        ub, ob = _branches(proj, kv, w_s, gmlp_b_s)
        ya = _s5_mixer_gelu(x1b, wt_ssm, tables, ssm_a_re, ssm_a_im, ssm_log_dt,
                            n_unit_seqs=n_unit_seqs)
        merged = _mixer(ya, ub, ob, proj, w_glu, w_gproj, w_aproj)
        x2 = _outproj_ln(merged, w_out_b, x1, row(ln2_g), row(ln2_b))
        outs.append(_ffn_ln(x2, ffn2, row(ln3_g), row(ln3_b)))
    return outs


def kernel(x_prompt, x_sample, mem_prompt, mem_sample, ffn1_w_gu, ffn1_w_down, ln1_g, ln1_b, w_in,
           ssm_a_re, ssm_a_im, ssm_log_dt, ssm_b_re, ssm_b_im, ssm_c_re, ssm_c_im, ssm_d, ssm_w_glu,
           gmlp_ln_g, gmlp_ln_b, gmlp_w_s, gmlp_b_s, gmlp_w_proj, mem_ln_g, mem_ln_b, attn_w_kv,
           attn_w_proj, w_out, ln2_g, ln2_b, ffn2_w_gu, ffn2_w_down, ln3_g, ln3_b):
    d = x_prompt.shape[-1]
    assert x_prompt.shape[1] == SEQ_BLOCK and x_sample.shape[0] == 1
    assert x_sample.shape[1] % SEQ_BLOCK == 0
    xs = [x_prompt.reshape(-1, d), x_sample.reshape(-1, d)]
    mems = [mem_prompt.reshape(-1, d), mem_sample.reshape(-1, d)]
    n_unit = [x_prompt.shape[0], 0]
    for l in range(DEPTH):
        xs = _layer(list(zip(xs, mems, n_unit)),
                    ffn1_w_gu[l], ffn1_w_down[l], ln1_g[l], ln1_b[l], w_in[l],
                    ssm_a_re[l], ssm_a_im[l], ssm_log_dt[l], ssm_b_re[l], ssm_b_im[l],
                    ssm_c_re[l], ssm_c_im[l], ssm_d[l], ssm_w_glu[l],
                    gmlp_ln_g[l], gmlp_ln_b[l], gmlp_w_s[l], gmlp_b_s[l], gmlp_w_proj[l],
                    mem_ln_g[l], mem_ln_b[l], attn_w_kv[l], attn_w_proj[l],
                    w_out[l], ln2_g[l], ln2_b[l], ffn2_w_gu[l], ffn2_w_down[l], ln3_g[l], ln3_b[l])
    return (xs[0].reshape(x_prompt.shape), xs[1].reshape(x_sample.shape))
```

```python
import functools
import math

import jax
import jax.numpy as jnp
from jax import lax
from jax.experimental import pallas as pl
from jax.experimental.pallas import tpu as pltpu

F32 = jnp.float32
BF16 = jnp.bfloat16

D_MODEL = 2048
DEPTH = 1
SEQ_BLOCK = 4096
N_MEM = 256
SSM_WIDTH = D_MODEL // 2
SSM_GROUP = 16
SSM_GROUPS = SSM_WIDTH // SSM_GROUP
SSM_STATE = 64
GMLP_WIDTH = D_MODEL // 2
GMLP_CHUNK = 128
GMLP_HEADS = 8
GMLP_HEAD_DIM = GMLP_WIDTH // GMLP_HEADS
ATTN_HEADS = 4
ATTN_HEAD_DIM = D_MODEL // 8
ATTN_WIDTH = ATTN_HEADS * ATTN_HEAD_DIM
D_FF = 5504
ALPHA = (2.0 * DEPTH) ** 0.25
LN_EPS = 1e-5

V7X_LANES = 128
V7X_SUBLANES = 8
V7X_VMEM_BYTES = 64 * 1024 * 1024

SSM_CHUNK = 32
SSM_GROUPS_PER_STEP = 16
SSM_TABLE_GROUPS = 1
SSM_IN_POSITIONS = 4
SSM_TOK_CHUNKS = 128
FFN_TM = 512
FFN_TF = 1024
PROJ_TM = 1024
PROJ_TN = 1024
BRANCH_TM = 1024
MIX_TM = 1024
MIX_TN = 512
OUT_TM = 512
KV_TN = 512


def _vmem_limit(nbytes):
    return int(min(nbytes + (16 << 20), V7X_VMEM_BYTES - (4 << 20)))


def _layernorm_rows(y, g, b):
    mu = jnp.mean(y, axis=-1, keepdims=True)
    yc = y - mu
    var = jnp.mean(yc * yc, axis=-1, keepdims=True)
    return yc * lax.rsqrt(var + LN_EPS) * g + b


def _gelu_tanh(x):
    c = math.sqrt(2.0 / math.pi)
    return 0.5 * x * (1.0 + jnp.tanh(c * (x + 0.044715 * (x * x * x))))


def _sigmoid(x):
    return 1.0 / (1.0 + jnp.exp(-x))


def _dot(a, b):
    return jnp.dot(a, b, preferred_element_type=F32)


def _dot_nt(a, b):
    return lax.dot_general(a, b, (((1,), (1,)), ((), ())), preferred_element_type=F32)


def _swiglu_down(xb, wg, wu, wd):
    gate = _dot(xb, wg)
    up = _dot(xb, wu)
    act = (gate * _sigmoid(gate) * up).astype(BF16)
    return _dot(act, wd)


def _ffn_ln_kernel(x_ref, wg_ref, wu_ref, wd_ref, wgt_ref, wut_ref, wdt_ref, g_ref, b_ref,
                   o_ref, xb_ref):
    j = pl.program_id(1)

    @pl.when(j == 0)
    def _():
        xb_ref[...] = x_ref[...].astype(BF16)
        o_ref[...] = jnp.zeros_like(o_ref)

    o_ref[...] += _swiglu_down(xb_ref[...], wg_ref[...], wu_ref[...], wd_ref[...])

    @pl.when(j == pl.num_programs(1) - 1)
    def _():
        acc = o_ref[...] + _swiglu_down(xb_ref[...], wgt_ref[...], wut_ref[...], wdt_ref[...])
        y = ALPHA * x_ref[...] + 0.5 * acc
        o_ref[...] = _layernorm_rows(y, g_ref[...], b_ref[...])


def _ffn_ln(x, weights, ln_g, ln_b):
    wg, wu, wd, wgt, wut, wdt = weights
    n, d = x.shape
    tm, tf = FFN_TM, FFN_TF
    nf = wd.shape[0] // tf
    ft = wdt.shape[0]
    once = dict(pipeline_mode=pl.Buffered(1))
    est = (2 * tm * d * 4 + 2 * tm * d * 4 + tm * d * 2
           + 2 * 3 * (d * tf * 2) + 3 * (d * ft * 2) + 3 * tm * tf * 4)
    return pl.pallas_call(
        _ffn_ln_kernel,
        grid=(n // tm, nf),
        in_specs=[
            pl.BlockSpec((tm, d), lambda i, j: (i, 0)),
            pl.BlockSpec((d, tf), lambda i, j: (0, j)),
            pl.BlockSpec((d, tf), lambda i, j: (0, j)),
            pl.BlockSpec((tf, d), lambda i, j: (j, 0)),
            pl.BlockSpec((d, ft), lambda i, j: (0, 0), **once),
            pl.BlockSpec((d, ft), lambda i, j: (0, 0), **once),
            pl.BlockSpec((ft, d), lambda i, j: (0, 0), **once),
            pl.BlockSpec((1, d), lambda i, j: (0, 0)),
            pl.BlockSpec((1, d), lambda i, j: (0, 0)),
        ],
        out_specs=pl.BlockSpec((tm, d), lambda i, j: (i, 0)),
        out_shape=jax.ShapeDtypeStruct((n, d), F32),
        scratch_shapes=[pltpu.VMEM((tm, d), BF16)],
        compiler_params=pltpu.CompilerParams(
            dimension_semantics=("parallel", "arbitrary"),
            vmem_limit_bytes=_vmem_limit(est)),
        name="ffn_ln",
    )(x, wg, wu, wd, wgt, wut, wdt, ln_g, ln_b)


def _ffn_weight_set(wg, wu, wd):
    full = (D_FF // FFN_TF) * FFN_TF
    return wg, wu, wd, wg[:, full:], wu[:, full:], wd[full:]


def _cast_job_specs(cast_jobs, steps, step_index):
    src_specs, dst_specs, dst_shapes, casts, nbytes = [], [], [], [], 0
    for src, rows, splits in cast_jobs:
        nblk = src.shape[0] // rows
        assert src.shape[0] % rows == 0 and nblk <= steps
        blk_map = lambda *idx, nblk=nblk: (jnp.minimum(step_index(*idx), nblk - 1), 0)
        src_specs.append(pl.BlockSpec((rows, src.shape[1]), blk_map))
        for lo, hi in splits:
            dst_specs.append(pl.BlockSpec((rows, hi - lo), blk_map))
            dst_shapes.append(jax.ShapeDtypeStruct((src.shape[0], hi - lo), BF16))
        casts.append(tuple(splits))
        nbytes += 2 * rows * src.shape[1] * 6
    return src_specs, dst_specs, dst_shapes, tuple(casts), nbytes


def _cast_rows(src_refs, dst_refs, casts):
    dsts = iter(dst_refs)
    for src, splits in zip(src_refs, casts):
        for lo, hi in splits:
            next(dsts)[...] = src[:, lo:hi].astype(BF16)


def _in_proj_kernel(x_ref, w_ref, lng_ref, lnb_ref, *refs, casts):
    n_src = len(casts)
    src_refs = refs[:n_src]
    o_ref, xb_ref = refs[n_src:n_src + 2]
    dst_refs = refs[n_src + 2:]
    j = pl.program_id(1)

    @pl.when(j == 0)
    def _():
        xb_ref[...] = x_ref[...].astype(BF16)

    def convert_weights():
        _cast_rows(src_refs, dst_refs, casts)

    @pl.when(j == 0)
    def _():
        convert_weights()
        o_ref[...] = _gelu_tanh(_dot(xb_ref[...], w_ref[...])).astype(o_ref.dtype)

    @pl.when(j == 1)
    def _():
        convert_weights()
        v = _gelu_tanh(_dot(xb_ref[...], w_ref[...]))
        o_ref[...] = _layernorm_rows(v, lng_ref[...], lnb_ref[...]).astype(o_ref.dtype)

    @pl.when(j >= 2)
    def _():
        convert_weights()
        o_ref[...] = _dot(xb_ref[...], w_ref[...]).astype(o_ref.dtype)


def _in_proj(x, w, gln_g, gln_b, cast_jobs):
    n, k = x.shape
    tm, tn = PROJ_TM, PROJ_TN
    assert tn == GMLP_WIDTH
    skip = SSM_WIDTH // tn
    m = w.shape[1] - SSM_WIDTH
    ncol = m // tn
    est = 2 * tm * k * 4 + 2 * tm * k * 2 + 2 * k * tn * 2 + 2 * tm * tn * 2 + 3 * tm * tn * 4
    src_specs, dst_specs, dst_shapes, casts, cast_bytes = _cast_job_specs(
        cast_jobs, (n // tm) * ncol, lambda i, j: i * ncol + j)
    est += cast_bytes
    res = pl.pallas_call(
        functools.partial(_in_proj_kernel, casts=casts),
        grid=(n // tm, ncol),
        in_specs=[pl.BlockSpec((tm, k), lambda i, j: (i, 0)),
                  pl.BlockSpec((k, tn), lambda i, j: (0, skip + j)),
                  pl.BlockSpec((1, tn), lambda i, j: (0, 0)),
                  pl.BlockSpec((1, tn), lambda i, j: (0, 0))] + src_specs,
        out_specs=[pl.BlockSpec((tm, tn), lambda i, j: (i, j)),
                   pl.BlockSpec((tm, k), lambda i, j: (i, 0))] + dst_specs,
        out_shape=[jax.ShapeDtypeStruct((n, m), BF16),
                   jax.ShapeDtypeStruct((n, k), BF16)] + dst_shapes,
        compiler_params=pltpu.CompilerParams(
            dimension_semantics=("arbitrary", "arbitrary"),
            vmem_limit_bytes=_vmem_limit(est)),
        name="in_proj",
    )(x, w, gln_g, gln_b, *[job[0] for job in cast_jobs])
    return res[0], res[1], res[2:]


def _branches_kernel(u_ref, v_ref, q_ref, kv_ref, ws_ref, bs_ref, ub_ref, ob_ref, *, tm):
    hd = GMLP_HEAD_DIM

    def chunk_body(ci, carry):
        rows = pl.ds(pl.multiple_of(ci * GMLP_CHUNK, GMLP_CHUNK), GMLP_CHUNK)
        for h in range(GMLP_HEADS):
            cols = slice(h * hd, (h + 1) * hd)
            mixed = _dot(ws_ref[h], v_ref[rows, cols]) + bs_ref[:, h:h + 1]
            ub_ref[rows, cols] = (u_ref[rows, cols].astype(F32) * mixed).astype(BF16)
        return carry

    lax.fori_loop(0, tm // GMLP_CHUNK, chunk_body, 0)

    ad = ATTN_HEAD_DIM
    scale = ad ** -0.5
    for h in range(ATTN_HEADS):
        cols = slice(h * ad, (h + 1) * ad)
        vcols = slice(ATTN_WIDTH + h * ad, ATTN_WIDTH + (h + 1) * ad)
        s = _dot_nt(q_ref[:, cols], kv_ref[:, cols]) * scale
        p = jnp.exp(s - jnp.max(s, axis=-1, keepdims=True))
        p = p * (1.0 / jnp.sum(p, axis=-1, keepdims=True))
        ob_ref[:, cols] = _dot(p.astype(BF16), kv_ref[:, vcols]).astype(BF16)


def _branches(proj, kv, w_s, b_s):
    n = proj.shape[0]
    tm = BRANCH_TM
    wid = GMLP_WIDTH
    assert wid == ATTN_WIDTH
    blocks_per_seq = SEQ_BLOCK // tm
    n_mem_batches = kv.shape[0] // N_MEM
    col = lambda c: pl.BlockSpec((tm, wid), lambda i: (i, c))
    full2 = lambda a: pl.BlockSpec(a.shape, lambda i: (0, 0))
    out = pl.BlockSpec((tm, wid), lambda i: (i, 0))
    return pl.pallas_call(
        functools.partial(_branches_kernel, tm=tm),
        grid=(n // tm,),
        in_specs=[col(0), col(1), col(2),
                  pl.BlockSpec((N_MEM, 2 * ATTN_WIDTH),
                               lambda i: (jnp.minimum(i // blocks_per_seq, n_mem_batches - 1), 0)),
                  pl.BlockSpec(w_s.shape, lambda i: (0, 0, 0)),
                  full2(b_s)],
        out_specs=[out, out],
        out_shape=[jax.ShapeDtypeStruct((n, wid), BF16), jax.ShapeDtypeStruct((n, wid), BF16)],
        compiler_params=pltpu.CompilerParams(dimension_semantics=("parallel",)),
        name="branches",
    )(proj, proj, proj, kv, w_s, b_s)


def _ssm_weight_t_kernel(w_ref, o_ref):
    o_ref[...] = w_ref[...].T.astype(o_ref.dtype)


def _ssm_weight_t(w_in):
    d = w_in.shape[0]
    tn = 2 * V7X_LANES
    return pl.pallas_call(
        _ssm_weight_t_kernel,
        grid=(SSM_WIDTH // tn,),
        in_specs=[pl.BlockSpec((d, tn), lambda j: (0, j))],
        out_specs=pl.BlockSpec((tn, d), lambda j: (j, 0)),
        out_shape=jax.ShapeDtypeStruct((SSM_WIDTH, d), BF16),
        compiler_params=pltpu.CompilerParams(dimension_semantics=("parallel",)),
        name="ssm_weight_t",
    )(w_in)


def _ssm_in_kernel(x_ref, wt_ref, o_ref, *, rb):
    d = wt_ref.shape[1]
    for k in range(rb):
        ut = _dot_nt(wt_ref[...], x_ref[:, k * d:(k + 1) * d])
        o_ref[:, k] = ut.astype(BF16).reshape(o_ref.shape[0], o_ref.shape[2], o_ref.shape[3])


def _ssm_in(xc, wt, *, T):
    nc = xc.shape[0]
    d = xc.shape[1] // T
    G, C = SSM_GROUPS, SSM_GROUP
    rb = SSM_IN_POSITIONS
    est = 2 * nc * rb * d * 2 + 2 * G * C * d * 2 + 2 * rb * G * C * nc * 2 + 2 * G * C * nc * 4
    return pl.pallas_call(
        functools.partial(_ssm_in_kernel, rb=rb),
        grid=(T // rb,),
        in_specs=[pl.BlockSpec((nc, rb * d), lambda r: (0, r)),
                  pl.BlockSpec((G * C, d), lambda r: (0, 0))],
        out_specs=pl.BlockSpec((G, rb, C, nc), lambda r: (0, r, 0, 0)),
        out_shape=jax.ShapeDtypeStruct((G, T, C, nc), BF16),
        compiler_params=pltpu.CompilerParams(
            dimension_semantics=("parallel",), vmem_limit_bytes=_vmem_limit(est)),
        name="ssm_in",
    )(xc, wt)


def _cmul(x, y):
    return x[0] * y[0] - x[1] * y[1], x[0] * y[1] + x[1] * y[0]


def _split_hi_lo(x):
    hi = x.astype(BF16)
    lo = (x - hi.astype(F32)).astype(BF16)
    return hi, lo


def _dot_hi_lo(a, b):
    ah, al = _split_hi_lo(a)
    bh, bl = _split_hi_lo(b)
    return _dot(ah, bh) + _dot(al, bh) + _dot(ah, bl)


N_TABLE_INPUTS = 6
N_TABLE_OUTPUTS = 3


def _ssm_tables_kernel(*refs, T, gb, casts):
    n_src = len(casts)
    n_dst = sum(len(splits) for splits in casts)
    ins = refs[:N_TABLE_INPUTS]
    srcs = refs[N_TABLE_INPUTS:N_TABLE_INPUTS + n_src]
    outs = refs[N_TABLE_INPUTS + n_src:N_TABLE_INPUTS + n_src + N_TABLE_OUTPUTS]
    dsts = refs[N_TABLE_INPUTS + n_src + N_TABLE_OUTPUTS:][:n_dst]
    q_ref = refs[-1]
    _cast_rows(srcs, dsts, casts)
    for gi in range(gb):
        _ssm_tables_group(*[r.at[gi] for r in ins + outs + (q_ref,)], T=T)


def _ssm_tables_group(pcol_ref, prow_ref, bcat_ref, cre_ref, cim_ref, dtile_ref,
                      mt_ref, gm_ref, cs_ref, q_ref, *, T):
    P, C = SSM_STATE, SSM_GROUP
    rpt = V7X_LANES // C
    nt = T // rpt
    wide = 2 * T * C

    lane = lax.broadcasted_iota(jnp.int32, (1, V7X_LANES), 1)
    rr = lax.shift_right_logical(lane, 4)
    expand = (lax.broadcasted_iota(jnp.int32, (C, V7X_LANES), 0)
              == (lax.broadcasted_iota(jnp.int32, (C, V7X_LANES), 1) & (C - 1))).astype(F32)

    pc = pcol_ref[...]

    lane4 = lax.broadcasted_iota(jnp.int32, (1, 4 * P), 1)
    is_re = (lax.shift_right_logical(lane4, 6) & 1) == 0
    is_f = lane4 < 2 * P
    pr = prow_ref[...]
    dt4 = jnp.exp(pr[2:3])
    zr4, zi4 = pr[0:1] * dt4, pr[1:2] * dt4

    kp = ((T + 1 + 7) // 8) * 8
    krow = lax.broadcasted_iota(jnp.int32, (kp, 4 * P), 0).astype(F32)
    mag = jnp.exp(krow * zr4)
    pw_r, pw_i = mag * jnp.cos(krow * zi4), mag * jnp.sin(krow * zi4)

    def states_on_rows(tab, lo):
        slab = tab[:, lo:lo + V7X_LANES]
        padded = jnp.concatenate([slab, jnp.zeros((V7X_LANES - kp, V7X_LANES), F32)], axis=0)
        return padded.T[0:P]

    pt_f = states_on_rows(pw_r, 0), states_on_rows(pw_i, 0)
    pt_b = states_on_rows(pw_r, 2 * P), states_on_rows(pw_i, 2 * P)

    def stair(pt, descending):
        re = jnp.zeros((P, V7X_LANES), F32)
        im = re
        for k in range(rpt):
            m = rpt - 1 - k if descending else k
            re = jnp.where(rr == k, pt[0][:, m:m + 1], re)
            im = jnp.where(rr == k, pt[1][:, m:m + 1], im)
        return re, im

    def zoh(d, pt):
        are = pc[:, d:d + 1]
        aim = pc[:, 2 + d:3 + d]
        nr = pt[0][:, 1:2] - 1.0
        ni = pt[1][:, 1:2]
        den = are * are + aim * aim
        return (nr * are + ni * aim) / den, (ni * are - nr * aim) / den

    def col(pt, m):
        return pt[0][:, m:m + 1], pt[1][:, m:m + 1]

    dsc_f = stair(pt_f, True)
    asc_b = stair(pt_b, False)
    btile = jnp.dot(bcat_ref[...], expand, precision=lax.Precision.HIGHEST,
                    preferred_element_type=F32)
    bt = (btile[0:P], btile[P:2 * P])
    bb_f = _cmul(zoh(0, pt_f), bt)
    bd_f = _cmul(dsc_f, bb_f)
    ba_b = _cmul(asc_b, _cmul(zoh(1, pt_b), bt))

    for j in range(nt):
        cols = slice(j * V7X_LANES, (j + 1) * V7X_LANES)
        xf = _cmul(bd_f, col(pt_f, T - rpt - rpt * j))
        xb = _cmul(ba_b, col(pt_b, rpt * j))
        gm_ref[0:P, cols] = xf[0].astype(BF16)
        gm_ref[P:2 * P, cols] = xf[1].astype(BF16)
        gm_ref[2 * P:3 * P, cols] = xb[0].astype(BF16)
        gm_ref[3 * P:4 * P, cols] = xb[1].astype(BF16)

    zeros = jnp.zeros((2 * P, V7X_LANES), F32)
    for j in range(2 * nt):
        cols = slice(j * V7X_LANES, (j + 1) * V7X_LANES)
        if j < nt:
            qf = _cmul(bd_f, col(pt_f, T - rpt * j - (rpt - 1)))
            q_ref[0:P, cols] = qf[0]
            q_ref[P:2 * P, cols] = qf[1]
            q_ref[2 * P:4 * P, cols] = zeros
        else:
            qb = _cmul(ba_b, col(pt_b, rpt * j - T))
            q_ref[2 * P:3 * P, cols] = qb[0]
            q_ref[3 * P:4 * P, cols] = qb[1]
            if j == nt:
                center = rr == 0
                q_ref[0:P, cols] = jnp.where(center, bb_f[0], 0.0)
                q_ref[P:2 * P, cols] = jnp.where(center, bb_f[1], 0.0)
            else:
                q_ref[0:2 * P, cols] = zeros

    cre = cre_ref[...]
    cim = cim_ref[...]

    for r in range(T):
        prr = jnp.where(is_f, pw_r[r + 1:r + 2], pw_r[T - r:T - r + 1])
        pii = jnp.where(is_f, pw_i[r + 1:r + 2], pw_i[T - r:T - r + 1])
        blk = jnp.where(is_re, cre * prr - cim * pii, -(cre * pii + cim * prr))
        cs_ref[r * C:(r + 1) * C, :] = blk.astype(BF16)

    lhs = jnp.where(is_re, cre, -cim)
    zt = _dot_hi_lo(lhs, q_ref[...])
    lanew = lax.broadcasted_iota(jnp.int32, (C, wide), 1)
    roww = lax.broadcasted_iota(jnp.int32, (C, wide), 0)
    diag = (lax.shift_right_logical(lanew, 4) == T) & ((lanew & (C - 1)) == roww)
    zt = zt + jnp.where(diag, dtile_ref[...], 0.0)

    for r in range(T):
        off = (T - r) * C
        shifted = pltpu.roll(zt, (wide - off) % wide, 1)
        mt_ref[r * C:(r + 1) * C, :] = shifted[:, :T * C].astype(BF16)


def _ssm_tables(a_re, a_im, log_dt, b_re, b_im, c_re, c_im, d_skip, cast_jobs, *, T):
    G, P, C = SSM_GROUPS, SSM_STATE, SSM_GROUP
    ldt = jnp.broadcast_to(log_dt[:, :, None], (2, G, P))
    zc = jnp.zeros((G, P), F32)
    pcol = jnp.stack([a_re[0], a_re[1], a_im[0], a_im[1], ldt[0], ldt[1], zc, zc], axis=-1)

    def row4(x):
        return jnp.concatenate([x[0], x[0], x[1], x[1]], axis=-1)

    zr4 = jnp.zeros((G, 4 * P), F32)
    prow = jnp.stack([row4(a_re), row4(a_im), row4(ldt), zr4, zr4, zr4, zr4, zr4], axis=1)
    bcat = jnp.concatenate([b_re, b_im], axis=1)
    cre4 = row4(c_re)
    cim4 = row4(c_im)
    dtile = jnp.tile(d_skip, (1, 2 * T))[:, None, :]

    tc = T * C
    gb = SSM_TABLE_GROUPS
    blk = lambda *shape: pl.BlockSpec((gb,) + shape, lambda g: (g,) + (0,) * len(shape))
    src_specs, dst_specs, dst_shapes, casts, cast_bytes = _cast_job_specs(
        cast_jobs, G // gb, lambda g: g)
    est = 2 * gb * (2 * tc * tc * 2 + 3 * 4 * P * tc * 4) + cast_bytes
    res = pl.pallas_call(
        functools.partial(_ssm_tables_kernel, T=T, gb=gb, casts=casts),
        grid=(G // gb,),
        in_specs=[blk(P, 8), blk(8, 4 * P), blk(2 * P, C), blk(C, 4 * P), blk(C, 4 * P),
                  blk(1, 2 * tc)] + src_specs,
        out_specs=[blk(tc, tc), blk(4 * P, tc), blk(tc, 4 * P)] + dst_specs,
        out_shape=[
            jax.ShapeDtypeStruct((G, tc, tc), BF16),
            jax.ShapeDtypeStruct((G, 4 * P, tc), BF16),
            jax.ShapeDtypeStruct((G, tc, 4 * P), BF16),
        ] + dst_shapes,
        scratch_shapes=[pltpu.VMEM((gb, 4 * P, 2 * tc), F32)],
        compiler_params=pltpu.CompilerParams(
            dimension_semantics=("arbitrary",), vmem_limit_bytes=_vmem_limit(est)),
        name="ssm_tables",
    )(pcol, prow, bcat, cre4, cim4, dtile, *[job[0] for job in cast_jobs])
    return tuple(res[:N_TABLE_OUTPUTS]), res[N_TABLE_OUTPUTS:]


def _ssm_state_kernel(u_ref, gm_ref, o_ref, *, gb, ns):
    nc = u_ref.shape[-1]
    for gi in range(gb):
        u = u_ref[gi].reshape(-1, nc)
        o_ref[:, gi * ns:(gi + 1) * ns] = _dot(gm_ref[gi], u).T


def _ssm_state(ut, gm):
    G, T, C, nc = ut.shape
    ns = gm.shape[1]
    gb = SSM_GROUPS_PER_STEP
    return pl.pallas_call(
        functools.partial(_ssm_state_kernel, gb=gb, ns=ns),
        grid=(G // gb,),
        in_specs=[pl.BlockSpec((gb, T, C, nc), lambda s: (s, 0, 0, 0)),
                  pl.BlockSpec((gb, ns, T * C), lambda s: (s, 0, 0))],
        out_specs=pl.BlockSpec((nc, gb * ns), lambda s: (0, s)),
        out_shape=jax.ShapeDtypeStruct((nc, G * ns), F32),
        compiler_params=pltpu.CompilerParams(dimension_semantics=("parallel",)),
        name="ssm_state",
    )(ut, gm)


def _ssm_scan_kernel(are_ref, aim_ref, ldt_ref, s_ref, o_ref, x_ref, xs_ref, *, T, n0, seq_starts, seq_ends):
    d = pl.program_id(0)
    j = pl.program_id(1)
    nblk = pl.num_programs(1)
    blk = j + d * (nblk - 1 - 2 * j)
    P = SSM_STATE

    is_start = functools.reduce(jnp.logical_or, [blk == s for s in seq_starts])
    is_end = functools.reduce(jnp.logical_or, [blk == e for e in seq_ends])
    reset = jnp.where(d == 0, is_start, is_end)

    @pl.when(reset)
    def _():
        x_ref[...] = jnp.zeros_like(x_ref)
        xs_ref[...] = jnp.zeros_like(xs_ref)

    dt = jnp.exp(ldt_ref[...])
    zr = are_ref[...] * dt
    zi = aim_ref[...] * dt
    mag = jnp.exp(float(T) * zr)
    mr = mag * jnp.cos(float(T) * zi)
    mi = mag * jnp.sin(float(T) * zi)
    lane = lax.broadcasted_iota(jnp.int32, mr.shape, 1)
    m2 = jnp.where(lane < P, -mi, mi)
    m2s = -m2

    def body(k, carry):
        x, xs = carry
        row = k + d * (n0 - 1 - 2 * k)
        loc = s_ref[row]
        o_ref[row] = x.astype(o_ref.dtype)
        locs = pltpu.roll(loc, P, 1)
        return x * mr + xs * m2 + loc, xs * mr + x * m2s + locs

    x, xs = lax.fori_loop(0, n0, body, (x_ref[...], xs_ref[...]), unroll=4)
    x_ref[...] = x
    xs_ref[...] = xs


def _ssm_scan(s_loc, a_re, a_im, log_dt, *, T, n_unit_seqs):
    G, P = SSM_GROUPS, SSM_STATE
    nc = s_loc.shape[0]
    n0 = SEQ_BLOCK // T
    nblk = nc // n0
    seq_starts = tuple(range(n_unit_seqs + 1))
    seq_ends = tuple(range(n_unit_seqs)) + (nblk - 1,)
    dup = lambda x: jnp.concatenate([x, x], axis=-1)
    are2, aim2 = dup(a_re), dup(a_im)
    ldt2 = jnp.broadcast_to(log_dt[:, :, None], (2, G, 2 * P))

    def blk_map(d, j):
        return (j + d * (nblk - 1 - 2 * j), 0, d)

    par = pl.BlockSpec((None, G, 2 * P), lambda d, j: (d, 0, 0))
    return pl.pallas_call(
        functools.partial(_ssm_scan_kernel, T=T, n0=n0, seq_starts=seq_starts, seq_ends=seq_ends),
        grid=(2, nblk),
        in_specs=[par, par, par, pl.BlockSpec((n0, G, 2 * P), blk_map)],
        out_specs=pl.BlockSpec((n0, G, 2 * P), blk_map),
        out_shape=jax.ShapeDtypeStruct((nc, G, 4 * P), BF16),
        scratch_shapes=[pltpu.VMEM((G, 2 * P), F32), pltpu.VMEM((G, 2 * P), F32)],
        compiler_params=pltpu.CompilerParams(dimension_semantics=("arbitrary", "arbitrary")),
        name="ssm_scan",
    )(are2, aim2, ldt2, s_loc)


def _ssm_out_kernel(u_ref, mt_ref, s_ref, cs_ref, y_ref, *, gb, ns):
    nc = u_ref.shape[-1]
    for gi in range(gb):
        u = u_ref[gi].reshape(-1, nc)
        y = _dot(mt_ref[gi], u) + _dot_nt(cs_ref[gi], s_ref[:, gi * ns:(gi + 1) * ns])
        y_ref[gi] = _gelu_tanh(y).astype(y_ref.dtype).reshape(y_ref.shape[1:])


def _ssm_out(ut, mt, s_in, cs):
    G, T, C, nc = ut.shape
    ns = cs.shape[2]
    gb = SSM_GROUPS_PER_STEP
    return pl.pallas_call(
        functools.partial(_ssm_out_kernel, gb=gb, ns=ns),
        grid=(G // gb,),
        in_specs=[pl.BlockSpec((gb, T, C, nc), lambda s: (s, 0, 0, 0)),
                  pl.BlockSpec((gb, T * C, T * C), lambda s: (s, 0, 0)),
                  pl.BlockSpec((nc, gb * ns), lambda s: (0, s)),
                  pl.BlockSpec((gb, T * C, ns), lambda s: (s, 0, 0))],
        out_specs=pl.BlockSpec((gb, T, C, nc), lambda s: (s, 0, 0, 0)),
        out_shape=jax.ShapeDtypeStruct((G, T, C, nc), BF16),
        compiler_params=pltpu.CompilerParams(dimension_semantics=("parallel",)),
        name="ssm_out",
    )(ut, mt, s_in, cs)


def _ssm_tok_kernel(y_ref, o_ref, stage_ref):
    T, ncb = y_ref.shape[1], y_ref.shape[-1]
    R = stage_ref.shape[2]
    row = lax.broadcasted_iota(jnp.int32, (R * ncb, R * ncb), 0)
    col = lax.broadcasted_iota(jnp.int32, (R * ncb, R * ncb), 1)
    sel = ((row // R == col % ncb) & (row % R == col // ncb)).astype(BF16)
    for k in range(T // R):
        ycat = jnp.concatenate([y_ref[:, k * R + rr].reshape(-1, ncb) for rr in range(R)], axis=1)
        tok = _dot_nt(sel, ycat)
        stage_ref[:, k] = tok.reshape(ncb, R, tok.shape[-1])
    o_ref[...] = stage_ref[...].reshape(o_ref.shape).astype(o_ref.dtype)


def _ssm_tok(yt):
    G, T, C, nc = yt.shape
    ncb = SSM_TOK_CHUNKS
    est = 2 * G * T * C * ncb * 2 + ncb * T * G * C * 4 + 2 * ncb * T * G * C * 2
    return pl.pallas_call(
        _ssm_tok_kernel,
        grid=(nc // ncb,),
        in_specs=[pl.BlockSpec((G, T, C, ncb), lambda s: (0, 0, 0, s))],
        out_specs=pl.BlockSpec((ncb * T, G * C), lambda s: (s, 0)),
        out_shape=jax.ShapeDtypeStruct((nc * T, G * C), BF16),
        scratch_shapes=[pltpu.VMEM((ncb, T // V7X_SUBLANES, V7X_SUBLANES, G * C), F32)],
        compiler_params=pltpu.CompilerParams(
            dimension_semantics=("parallel",), vmem_limit_bytes=_vmem_limit(est)),
        name="ssm_tok",
    )(yt)


def _s5_mixer_gelu(xb, wt, tables, a_re, a_im, log_dt, *, n_unit_seqs):
    T, G, P = SSM_CHUNK, SSM_GROUPS, SSM_STATE
    nc = xb.shape[0] // T
    mt, gm, cs = tables
    ut = _ssm_in(xb.reshape(nc, T * xb.shape[1]), wt, T=T)
    s_loc = _ssm_state(ut, gm)
    s_in = _ssm_scan(s_loc.reshape(nc, G, 4 * P), a_re, a_im, log_dt, T=T, n_unit_seqs=n_unit_seqs)
    yt = _ssm_out(ut, mt, s_in.reshape(nc, G * 4 * P), cs)
    return _ssm_tok(yt)


def _kv_kernel(m_ref, g_ref, b_ref, w_ref, o_ref, mb_ref):
    @pl.when(pl.program_id(0) == 0)
    def _():
        mb_ref[...] = _layernorm_rows(m_ref[...], g_ref[...], b_ref[...]).astype(BF16)

    o_ref[...] = _dot(mb_ref[...], w_ref[...]).astype(o_ref.dtype)


def _kv_proj(mem, ln_g, ln_b, w_kv):
    n, d = mem.shape
    m = w_kv.shape[1]
    tn = KV_TN
    est = 2 * n * d * 4 + n * d * 2 + 2 * d * tn * 2 + 2 * n * tn * 2 + 3 * n * d * 4
    return pl.pallas_call(
        _kv_kernel,
        grid=(m // tn,),
        in_specs=[pl.BlockSpec((n, d), lambda j: (0, 0)),
                  pl.BlockSpec((1, d), lambda j: (0, 0)),
                  pl.BlockSpec((1, d), lambda j: (0, 0)),
                  pl.BlockSpec((d, tn), lambda j: (0, j))],
        out_specs=pl.BlockSpec((n, tn), lambda j: (0, j)),
        out_shape=jax.ShapeDtypeStruct((n, m), BF16),
        scratch_shapes=[pltpu.VMEM((n, d), BF16)],
        compiler_params=pltpu.CompilerParams(
            dimension_semantics=("arbitrary",), vmem_limit_bytes=_vmem_limit(est)),
        name="kv_proj",
    )(mem, ln_g, ln_b, w_kv)


def _mixer_kernel(ya_ref, ub_ref, ob_ref, g0_ref, g1_ref, g2_ref,
                  wa_ref, wb_ref, wgp_ref, wap_ref, o_ref):
    ya = ya_ref[...]
    y_a = _dot(ya, wa_ref[...]) * _sigmoid(_dot(ya, wb_ref[...]))
    y_b = _dot(ub_ref[...], wgp_ref[...])
    y_c = _dot(ob_ref[...], wap_ref[...])
    merged = (_sigmoid(g0_ref[...].astype(F32)) * y_a
              + _sigmoid(g1_ref[...].astype(F32)) * y_b
              + _sigmoid(g2_ref[...].astype(F32)) * y_c)
    o_ref[...] = merged.astype(o_ref.dtype)


def _mixer(ya, ub, ob, proj, w_glu, w_gproj, w_aproj):
    n, wid = ya.shape
    d = D_MODEL
    tm, tn = MIX_TM, MIX_TN
    nn = d // tn
    gate0 = (proj.shape[1] - 3 * d) // tn
    gate_spec = lambda br: pl.BlockSpec((tm, tn), lambda i, j: (i, gate0 + br * nn + j))
    act_spec = pl.BlockSpec((tm, wid), lambda i, j: (i, 0))
    est = (2 * 3 * tm * wid * 2 + 2 * 3 * tm * tn * 2 + 2 * 4 * wid * tn * 2 + 2 * tm * tn * 2
           + 8 * tm * tn * 4)
    return pl.pallas_call(
        _mixer_kernel,
        grid=(n // tm, nn),
        in_specs=[
            act_spec, act_spec, act_spec,
            gate_spec(0), gate_spec(1), gate_spec(2),
            pl.BlockSpec((wid, tn), lambda i, j: (0, j)),
            pl.BlockSpec((wid, tn), lambda i, j: (0, nn + j)),
            pl.BlockSpec((wid, tn), lambda i, j: (0, j)),
            pl.BlockSpec((wid, tn), lambda i, j: (0, j)),
        ],
        out_specs=pl.BlockSpec((tm, tn), lambda i, j: (i, j)),
        out_shape=jax.ShapeDtypeStruct((n, d), BF16),
        compiler_params=pltpu.CompilerParams(
            dimension_semantics=("parallel", "parallel"),
            vmem_limit_bytes=_vmem_limit(est)),
        name="mixer",
    )(ya, ub, ob, proj, proj, proj, w_glu, w_glu, w_gproj, w_aproj)


def _outproj_ln_kernel(m_ref, w_ref, x_ref, g_ref, b_ref, o_ref):
    y = ALPHA * x_ref[...] + _dot(m_ref[...], w_ref[...])
    o_ref[...] = _layernorm_rows(y, g_ref[...], b_ref[...])


def _outproj_ln(merged, w_out, x, ln_g, ln_b):
    n, d = x.shape
    tm = OUT_TM
    est = 2 * tm * d * 2 + 2 * d * d * 2 + 4 * tm * d * 4 + 2 * tm * d * 4
    return pl.pallas_call(
        _outproj_ln_kernel,
        grid=(n // tm,),
        in_specs=[pl.BlockSpec((tm, d), lambda i: (i, 0)),
                  pl.BlockSpec((d, d), lambda i: (0, 0)),
                  pl.BlockSpec((tm, d), lambda i: (i, 0)),
                  pl.BlockSpec((1, d), lambda i: (0, 0)),
                  pl.BlockSpec((1, d), lambda i: (0, 0))],
        out_specs=pl.BlockSpec((tm, d), lambda i: (i, 0)),
        out_shape=jax.ShapeDtypeStruct((n, d), F32),
        compiler_params=pltpu.CompilerParams(
            dimension_semantics=("parallel",), vmem_limit_bytes=_vmem_limit(est)),
        name="outproj_ln",
    )(merged, w_out, x, ln_g, ln_b)


def _layer(groups, ffn1_w_gu, ffn1_w_down, ln1_g, ln1_b, w_in,
           ssm_a_re, ssm_a_im, ssm_log_dt, ssm_b_re, ssm_b_im, ssm_c_re, ssm_c_im, ssm_d, ssm_w_glu,
           gmlp_ln_g, gmlp_ln_b, gmlp_w_s, gmlp_b_s, gmlp_w_proj,
           mem_ln_g, mem_ln_b, attn_w_kv, attn_w_proj,
           w_out, ln2_g, ln2_b, ffn2_w_gu, ffn2_w_down, ln3_g, ln3_b):
    row = lambda v: v.reshape(1, -1)
    whole = lambda a: ((0, a.shape[1]),)
    gate_up = ((0, D_FF), (D_FF, 2 * D_FF))
    tables, early = _ssm_tables(
        ssm_a_re, ssm_a_im, ssm_log_dt, ssm_b_re, ssm_b_im, ssm_c_re, ssm_c_im, ssm_d,
        [(ffn1_w_gu, 32, gate_up), (ffn1_w_down, GMLP_CHUNK, whole(ffn1_w_down)),
         (w_in, 32, whole(w_in))], T=SSM_CHUNK)
    ffn1 = _ffn_weight_set(early[0], early[1], early[2])
    w_in_b = early[3]
    wt_ssm = _ssm_weight_t(w_in)
    w_kv, w_s = attn_w_kv.astype(BF16), gmlp_w_s.astype(BF16)

    jobs = [("ffn2_gu", ffn2_w_gu, 32, gate_up),
            ("ffn2_down", ffn2_w_down, GMLP_CHUNK, whole(ffn2_w_down)),
            ("glu", ssm_w_glu, 16, whole(ssm_w_glu)),
            ("out", w_out, 32, whole(w_out)),
            ("gproj", gmlp_w_proj, 16, whole(gmlp_w_proj)),
            ("aproj", attn_w_proj, 16, whole(attn_w_proj))]
    bf16_w = {}
    stage1 = []
    for gi, (x, mem, n_unit_seqs) in enumerate(groups):
        mine = jobs[gi::len(groups)]
        x1 = _ffn_ln(x, ffn1, row(ln1_g), row(ln1_b))
        proj, x1b, conv = _in_proj(x1, w_in_b, row(gmlp_ln_g), row(gmlp_ln_b),
                                   [job[1:] for job in mine])
        conv = iter(conv)
        for name, _, _, splits in mine:
            bf16_w[name] = [next(conv) for _ in splits]
        stage1.append((x1, proj, x1b))
    ffn2 = _ffn_weight_set(bf16_w["ffn2_gu"][0], bf16_w["ffn2_gu"][1], bf16_w["ffn2_down"][0])
    w_glu, w_out_b = bf16_w["glu"][0], bf16_w["out"][0]
    w_gproj, w_aproj = bf16_w["gproj"][0], bf16_w["aproj"][0]

    outs = []
    for (x, mem, n_unit_seqs), (x1, proj, x1b) in zip(groups, stage1):
        kv = _kv_proj(mem, row(mem_ln_g), row(mem_ln_b), w_kv)
        ub, ob = _branches(proj, kv, w_s, gmlp_b_s)
        ya = _s5_mixer_gelu(x1b, wt_ssm, tables, ssm_a_re, ssm_a_im, ssm_log_dt,
                            n_unit_seqs=n_unit_seqs)
        merged = _mixer(ya, ub, ob, proj, w_glu, w_gproj, w_aproj)
        x2 = _outproj_ln(merged, w_out_b, x1, row(ln2_g), row(ln2_b))
        outs.append(_ffn_ln(x2, ffn2, row(ln3_g), row(ln3_b)))
    return outs


def kernel(x_prompt, x_sample, mem_prompt, mem_sample, ffn1_w_gu, ffn1_w_down, ln1_g, ln1_b, w_in,
           ssm_a_re, ssm_a_im, ssm_log_dt, ssm_b_re, ssm_b_im, ssm_c_re, ssm_c_im, ssm_d, ssm_w_glu,
           gmlp_ln_g, gmlp_ln_b, gmlp_w_s, gmlp_b_s, gmlp_w_proj, mem_ln_g, mem_ln_b, attn_w_kv,
           attn_w_proj, w_out, ln2_g, ln2_b, ffn2_w_gu, ffn2_w_down, ln3_g, ln3_b):
    d = x_prompt.shape[-1]
    assert x_prompt.shape[1] == SEQ_BLOCK and x_sample.shape[0] == 1
    assert x_sample.shape[1] % SEQ_BLOCK == 0
    xs = [x_prompt.reshape(-1, d), x_sample.reshape(-1, d)]
    mems = [mem_prompt.reshape(-1, d), mem_sample.reshape(-1, d)]
    n_unit = [x_prompt.shape[0], 0]
    for l in range(DEPTH):
        xs = _layer(list(zip(xs, mems, n_unit)),
                    ffn1_w_gu[l], ffn1_w_down[l], ln1_g[l], ln1_b[l], w_in[l],
                    ssm_a_re[l], ssm_a_im[l], ssm_log_dt[l], ssm_b_re[l], ssm_b_im[l],
                    ssm_c_re[l], ssm_c_im[l], ssm_d[l], ssm_w_glu[l],
                    gmlp_ln_g[l], gmlp_ln_b[l], gmlp_w_s[l], gmlp_b_s[l], gmlp_w_proj[l],
                    mem_ln_g[l], mem_ln_b[l], attn_w_kv[l], attn_w_proj[l],
                    w_out[l], ln2_g[l], ln2_b[l], ffn2_w_gu[l], ffn2_w_down[l], ln3_g[l], ln3_b[l])
    return (xs[0].reshape(x_prompt.shape), xs[1].reshape(x_sample.shape))
```

```python
import functools
import math

import jax
import jax.numpy as jnp
from jax import lax
from jax.experimental import pallas as pl
from jax.experimental.pallas import tpu as pltpu

F32 = jnp.float32
BF16 = jnp.bfloat16

D_MODEL = 2048
DEPTH = 1
SEQ_BLOCK = 4096
N_MEM = 256
SSM_WIDTH = D_MODEL // 2
SSM_GROUP = 16
SSM_GROUPS = SSM_WIDTH // SSM_GROUP
SSM_STATE = 64
GMLP_WIDTH = D_MODEL // 2
GMLP_CHUNK = 128
GMLP_HEADS = 8
GMLP_HEAD_DIM = GMLP_WIDTH // GMLP_HEADS
ATTN_HEADS = 4
ATTN_HEAD_DIM = D_MODEL // 8
ATTN_WIDTH = ATTN_HEADS * ATTN_HEAD_DIM
D_FF = 5504
ALPHA = (2.0 * DEPTH) ** 0.25
LN_EPS = 1e-5

V7X_LANES = 128
V7X_SUBLANES = 8
V7X_VMEM_BYTES = 64 * 1024 * 1024

SSM_CHUNK = 32
SSM_GROUPS_PER_STEP = 16
SSM_TABLE_GROUPS = 1
SSM_IN_POSITIONS = 4
SSM_TOK_CHUNKS = 128
FFN_TM = 512
FFN_TF = 1024
PROJ_TM = 1024
PROJ_TN = 1024
BRANCH_TM = 1024
MIX_TM = 1024
MIX_TN = 512
OUT_TM = 512
KV_TN = 512


def _vmem_limit(nbytes):
    return int(min(nbytes + (16 << 20), V7X_VMEM_BYTES - (4 << 20)))


def _layernorm_rows(y, g, b):
    mu = jnp.mean(y, axis=-1, keepdims=True)
    yc = y - mu
    var = jnp.mean(yc * yc, axis=-1, keepdims=True)
    return yc * lax.rsqrt(var + LN_EPS) * g + b


def _gelu_tanh(x):
    c = math.sqrt(2.0 / math.pi)
    return 0.5 * x * (1.0 + jnp.tanh(c * (x + 0.044715 * (x * x * x))))


def _sigmoid(x):
    return 1.0 / (1.0 + jnp.exp(-x))


def _dot(a, b):
    return jnp.dot(a, b, preferred_element_type=F32)


def _dot_nt(a, b):
    return lax.dot_general(a, b, (((1,), (1,)), ((), ())), preferred_element_type=F32)


def _swiglu_down(xb, wg, wu, wd):
    gate = _dot(xb, wg)
    up = _dot(xb, wu)
    act = (gate * _sigmoid(gate) * up).astype(BF16)
    return _dot(act, wd)


def _ffn_ln_kernel(x_ref, wg_ref, wu_ref, wd_ref, wgt_ref, wut_ref, wdt_ref, g_ref, b_ref,
                   o_ref, xb_ref):
    j = pl.program_id(1)

    @pl.when(j == 0)
    def _():
        xb_ref[...] = x_ref[...].astype(BF16)
        o_ref[...] = jnp.zeros_like(o_ref)

    o_ref[...] += _swiglu_down(xb_ref[...], wg_ref[...], wu_ref[...], wd_ref[...])

    @pl.when(j == pl.num_programs(1) - 1)
    def _():
        acc = o_ref[...] + _swiglu_down(xb_ref[...], wgt_ref[...], wut_ref[...], wdt_ref[...])
        y = ALPHA * x_ref[...] + 0.5 * acc
        o_ref[...] = _layernorm_rows(y, g_ref[...], b_ref[...])


def _ffn_ln(x, weights, ln_g, ln_b):
    wg, wu, wd, wgt, wut, wdt = weights
    n, d = x.shape
    tm, tf = FFN_TM, FFN_TF
    nf = wd.shape[0] // tf
    ft = wdt.shape[0]
    once = dict(pipeline_mode=pl.Buffered(1))
    est = (2 * tm * d * 4 + 2 * tm * d * 4 + tm * d * 2
           + 2 * 3 * (d * tf * 2) + 3 * (d * ft * 2) + 3 * tm * tf * 4)
    return pl.pallas_call(
        _ffn_ln_kernel,
        grid=(n // tm, nf),
        in_specs=[
            pl.BlockSpec((tm, d), lambda i, j: (i, 0)),
            pl.BlockSpec((d, tf), lambda i, j: (0, j)),
            pl.BlockSpec((d, tf), lambda i, j: (0, j)),
            pl.BlockSpec((tf, d), lambda i, j: (j, 0)),
            pl.BlockSpec((d, ft), lambda i, j: (0, 0), **once),
            pl.BlockSpec((d, ft), lambda i, j: (0, 0), **once),
            pl.BlockSpec((ft, d), lambda i, j: (0, 0), **once),
            pl.BlockSpec((1, d), lambda i, j: (0, 0)),
            pl.BlockSpec((1, d), lambda i, j: (0, 0)),
        ],
        out_specs=pl.BlockSpec((tm, d), lambda i, j: (i, 0)),
        out_shape=jax.ShapeDtypeStruct((n, d), F32),
        scratch_shapes=[pltpu.VMEM((tm, d), BF16)],
        compiler_params=pltpu.CompilerParams(
            dimension_semantics=("parallel", "arbitrary"),
            vmem_limit_bytes=_vmem_limit(est)),
        name="ffn_ln",
    )(x, wg, wu, wd, wgt, wut, wdt, ln_g, ln_b)


def _ffn_weight_set(wg, wu, wd):
    full = (D_FF // FFN_TF) * FFN_TF
    return wg, wu, wd, wg[:, full:], wu[:, full:], wd[full:]


def _cast_job_specs(cast_jobs, steps, step_index):
    src_specs, dst_specs, dst_shapes, casts, nbytes = [], [], [], [], 0
    for src, rows, splits in cast_jobs:
        nblk = src.shape[0] // rows
        assert src.shape[0] % rows == 0 and nblk <= steps
        blk_map = lambda *idx, nblk=nblk: (jnp.minimum(step_index(*idx), nblk - 1), 0)
        src_specs.append(pl.BlockSpec((rows, src.shape[1]), blk_map))
        for lo, hi in splits:
            dst_specs.append(pl.BlockSpec((rows, hi - lo), blk_map))
            dst_shapes.append(jax.ShapeDtypeStruct((src.shape[0], hi - lo), BF16))
        casts.append(tuple(splits))
        nbytes += 2 * rows * src.shape[1] * 6
    return src_specs, dst_specs, dst_shapes, tuple(casts), nbytes


def _cast_rows(src_refs, dst_refs, casts):
    dsts = iter(dst_refs)
    for src, splits in zip(src_refs, casts):
        for lo, hi in splits:
            next(dsts)[...] = src[:, lo:hi].astype(BF16)


def _in_proj_kernel(x_ref, w_ref, lng_ref, lnb_ref, *refs, casts):
    n_src = len(casts)
    src_refs = refs[:n_src]
    o_ref, xb_ref = refs[n_src:n_src + 2]
    dst_refs = refs[n_src + 2:]
    j = pl.program_id(1)

    @pl.when(j == 0)
    def _():
        xb_ref[...] = x_ref[...].astype(BF16)

    def convert_weights():
        _cast_rows(src_refs, dst_refs, casts)

    @pl.when(j == 0)
    def _():
        convert_weights()
        o_ref[...] = _gelu_tanh(_dot(xb_ref[...], w_ref[...])).astype(o_ref.dtype)

    @pl.when(j == 1)
    def _():
        convert_weights()
        v = _gelu_tanh(_dot(xb_ref[...], w_ref[...]))
        o_ref[...] = _layernorm_rows(v, lng_ref[...], lnb_ref[...]).astype(o_ref.dtype)

    @pl.when(j >= 2)
    def _():
        convert_weights()
        o_ref[...] = _dot(xb_ref[...], w_ref[...]).astype(o_ref.dtype)


def _in_proj(x, w, gln_g, gln_b, cast_jobs):
    n, k = x.shape
    tm, tn = PROJ_TM, PROJ_TN
    assert tn == GMLP_WIDTH
    skip = SSM_WIDTH // tn
    m = w.shape[1] - SSM_WIDTH
    ncol = m // tn
    est = 2 * tm * k * 4 + 2 * tm * k * 2 + 2 * k * tn * 2 + 2 * tm * tn * 2 + 3 * tm * tn * 4
    src_specs, dst_specs, dst_shapes, casts, cast_bytes = _cast_job_specs(
        cast_jobs, (n // tm) * ncol, lambda i, j: i * ncol + j)
    est += cast_bytes
    res = pl.pallas_call(
        functools.partial(_in_proj_kernel, casts=casts),
        grid=(n // tm, ncol),
        in_specs=[pl.BlockSpec((tm, k), lambda i, j: (i, 0)),
                  pl.BlockSpec((k, tn), lambda i, j: (0, skip + j)),
                  pl.BlockSpec((1, tn), lambda i, j: (0, 0)),
                  pl.BlockSpec((1, tn), lambda i, j: (0, 0))] + src_specs,
        out_specs=[pl.BlockSpec((tm, tn), lambda i, j: (i, j)),
                   pl.BlockSpec((tm, k), lambda i, j: (i, 0))] + dst_specs,
        out_shape=[jax.ShapeDtypeStruct((n, m), BF16),
                   jax.ShapeDtypeStruct((n, k), BF16)] + dst_shapes,
        compiler_params=pltpu.CompilerParams(
            dimension_semantics=("arbitrary", "arbitrary"),
            vmem_limit_bytes=_vmem_limit(est)),
        name="in_proj",
    )(x, w, gln_g, gln_b, *[job[0] for job in cast_jobs])
    return res[0], res[1], res[2:]


def _branches_kernel(u_ref, v_ref, q_ref, kv_ref, ws_ref, bs_ref, ub_ref, ob_ref, *, tm):
    hd = GMLP_HEAD_DIM

    def chunk_body(ci, carry):
        rows = pl.ds(pl.multiple_of(ci * GMLP_CHUNK, GMLP_CHUNK), GMLP_CHUNK)
        for h in range(GMLP_HEADS):
            cols = slice(h * hd, (h + 1) * hd)
            mixed = _dot(ws_ref[h], v_ref[rows, cols]) + bs_ref[:, h:h + 1]
            ub_ref[rows, cols] = (u_ref[rows, cols].astype(F32) * mixed).astype(BF16)
        return carry

    lax.fori_loop(0, tm // GMLP_CHUNK, chunk_body, 0)

    ad = ATTN_HEAD_DIM
    scale = ad ** -0.5
    for h in range(ATTN_HEADS):
        cols = slice(h * ad, (h + 1) * ad)
        vcols = slice(ATTN_WIDTH + h * ad, ATTN_WIDTH + (h + 1) * ad)
        s = _dot_nt(q_ref[:, cols], kv_ref[:, cols]) * scale
        p = jnp.exp(s - jnp.max(s, axis=-1, keepdims=True))
        p = p * (1.0 / jnp.sum(p, axis=-1, keepdims=True))
        ob_ref[:, cols] = _dot(p.astype(BF16), kv_ref[:, vcols]).astype(BF16)


def _branches(proj, kv, w_s, b_s):
    n = proj.shape[0]
    tm = BRANCH_TM
    wid = GMLP_WIDTH
    assert wid == ATTN_WIDTH
    blocks_per_seq = SEQ_BLOCK // tm
    n_mem_batches = kv.shape[0] // N_MEM
    col = lambda c: pl.BlockSpec((tm, wid), lambda i: (i, c))
    full2 = lambda a: pl.BlockSpec(a.shape, lambda i: (0, 0))
    out = pl.BlockSpec((tm, wid), lambda i: (i, 0))
    return pl.pallas_call(
        functools.partial(_branches_kernel, tm=tm),
        grid=(n // tm,),
        in_specs=[col(0), col(1), col(2),
                  pl.BlockSpec((N_MEM, 2 * ATTN_WIDTH),
                               lambda i: (jnp.minimum(i // blocks_per_seq, n_mem_batches - 1), 0)),
                  pl.BlockSpec(w_s.shape, lambda i: (0, 0, 0)),
                  full2(b_s)],
        out_specs=[out, out],
        out_shape=[jax.ShapeDtypeStruct((n, wid), BF16), jax.ShapeDtypeStruct((n, wid), BF16)],
        compiler_params=pltpu.CompilerParams(dimension_semantics=("parallel",)),
        name="branches",
    )(proj, proj, proj, kv, w_s, b_s)


def _ssm_weight_t_kernel(w_ref, o_ref):
    o_ref[...] = w_ref[...].T.astype(o_ref.dtype)


def _ssm_weight_t(w_in):
    d = w_in.shape[0]
    tn = 2 * V7X_LANES
    return pl.pallas_call(
        _ssm_weight_t_kernel,
        grid=(SSM_WIDTH // tn,),
        in_specs=[pl.BlockSpec((d, tn), lambda j: (0, j))],
        out_specs=pl.BlockSpec((tn, d), lambda j: (j, 0)),
        out_shape=jax.ShapeDtypeStruct((SSM_WIDTH, d), BF16),
        compiler_params=pltpu.CompilerParams(dimension_semantics=("parallel",)),
        name="ssm_weight_t",
    )(w_in)


def _ssm_in_kernel(x_ref, wt_ref, o_ref, *, rb):
    d = wt_ref.shape[1]
    for k in range(rb):
        ut = _dot_nt(wt_ref[...], x_ref[:, k * d:(k + 1) * d])
        o_ref[:, k] = ut.astype(BF16).reshape(o_ref.shape[0], o_ref.shape[2], o_ref.shape[3])


def _ssm_in(xc, wt, *, T):
    nc = xc.shape[0]
    d = xc.shape[1] // T
    G, C = SSM_GROUPS, SSM_GROUP
    rb = SSM_IN_POSITIONS
    est = 2 * nc * rb * d * 2 + 2 * G * C * d * 2 + 2 * rb * G * C * nc * 2 + 2 * G * C * nc * 4
    return pl.pallas_call(
        functools.partial(_ssm_in_kernel, rb=rb),
        grid=(T // rb,),
        in_specs=[pl.BlockSpec((nc, rb * d), lambda r: (0, r)),
                  pl.BlockSpec((G * C, d), lambda r: (0, 0))],
        out_specs=pl.BlockSpec((G, rb, C, nc), lambda r: (0, r, 0, 0)),
        out_shape=jax.ShapeDtypeStruct((G, T, C, nc), BF16),
        compiler_params=pltpu.CompilerParams(
            dimension_semantics=("parallel",), vmem_limit_bytes=_vmem_limit(est)),
        name="ssm_in",
    )(xc, wt)


def _cmul(x, y):
    return x[0] * y[0] - x[1] * y[1], x[0] * y[1] + x[1] * y[0]


def _split_hi_lo(x):
    hi = x.astype(BF16)
    lo = (x - hi.astype(F32)).astype(BF16)
    return hi, lo


def _dot_hi_lo(a, b):
    ah, al = _split_hi_lo(a)
    bh, bl = _split_hi_lo(b)
    return _dot(ah, bh) + _dot(al, bh) + _dot(ah, bl)


N_TABLE_INPUTS = 2
N_TABLE_OUTPUTS = 3


def _ssm_tables_kernel(*refs, T, gb, casts):
    n_src = len(casts)
    n_dst = sum(len(splits) for splits in casts)
    ins = refs[:N_TABLE_INPUTS]
    srcs = refs[N_TABLE_INPUTS:N_TABLE_INPUTS + n_src]
    outs = refs[N_TABLE_INPUTS + n_src:N_TABLE_INPUTS + n_src + N_TABLE_OUTPUTS]
    dsts = refs[N_TABLE_INPUTS + n_src + N_TABLE_OUTPUTS:][:n_dst]
    q_ref = refs[-1]
    _cast_rows(srcs, dsts, casts)
    for gi in range(gb):
        _ssm_tables_group(*[r.at[gi] for r in ins + outs + (q_ref,)], T=T)


def _ssm_tables_group(col_ref, row_ref, mt_ref, gm_ref, cs_ref, q_ref, *, T):
    P, C = SSM_STATE, SSM_GROUP
    rpt = V7X_LANES // C
    nt = T // rpt
    wide = 2 * T * C

    lane = lax.broadcasted_iota(jnp.int32, (1, V7X_LANES), 1)
    rr = lax.shift_right_logical(lane, 4)
    expand = (lax.broadcasted_iota(jnp.int32, (C, V7X_LANES), 0)
              == (lax.broadcasted_iota(jnp.int32, (C, V7X_LANES), 1) & (C - 1))).astype(F32)

    pc = col_ref[0:P, C:C + 8]

    lane4 = lax.broadcasted_iota(jnp.int32, (1, 4 * P), 1)
    is_re = (lax.shift_right_logical(lane4, 6) & 1) == 0
    is_f = lane4 < 2 * P
    pr = row_ref[0:8]
    dt4 = jnp.exp(pr[2:3])
    zr4, zi4 = pr[0:1] * dt4, pr[1:2] * dt4

    kp = ((T + 1 + 7) // 8) * 8
    krow = lax.broadcasted_iota(jnp.int32, (kp, 4 * P), 0).astype(F32)
    mag = jnp.exp(krow * zr4)
    pw_r, pw_i = mag * jnp.cos(krow * zi4), mag * jnp.sin(krow * zi4)

    def states_on_rows(tab, lo):
        slab = tab[:, lo:lo + V7X_LANES]
        padded = jnp.concatenate([slab, jnp.zeros((V7X_LANES - kp, V7X_LANES), F32)], axis=0)
        return padded.T[0:P]

    pt_f = states_on_rows(pw_r, 0), states_on_rows(pw_i, 0)
    pt_b = states_on_rows(pw_r, 2 * P), states_on_rows(pw_i, 2 * P)

    def stair(pt, descending):
        re = jnp.zeros((P, V7X_LANES), F32)
        im = re
        for k in range(rpt):
            m = rpt - 1 - k if descending else k
            re = jnp.where(rr == k, pt[0][:, m:m + 1], re)
            im = jnp.where(rr == k, pt[1][:, m:m + 1], im)
        return re, im

    def zoh(d, pt):
        are = pc[:, d:d + 1]
        aim = pc[:, 2 + d:3 + d]
        nr = pt[0][:, 1:2] - 1.0
        ni = pt[1][:, 1:2]
        den = are * are + aim * aim
        return (nr * are + ni * aim) / den, (ni * are - nr * aim) / den

    def col(pt, m):
        return pt[0][:, m:m + 1], pt[1][:, m:m + 1]

    dsc_f = stair(pt_f, True)
    asc_b = stair(pt_b, False)
    btile = jnp.dot(col_ref[:, 0:C], expand, precision=lax.Precision.HIGHEST,
                    preferred_element_type=F32)
    bt = (btile[0:P], btile[P:2 * P])
    bb_f = _cmul(zoh(0, pt_f), bt)
    bd_f = _cmul(dsc_f, bb_f)
    ba_b = _cmul(asc_b, _cmul(zoh(1, pt_b), bt))

    for j in range(nt):
        cols = slice(j * V7X_LANES, (j + 1) * V7X_LANES)
        xf = _cmul(bd_f, col(pt_f, T - rpt - rpt * j))
        xb = _cmul(ba_b, col(pt_b, rpt * j))
        gm_ref[0:P, cols] = xf[0].astype(BF16)
        gm_ref[P:2 * P, cols] = xf[1].astype(BF16)
        gm_ref[2 * P:3 * P, cols] = xb[0].astype(BF16)
        gm_ref[3 * P:4 * P, cols] = xb[1].astype(BF16)

    zeros = jnp.zeros((2 * P, V7X_LANES), F32)
    for j in range(2 * nt):
        cols = slice(j * V7X_LANES, (j + 1) * V7X_LANES)
        if j < nt:
            qf = _cmul(bd_f, col(pt_f, T - rpt * j - (rpt - 1)))
            q_ref[0:P, cols] = qf[0]
            q_ref[P:2 * P, cols] = qf[1]
            q_ref[2 * P:4 * P, cols] = zeros
        else:
            qb = _cmul(ba_b, col(pt_b, rpt * j - T))
            q_ref[2 * P:3 * P, cols] = qb[0]
            q_ref[3 * P:4 * P, cols] = qb[1]
            if j == nt:
                center = rr == 0
                q_ref[0:P, cols] = jnp.where(center, bb_f[0], 0.0)
                q_ref[P:2 * P, cols] = jnp.where(center, bb_f[1], 0.0)
            else:
                q_ref[0:2 * P, cols] = zeros

    cre = row_ref[8:8 + C]
    cim = row_ref[8 + C:8 + 2 * C]

    for r in range(T):
        prr = jnp.where(is_f, pw_r[r + 1:r + 2], pw_r[T - r:T - r + 1])
        pii = jnp.where(is_f, pw_i[r + 1:r + 2], pw_i[T - r:T - r + 1])
        blk = jnp.where(is_re, cre * prr - cim * pii, -(cre * pii + cim * prr))
        cs_ref[r * C:(r + 1) * C, :] = blk.astype(BF16)

    lhs = jnp.where(is_re, cre, -cim)
    zt = _dot_hi_lo(lhs, q_ref[...])
    lanew = lax.broadcasted_iota(jnp.int32, (C, wide), 1)
    roww = lax.broadcasted_iota(jnp.int32, (C, wide), 0)
    diag = (lax.shift_right_logical(lanew, 4) == T) & ((lanew & (C - 1)) == roww)
    d0 = 8 + 2 * C
    dtile = jnp.concatenate([row_ref[d0 + k:d0 + k + 1] for k in range(wide // (4 * P))], axis=1)
    zt = zt + jnp.where(diag, dtile, 0.0)

    for r in range(T):
        off = (T - r) * C
        shifted = pltpu.roll(zt, (wide - off) % wide, 1)
        mt_ref[r * C:(r + 1) * C, :] = shifted[:, :T * C].astype(BF16)


def _ssm_tables(a_re, a_im, log_dt, b_re, b_im, c_re, c_im, d_skip, cast_jobs, *, T):
    G, P, C = SSM_GROUPS, SSM_STATE, SSM_GROUP
    ldt = jnp.broadcast_to(log_dt[:, :, None], (2, G, P))
    zc = jnp.zeros((G, P), F32)
    pcol = jnp.stack([a_re[0], a_re[1], a_im[0], a_im[1], ldt[0], ldt[1], zc, zc], axis=-1)

    def row4(x):
        return jnp.concatenate([x[0], x[0], x[1], x[1]], axis=-1)

    zr4 = jnp.zeros((G, 4 * P), F32)
    prow = jnp.stack([row4(a_re), row4(a_im), row4(ldt), zr4, zr4, zr4, zr4, zr4], axis=1)
    bcat = jnp.concatenate([b_re, b_im], axis=1)
    dtile = jnp.tile(d_skip, (1, 2 * T)).reshape(G, -1, 4 * P)
    colpack = jnp.concatenate([bcat, jnp.pad(pcol, ((0, 0), (0, P), (0, 0)))], axis=-1)
    rowpack = jnp.concatenate([prow, row4(c_re), row4(c_im), dtile], axis=1)

    tc = T * C
    gb = SSM_TABLE_GROUPS
    blk = lambda *shape: pl.BlockSpec((gb,) + shape, lambda g: (g,) + (0,) * len(shape))
    src_specs, dst_specs, dst_shapes, casts, cast_bytes = _cast_job_specs(
        cast_jobs, G // gb, lambda g: g)
    est = 2 * gb * (2 * tc * tc * 2 + 3 * 4 * P * tc * 4) + cast_bytes
    res = pl.pallas_call(
        functools.partial(_ssm_tables_kernel, T=T, gb=gb, casts=casts),
        grid=(G // gb,),
        in_specs=[blk(*colpack.shape[1:]), blk(*rowpack.shape[1:])] + src_specs,
        out_specs=[blk(tc, tc), blk(4 * P, tc), blk(tc, 4 * P)] + dst_specs,
        out_shape=[
            jax.ShapeDtypeStruct((G, tc, tc), BF16),
            jax.ShapeDtypeStruct((G, 4 * P, tc), BF16),
            jax.ShapeDtypeStruct((G, tc, 4 * P), BF16),
        ] + dst_shapes,
        scratch_shapes=[pltpu.VMEM((gb, 4 * P, 2 * tc), F32)],
        compiler_params=pltpu.CompilerParams(
            dimension_semantics=("arbitrary",), vmem_limit_bytes=_vmem_limit(est)),
        name="ssm_tables",
    )(colpack, rowpack, *[job[0] for job in cast_jobs])
    return tuple(res[:N_TABLE_OUTPUTS]), res[N_TABLE_OUTPUTS:]


def _ssm_state_kernel(u_ref, gm_ref, o_ref, *, gb, ns):
    nc = u_ref.shape[-1]
    for gi in range(gb):
        u = u_ref[gi].reshape(-1, nc)
        o_ref[:, gi * ns:(gi + 1) * ns] = _dot(gm_ref[gi], u).T


def _ssm_state(ut, gm):
    G, T, C, nc = ut.shape
    ns = gm.shape[1]
    gb = SSM_GROUPS_PER_STEP
    return pl.pallas_call(
        functools.partial(_ssm_state_kernel, gb=gb, ns=ns),
        grid=(G // gb,),
        in_specs=[pl.BlockSpec((gb, T, C, nc), lambda s: (s, 0, 0, 0)),
                  pl.BlockSpec((gb, ns, T * C), lambda s: (s, 0, 0))],
        out_specs=pl.BlockSpec((nc, gb * ns), lambda s: (0, s)),
        out_shape=jax.ShapeDtypeStruct((nc, G * ns), F32),
        compiler_params=pltpu.CompilerParams(dimension_semantics=("parallel",)),
        name="ssm_state",
    )(ut, gm)


def _ssm_scan_kernel(are_ref, aim_ref, ldt_ref, s_ref, o_ref, x_ref, xs_ref, *, T, n0, seq_starts, seq_ends):
    d = pl.program_id(0)
    j = pl.program_id(1)
    nblk = pl.num_programs(1)
    blk = j + d * (nblk - 1 - 2 * j)
    P = SSM_STATE

    is_start = functools.reduce(jnp.logical_or, [blk == s for s in seq_starts])
    is_end = functools.reduce(jnp.logical_or, [blk == e for e in seq_ends])
    reset = jnp.where(d == 0, is_start, is_end)

    @pl.when(reset)
    def _():
        x_ref[...] = jnp.zeros_like(x_ref)
        xs_ref[...] = jnp.zeros_like(xs_ref)

    dt = jnp.exp(ldt_ref[...])
    zr = are_ref[...] * dt
    zi = aim_ref[...] * dt
    mag = jnp.exp(float(T) * zr)
    mr = mag * jnp.cos(float(T) * zi)
    mi = mag * jnp.sin(float(T) * zi)
    lane = lax.broadcasted_iota(jnp.int32, mr.shape, 1)
    m2 = jnp.where(lane < P, -mi, mi)
    m2s = -m2

    def body(k, carry):
        x, xs = carry
        row = k + d * (n0 - 1 - 2 * k)
        loc = s_ref[row]
        o_ref[row] = x.astype(o_ref.dtype)
        locs = pltpu.roll(loc, P, 1)
        return x * mr + xs * m2 + loc, xs * mr + x * m2s + locs

    x, xs = lax.fori_loop(0, n0, body, (x_ref[...], xs_ref[...]), unroll=4)
    x_ref[...] = x
    xs_ref[...] = xs


def _ssm_scan(s_loc, a_re, a_im, log_dt, *, T, n_unit_seqs):
    G, P = SSM_GROUPS, SSM_STATE
    nc = s_loc.shape[0]
    n0 = SEQ_BLOCK // T
    nblk = nc // n0
    seq_starts = tuple(range(n_unit_seqs + 1))
    seq_ends = tuple(range(n_unit_seqs)) + (nblk - 1,)
    dup = lambda x: jnp.concatenate([x, x], axis=-1)
    are2, aim2 = dup(a_re), dup(a_im)
    ldt2 = jnp.broadcast_to(log_dt[:, :, None], (2, G, 2 * P))

    def blk_map(d, j):
        return (j + d * (nblk - 1 - 2 * j), 0, d)

    par = pl.BlockSpec((None, G, 2 * P), lambda d, j: (d, 0, 0))
    return pl.pallas_call(
        functools.partial(_ssm_scan_kernel, T=T, n0=n0, seq_starts=seq_starts, seq_ends=seq_ends),
        grid=(2, nblk),
        in_specs=[par, par, par, pl.BlockSpec((n0, G, 2 * P), blk_map)],
        out_specs=pl.BlockSpec((n0, G, 2 * P), blk_map),
        out_shape=jax.ShapeDtypeStruct((nc, G, 4 * P), BF16),
        scratch_shapes=[pltpu.VMEM((G, 2 * P), F32), pltpu.VMEM((G, 2 * P), F32)],
        compiler_params=pltpu.CompilerParams(dimension_semantics=("arbitrary", "arbitrary")),
        name="ssm_scan",
    )(are2, aim2, ldt2, s_loc)


def _ssm_out_kernel(u_ref, mt_ref, s_ref, cs_ref, y_ref, *, gb, ns):
    nc = u_ref.shape[-1]
    for gi in range(gb):
        u = u_ref[gi].reshape(-1, nc)
        y = _dot(mt_ref[gi], u) + _dot_nt(cs_ref[gi], s_ref[:, gi * ns:(gi + 1) * ns])
        y_ref[gi] = _gelu_tanh(y).astype(y_ref.dtype).reshape(y_ref.shape[1:])


def _ssm_out(ut, mt, s_in, cs):
    G, T, C, nc = ut.shape
    ns = cs.shape[2]
    gb = SSM_GROUPS_PER_STEP
    return pl.pallas_call(
        functools.partial(_ssm_out_kernel, gb=gb, ns=ns),
        grid=(G // gb,),
        in_specs=[pl.BlockSpec((gb, T, C, nc), lambda s: (s, 0, 0, 0)),
                  pl.BlockSpec((gb, T * C, T * C), lambda s: (s, 0, 0)),
                  pl.BlockSpec((nc, gb * ns), lambda s: (0, s)),
                  pl.BlockSpec((gb, T * C, ns), lambda s: (s, 0, 0))],
        out_specs=pl.BlockSpec((gb, T, C, nc), lambda s: (s, 0, 0, 0)),
        out_shape=jax.ShapeDtypeStruct((G, T, C, nc), BF16),
        compiler_params=pltpu.CompilerParams(dimension_semantics=("parallel",)),
        name="ssm_out",
    )(ut, mt, s_in, cs)


def _ssm_tok_kernel(y_ref, o_ref, stage_ref):
    T, ncb = y_ref.shape[1], y_ref.shape[-1]
    R = stage_ref.shape[2]
    row = lax.broadcasted_iota(jnp.int32, (R * ncb, R * ncb), 0)
    col = lax.broadcasted_iota(jnp.int32, (R * ncb, R * ncb), 1)
    sel = ((row // R == col % ncb) & (row % R == col // ncb)).astype(BF16)
    for k in range(T // R):
        ycat = jnp.concatenate([y_ref[:, k * R + rr].reshape(-1, ncb) for rr in range(R)], axis=1)
        tok = _dot_nt(sel, ycat)
        stage_ref[:, k] = tok.reshape(ncb, R, tok.shape[-1])
    o_ref[...] = stage_ref[...].reshape(o_ref.shape).astype(o_ref.dtype)


def _ssm_tok(yt):
    G, T, C, nc = yt.shape
    ncb = SSM_TOK_CHUNKS
    est = 2 * G * T * C * ncb * 2 + ncb * T * G * C * 4 + 2 * ncb * T * G * C * 2
    return pl.pallas_call(
        _ssm_tok_kernel,
        grid=(nc // ncb,),
        in_specs=[pl.BlockSpec((G, T, C, ncb), lambda s: (0, 0, 0, s))],
        out_specs=pl.BlockSpec((ncb * T, G * C), lambda s: (s, 0)),
        out_shape=jax.ShapeDtypeStruct((nc * T, G * C), BF16),
        scratch_shapes=[pltpu.VMEM((ncb, T // V7X_SUBLANES, V7X_SUBLANES, G * C), F32)],
        compiler_params=pltpu.CompilerParams(
            dimension_semantics=("parallel",), vmem_limit_bytes=_vmem_limit(est)),
        name="ssm_tok",
    )(yt)


def _s5_mixer_gelu(xb, wt, tables, a_re, a_im, log_dt, *, n_unit_seqs):
    T, G, P = SSM_CHUNK, SSM_GROUPS, SSM_STATE
    nc = xb.shape[0] // T
    mt, gm, cs = tables
    ut = _ssm_in(xb.reshape(nc, T * xb.shape[1]), wt, T=T)
    s_loc = _ssm_state(ut, gm)
    s_in = _ssm_scan(s_loc.reshape(nc, G, 4 * P), a_re, a_im, log_dt, T=T, n_unit_seqs=n_unit_seqs)
    yt = _ssm_out(ut, mt, s_in.reshape(nc, G * 4 * P), cs)
    return _ssm_tok(yt)


def _kv_kernel(m_ref, g_ref, b_ref, w_ref, o_ref, mb_ref):
    @pl.when(pl.program_id(0) == 0)
    def _():
        mb_ref[...] = _layernorm_rows(m_ref[...], g_ref[...], b_ref[...]).astype(BF16)

    o_ref[...] = _dot(mb_ref[...], w_ref[...]).astype(o_ref.dtype)


def _kv_proj(mem, ln_g, ln_b, w_kv):
    n, d = mem.shape
    m = w_kv.shape[1]
    tn = KV_TN
    est = 2 * n * d * 4 + n * d * 2 + 2 * d * tn * 2 + 2 * n * tn * 2 + 3 * n * d * 4
    return pl.pallas_call(
        _kv_kernel,
        grid=(m // tn,),
        in_specs=[pl.BlockSpec((n, d), lambda j: (0, 0)),
                  pl.BlockSpec((1, d), lambda j: (0, 0)),
                  pl.BlockSpec((1, d), lambda j: (0, 0)),
                  pl.BlockSpec((d, tn), lambda j: (0, j))],
        out_specs=pl.BlockSpec((n, tn), lambda j: (0, j)),
        out_shape=jax.ShapeDtypeStruct((n, m), BF16),
        scratch_shapes=[pltpu.VMEM((n, d), BF16)],
        compiler_params=pltpu.CompilerParams(
            dimension_semantics=("arbitrary",), vmem_limit_bytes=_vmem_limit(est)),
        name="kv_proj",
    )(mem, ln_g, ln_b, w_kv)


def _mixer_kernel(ya_ref, ub_ref, ob_ref, g0_ref, g1_ref, g2_ref,
                  wa_ref, wb_ref, wgp_ref, wap_ref, o_ref):
    ya = ya_ref[...]
    y_a = _dot(ya, wa_ref[...]) * _sigmoid(_dot(ya, wb_ref[...]))
    y_b = _dot(ub_ref[...], wgp_ref[...])
    y_c = _dot(ob_ref[...], wap_ref[...])
    merged = (_sigmoid(g0_ref[...].astype(F32)) * y_a
              + _sigmoid(g1_ref[...].astype(F32)) * y_b
              + _sigmoid(g2_ref[...].astype(F32)) * y_c)
    o_ref[...] = merged.astype(o_ref.dtype)


def _mixer(ya, ub, ob, proj, w_glu, w_gproj, w_aproj):
    n, wid = ya.shape
    d = D_MODEL
    tm, tn = MIX_TM, MIX_TN
    nn = d // tn
    gate0 = (proj.shape[1] - 3 * d) // tn
    gate_spec = lambda br: pl.BlockSpec((tm, tn), lambda i, j: (i, gate0 + br * nn + j))
    act_spec = pl.BlockSpec((tm, wid), lambda i, j: (i, 0))
    est = (2 * 3 * tm * wid * 2 + 2 * 3 * tm * tn * 2 + 2 * 4 * wid * tn * 2 + 2 * tm * tn * 2
           + 8 * tm * tn * 4)
    return pl.pallas_call(
        _mixer_kernel,
        grid=(n // tm, nn),
        in_specs=[
            act_spec, act_spec, act_spec,
            gate_spec(0), gate_spec(1), gate_spec(2),
            pl.BlockSpec((wid, tn), lambda i, j: (0, j)),
            pl.BlockSpec((wid, tn), lambda i, j: (0, nn + j)),
            pl.BlockSpec((wid, tn), lambda i, j: (0, j)),
            pl.BlockSpec((wid, tn), lambda i, j: (0, j)),
        ],
        out_specs=pl.BlockSpec((tm, tn), lambda i, j: (i, j)),
        out_shape=jax.ShapeDtypeStruct((n, d), BF16),
        compiler_params=pltpu.CompilerParams(
            dimension_semantics=("parallel", "parallel"),
            vmem_limit_bytes=_vmem_limit(est)),
        name="mixer",
    )(ya, ub, ob, proj, proj, proj, w_glu, w_glu, w_gproj, w_aproj)


def _outproj_ln_kernel(m_ref, w_ref, x_ref, g_ref, b_ref, o_ref):
    y = ALPHA * x_ref[...] + _dot(m_ref[...], w_ref[...])
    o_ref[...] = _layernorm_rows(y, g_ref[...], b_ref[...])


def _outproj_ln(merged, w_out, x, ln_g, ln_b):
    n, d = x.shape
    tm = OUT_TM
    est = 2 * tm * d * 2 + 2 * d * d * 2 + 4 * tm * d * 4 + 2 * tm * d * 4
    return pl.pallas_call(
        _outproj_ln_kernel,
        grid=(n // tm,),
        in_specs=[pl.BlockSpec((tm, d), lambda i: (i, 0)),
                  pl.BlockSpec((d, d), lambda i: (0, 0)),
                  pl.BlockSpec((tm, d), lambda i: (i, 0)),
                  pl.BlockSpec((1, d), lambda i: (0, 0)),
                  pl.BlockSpec((1, d), lambda i: (0, 0))],
        out_specs=pl.BlockSpec((tm, d), lambda i: (i, 0)),
        out_shape=jax.ShapeDtypeStruct((n, d), F32),
        compiler_params=pltpu.CompilerParams(
            dimension_semantics=("parallel",), vmem_limit_bytes=_vmem_limit(est)),
        name="outproj_ln",
    )(merged, w_out, x, ln_g, ln_b)


def _layer(groups, ffn1_w_gu, ffn1_w_down, ln1_g, ln1_b, w_in,
           ssm_a_re, ssm_a_im, ssm_log_dt, ssm_b_re, ssm_b_im, ssm_c_re, ssm_c_im, ssm_d, ssm_w_glu,
           gmlp_ln_g, gmlp_ln_b, gmlp_w_s, gmlp_b_s, gmlp_w_proj,
           mem_ln_g, mem_ln_b, attn_w_kv, attn_w_proj,
           w_out, ln2_g, ln2_b, ffn2_w_gu, ffn2_w_down, ln3_g, ln3_b):
    row = lambda v: v.reshape(1, -1)
    whole = lambda a: ((0, a.shape[1]),)
    gate_up = ((0, D_FF), (D_FF, 2 * D_FF))
    tables, early = _ssm_tables(
        ssm_a_re, ssm_a_im, ssm_log_dt, ssm_b_re, ssm_b_im, ssm_c_re, ssm_c_im, ssm_d,
        [(ffn1_w_gu, 32, gate_up), (ffn1_w_down, GMLP_CHUNK, whole(ffn1_w_down)),
         (w_in, 32, whole(w_in))], T=SSM_CHUNK)
    ffn1 = _ffn_weight_set(early[0], early[1], early[2])
    w_in_b = early[3]
    wt_ssm = _ssm_weight_t(w_in)
    w_kv, w_s = attn_w_kv.astype(BF16), gmlp_w_s.astype(BF16)

    jobs = [("ffn2_gu", ffn2_w_gu, 32, gate_up),
            ("ffn2_down", ffn2_w_down, GMLP_CHUNK, whole(ffn2_w_down)),
            ("glu", ssm_w_glu, 16, whole(ssm_w_glu)),
            ("out", w_out, 32, whole(w_out)),
            ("gproj", gmlp_w_proj, 16, whole(gmlp_w_proj)),
            ("aproj", attn_w_proj, 16, whole(attn_w_proj))]
    bf16_w = {}
    stage1 = []
    for gi, (x, mem, n_unit_seqs) in enumerate(groups):
        mine = jobs[gi::len(groups)]
        x1 = _ffn_ln(x, ffn1, row(ln1_g), row(ln1_b))
        proj, x1b, conv = _in_proj(x1, w_in_b, row(gmlp_ln_g), row(gmlp_ln_b),
                                   [job[1:] for job in mine])
        conv = iter(conv)
        for name, _, _, splits in mine:
            bf16_w[name] = [next(conv) for _ in splits]
        stage1.append((x1, proj, x1b))
    ffn2 = _ffn_weight_set(bf16_w["ffn2_gu"][0], bf16_w["ffn2_gu"][1], bf16_w["ffn2_down"][0])
    w_glu, w_out_b = bf16_w["glu"][0], bf16_w["out"][0]
    w_gproj, w_aproj = bf16_w["gproj"][0], bf16_w["aproj"][0]

    outs = []
    for (x, mem, n_unit_seqs), (x1, proj, x1b) in zip(groups, stage1):
        kv = _kv_proj(mem, row(mem_ln_g), row(mem_ln_b), w_kv)
        ub, ob = _branches(proj, kv, w_s, gmlp_b_s)
        ya = _s5_mixer_gelu(x1b, wt_ssm, tables, ssm_a_re, ssm_a_im, ssm_log_dt,
                            n_unit_seqs=n_unit_seqs)
        merged = _mixer(ya, ub, ob, proj, w_glu, w_gproj, w_aproj)
        x2 = _outproj_ln(merged, w_out_b, x1, row(ln2_g), row(ln2_b))
        outs.append(_ffn_ln(x2, ffn2, row(ln3_g), row(ln3_b)))
    return outs


def kernel(x_prompt, x_sample, mem_prompt, mem_sample, ffn1_w_gu, ffn1_w_down, ln1_g, ln1_b, w_in,
           ssm_a_re, ssm_a_im, ssm_log_dt, ssm_b_re, ssm_b_im, ssm_c_re, ssm_c_im, ssm_d, ssm_w_glu,
           gmlp_ln_g, gmlp_ln_b, gmlp_w_s, gmlp_b_s, gmlp_w_proj, mem_ln_g, mem_ln_b, attn_w_kv,
           attn_w_proj, w_out, ln2_g, ln2_b, ffn2_w_gu, ffn2_w_down, ln3_g, ln3_b):
    d = x_prompt.shape[-1]
    assert x_prompt.shape[1] == SEQ_BLOCK and x_sample.shape[0] == 1
    assert x_sample.shape[1] % SEQ_BLOCK == 0
    xs = [x_prompt.reshape(-1, d), x_sample.reshape(-1, d)]
    mems = [mem_prompt.reshape(-1, d), mem_sample.reshape(-1, d)]
    n_unit = [x_prompt.shape[0], 0]
    for l in range(DEPTH):
        xs = _layer(list(zip(xs, mems, n_unit)),
                    ffn1_w_gu[l], ffn1_w_down[l], ln1_g[l], ln1_b[l], w_in[l],
                    ssm_a_re[l], ssm_a_im[l], ssm_log_dt[l], ssm_b_re[l], ssm_b_im[l],
                    ssm_c_re[l], ssm_c_im[l], ssm_d[l], ssm_w_glu[l],
                    gmlp_ln_g[l], gmlp_ln_b[l], gmlp_w_s[l], gmlp_b_s[l], gmlp_w_proj[l],
                    mem_ln_g[l], mem_ln_b[l], attn_w_kv[l], attn_w_proj[l],
                    w_out[l], ln2_g[l], ln2_b[l], ffn2_w_gu[l], ffn2_w_down[l], ln3_g[l], ln3_b[l])
    return (xs[0].reshape(x_prompt.shape), xs[1].reshape(x_sample.shape))
```

```python
import functools
import math

import jax
import jax.numpy as jnp
from jax import lax
from jax.experimental import pallas as pl
from jax.experimental.pallas import tpu as pltpu

F32 = jnp.float32
BF16 = jnp.bfloat16

D_MODEL = 2048
DEPTH = 1
SEQ_BLOCK = 4096
N_MEM = 256
SSM_WIDTH = D_MODEL // 2
SSM_GROUP = 16
SSM_GROUPS = SSM_WIDTH // SSM_GROUP
SSM_STATE = 64
GMLP_WIDTH = D_MODEL // 2
GMLP_CHUNK = 128
GMLP_HEADS = 8
GMLP_HEAD_DIM = GMLP_WIDTH // GMLP_HEADS
ATTN_HEADS = 4
ATTN_HEAD_DIM = D_MODEL // 8
ATTN_WIDTH = ATTN_HEADS * ATTN_HEAD_DIM
D_FF = 5504
ALPHA = (2.0 * DEPTH) ** 0.25
LN_EPS = 1e-5

V7X_LANES = 128
V7X_SUBLANES = 8
V7X_VMEM_BYTES = 64 * 1024 * 1024

SSM_CHUNK = 32
SSM_GROUPS_PER_STEP = 16
SSM_TABLE_GROUPS = 1
SSM_IN_POSITIONS = 4
SSM_TOK_CHUNKS = 128
FFN_TM = 512
FFN_TF = 1024
PROJ_TM = 1024
PROJ_TN = 1024
BRANCH_TM = 1024
MIX_TM = 1024
MIX_TN = 512
OUT_TM = 512
KV_TN = 512


def _vmem_limit(nbytes):
    return int(min(nbytes + (16 << 20), V7X_VMEM_BYTES - (4 << 20)))


def _layernorm_rows(y, g, b):
    mu = jnp.mean(y, axis=-1, keepdims=True)
    yc = y - mu
    var = jnp.mean(yc * yc, axis=-1, keepdims=True)
    return yc * lax.rsqrt(var + LN_EPS) * g + b


def _gelu_tanh(x):
    c = math.sqrt(2.0 / math.pi)
    return 0.5 * x * (1.0 + jnp.tanh(c * (x + 0.044715 * (x * x * x))))


def _sigmoid(x):
    return 1.0 / (1.0 + jnp.exp(-x))


def _dot(a, b):
    return jnp.dot(a, b, preferred_element_type=F32)


def _dot_nt(a, b):
    return lax.dot_general(a, b, (((1,), (1,)), ((), ())), preferred_element_type=F32)


def _swiglu_down(xb, wg, wu, wd):
    gate = _dot(xb, wg)
    up = _dot(xb, wu)
    act = (gate * _sigmoid(gate) * up).astype(BF16)
    return _dot(act, wd)


def _ffn_ln_kernel(x_ref, wg_ref, wu_ref, wd_ref, wgt_ref, wut_ref, wdt_ref, g_ref, b_ref,
                   *refs, casts):
    n_src = len(casts)
    src_refs = refs[:n_src]
    o_ref = refs[n_src]
    dst_refs = refs[n_src + 1:-1]
    xb_ref = refs[-1]
    j = pl.program_id(1)

    @pl.when(j == 0)
    def _():
        xb_ref[...] = x_ref[...].astype(BF16)
        o_ref[...] = jnp.zeros_like(o_ref)

    _cast_rows(src_refs, dst_refs, casts)
    o_ref[...] += _swiglu_down(xb_ref[...], wg_ref[...], wu_ref[...], wd_ref[...])

    @pl.when(j == pl.num_programs(1) - 1)
    def _():
        acc = o_ref[...] + _swiglu_down(xb_ref[...], wgt_ref[...], wut_ref[...], wdt_ref[...])
        y = ALPHA * x_ref[...] + 0.5 * acc
        o_ref[...] = _layernorm_rows(y, g_ref[...], b_ref[...])


def _ffn_ln(x, weights, ln_g, ln_b, cast_jobs=()):
    wg, wu, wd, wgt, wut, wdt = weights
    n, d = x.shape
    tm, tf = FFN_TM, FFN_TF
    nf = wd.shape[0] // tf
    ft = wdt.shape[0]
    once = dict(pipeline_mode=pl.Buffered(1))
    src_specs, dst_specs, dst_shapes, casts, cast_bytes = _cast_job_specs(
        cast_jobs, (n // tm) * nf, lambda i, j: i * nf + j)
    est = (2 * tm * d * 4 + 2 * tm * d * 4 + tm * d * 2
           + 2 * 3 * (d * tf * 2) + 3 * (d * ft * 2) + 3 * tm * tf * 4 + cast_bytes)
    res = pl.pallas_call(
        functools.partial(_ffn_ln_kernel, casts=casts),
        grid=(n // tm, nf),
        in_specs=[
            pl.BlockSpec((tm, d), lambda i, j: (i, 0)),
            pl.BlockSpec((d, tf), lambda i, j: (0, j)),
            pl.BlockSpec((d, tf), lambda i, j: (0, j)),
            pl.BlockSpec((tf, d), lambda i, j: (j, 0)),
            pl.BlockSpec((d, ft), lambda i, j: (0, 0), **once),
            pl.BlockSpec((d, ft), lambda i, j: (0, 0), **once),
            pl.BlockSpec((ft, d), lambda i, j: (0, 0), **once),
            pl.BlockSpec((1, d), lambda i, j: (0, 0)),
            pl.BlockSpec((1, d), lambda i, j: (0, 0)),
        ] + src_specs,
        out_specs=[pl.BlockSpec((tm, d), lambda i, j: (i, 0))] + dst_specs,
        out_shape=[jax.ShapeDtypeStruct((n, d), F32)] + dst_shapes,
        scratch_shapes=[pltpu.VMEM((tm, d), BF16)],
        compiler_params=pltpu.CompilerParams(
            dimension_semantics=("arbitrary", "arbitrary"),
            vmem_limit_bytes=_vmem_limit(est)),
        name="ffn_ln",
    )(x, wg, wu, wd, wgt, wut, wdt, ln_g, ln_b, *[job[0] for job in cast_jobs])
    return res[0], res[1:]


def _ffn_weight_set(wg, wu, wd):
    full = (D_FF // FFN_TF) * FFN_TF
    return wg, wu, wd, wg[:, full:], wu[:, full:], wd[full:]


def _cast_job_specs(cast_jobs, steps, step_index):
    src_specs, dst_specs, dst_shapes, casts, nbytes = [], [], [], [], 0
    for src, rows, splits in cast_jobs:
        nblk = src.shape[0] // rows
        assert src.shape[0] % rows == 0 and nblk <= steps
        blk_map = lambda *idx, nblk=nblk: (jnp.minimum(step_index(*idx), nblk - 1), 0)
        src_specs.append(pl.BlockSpec((rows, src.shape[1]), blk_map))
        for lo, hi in splits:
            dst_specs.append(pl.BlockSpec((rows, hi - lo), blk_map))
            dst_shapes.append(jax.ShapeDtypeStruct((src.shape[0], hi - lo), BF16))
        casts.append(tuple(splits))
        nbytes += 2 * rows * src.shape[1] * 6
    return src_specs, dst_specs, dst_shapes, tuple(casts), nbytes


def _cast_rows(src_refs, dst_refs, casts):
    dsts = iter(dst_refs)
    for src, splits in zip(src_refs, casts):
        for lo, hi in splits:
            next(dsts)[...] = src[:, lo:hi].astype(BF16)


def _in_proj_kernel(x_ref, w_ref, lng_ref, lnb_ref, *refs, casts):
    n_src = len(casts)
    src_refs = refs[:n_src]
    o_ref, xb_ref = refs[n_src:n_src + 2]
    dst_refs = refs[n_src + 2:]
    j = pl.program_id(1)

    @pl.when(j == 0)
    def _():
        xb_ref[...] = x_ref[...].astype(BF16)

    def convert_weights():
        _cast_rows(src_refs, dst_refs, casts)

    @pl.when(j == 0)
    def _():
        convert_weights()
        o_ref[...] = _gelu_tanh(_dot(xb_ref[...], w_ref[...])).astype(o_ref.dtype)

    @pl.when(j == 1)
    def _():
        convert_weights()
        v = _gelu_tanh(_dot(xb_ref[...], w_ref[...]))
        o_ref[...] = _layernorm_rows(v, lng_ref[...], lnb_ref[...]).astype(o_ref.dtype)

    @pl.when(j >= 2)
    def _():
        convert_weights()
        o_ref[...] = _dot(xb_ref[...], w_ref[...]).astype(o_ref.dtype)


def _in_proj(x, w, gln_g, gln_b, cast_jobs):
    n, k = x.shape
    tm, tn = PROJ_TM, PROJ_TN
    assert tn == GMLP_WIDTH
    skip = SSM_WIDTH // tn
    m = w.shape[1] - SSM_WIDTH
    ncol = m // tn
    est = 2 * tm * k * 4 + 2 * tm * k * 2 + 2 * k * tn * 2 + 2 * tm * tn * 2 + 3 * tm * tn * 4
    src_specs, dst_specs, dst_shapes, casts, cast_bytes = _cast_job_specs(
        cast_jobs, (n // tm) * ncol, lambda i, j: i * ncol + j)
    est += cast_bytes
    res = pl.pallas_call(
        functools.partial(_in_proj_kernel, casts=casts),
        grid=(n // tm, ncol),
        in_specs=[pl.BlockSpec((tm, k), lambda i, j: (i, 0)),
                  pl.BlockSpec((k, tn), lambda i, j: (0, skip + j)),
                  pl.BlockSpec((1, tn), lambda i, j: (0, 0)),
                  pl.BlockSpec((1, tn), lambda i, j: (0, 0))] + src_specs,
        out_specs=[pl.BlockSpec((tm, tn), lambda i, j: (i, j)),
                   pl.BlockSpec((tm, k), lambda i, j: (i, 0))] + dst_specs,
        out_shape=[jax.ShapeDtypeStruct((n, m), BF16),
                   jax.ShapeDtypeStruct((n, k), BF16)] + dst_shapes,
        compiler_params=pltpu.CompilerParams(
            dimension_semantics=("arbitrary", "arbitrary"),
            vmem_limit_bytes=_vmem_limit(est)),
        name="in_proj",
    )(x, w, gln_g, gln_b, *[job[0] for job in cast_jobs])
    return res[0], res[1], res[2:]


def _branches_kernel(u_ref, v_ref, q_ref, kv_ref, ws_ref, bs_ref, ub_ref, ob_ref, *, tm):
    hd = GMLP_HEAD_DIM

    def chunk_body(ci, carry):
        rows = pl.ds(pl.multiple_of(ci * GMLP_CHUNK, GMLP_CHUNK), GMLP_CHUNK)
        for h in range(GMLP_HEADS):
            cols = slice(h * hd, (h + 1) * hd)
            mixed = _dot(ws_ref[h], v_ref[rows, cols]) + bs_ref[:, h:h + 1]
            ub_ref[rows, cols] = (u_ref[rows, cols].astype(F32) * mixed).astype(BF16)
        return carry

    lax.fori_loop(0, tm // GMLP_CHUNK, chunk_body, 0)

    ad = ATTN_HEAD_DIM
    scale = ad ** -0.5
    for h in range(ATTN_HEADS):
        cols = slice(h * ad, (h + 1) * ad)
        vcols = slice(ATTN_WIDTH + h * ad, ATTN_WIDTH + (h + 1) * ad)
        s = _dot_nt(q_ref[:, cols], kv_ref[:, cols]) * scale
        p = jnp.exp(s - jnp.max(s, axis=-1, keepdims=True))
        p = p * (1.0 / jnp.sum(p, axis=-1, keepdims=True))
        ob_ref[:, cols] = _dot(p.astype(BF16), kv_ref[:, vcols]).astype(BF16)


def _branches(proj, kv, w_s, b_s):
    n = proj.shape[0]
    tm = BRANCH_TM
    wid = GMLP_WIDTH
    assert wid == ATTN_WIDTH
    blocks_per_seq = SEQ_BLOCK // tm
    n_mem_batches = kv.shape[0] // N_MEM
    col = lambda c: pl.BlockSpec((tm, wid), lambda i: (i, c))
    full2 = lambda a: pl.BlockSpec(a.shape, lambda i: (0, 0))
    out = pl.BlockSpec((tm, wid), lambda i: (i, 0))
    return pl.pallas_call(
        functools.partial(_branches_kernel, tm=tm),
        grid=(n // tm,),
        in_specs=[col(0), col(1), col(2),
                  pl.BlockSpec((N_MEM, 2 * ATTN_WIDTH),
                               lambda i: (jnp.minimum(i // blocks_per_seq, n_mem_batches - 1), 0)),
                  pl.BlockSpec(w_s.shape, lambda i: (0, 0, 0)),
                  full2(b_s)],
        out_specs=[out, out],
        out_shape=[jax.ShapeDtypeStruct((n, wid), BF16), jax.ShapeDtypeStruct((n, wid), BF16)],
        compiler_params=pltpu.CompilerParams(dimension_semantics=("parallel",)),
        name="branches",
    )(proj, proj, proj, kv, w_s, b_s)


def _ssm_weight_t_kernel(w_ref, o_ref):
    o_ref[...] = w_ref[...].T.astype(o_ref.dtype)


def _ssm_weight_t(w_in):
    d = w_in.shape[0]
    tn = 2 * V7X_LANES
    return pl.pallas_call(
        _ssm_weight_t_kernel,
        grid=(SSM_WIDTH // tn,),
        in_specs=[pl.BlockSpec((d, tn), lambda j: (0, j))],
        out_specs=pl.BlockSpec((tn, d), lambda j: (j, 0)),
        out_shape=jax.ShapeDtypeStruct((SSM_WIDTH, d), BF16),
        compiler_params=pltpu.CompilerParams(dimension_semantics=("parallel",)),
        name="ssm_weight_t",
    )(w_in)


def _ssm_in_kernel(x_ref, wt_ref, o_ref, *, rb):
    d = wt_ref.shape[1]
    for k in range(rb):
        ut = _dot_nt(wt_ref[...], x_ref[:, k * d:(k + 1) * d])
        o_ref[:, k] = ut.astype(BF16).reshape(o_ref.shape[0], o_ref.shape[2], o_ref.shape[3])


def _ssm_in(xc, wt, *, T):
    nc = xc.shape[0]
    d = xc.shape[1] // T
    G, C = SSM_GROUPS, SSM_GROUP
    rb = SSM_IN_POSITIONS
    est = 2 * nc * rb * d * 2 + 2 * G * C * d * 2 + 2 * rb * G * C * nc * 2 + 2 * G * C * nc * 4
    return pl.pallas_call(
        functools.partial(_ssm_in_kernel, rb=rb),
        grid=(T // rb,),
        in_specs=[pl.BlockSpec((nc, rb * d), lambda r: (0, r)),
                  pl.BlockSpec((G * C, d), lambda r: (0, 0))],
        out_specs=pl.BlockSpec((G, rb, C, nc), lambda r: (0, r, 0, 0)),
        out_shape=jax.ShapeDtypeStruct((G, T, C, nc), BF16),
        compiler_params=pltpu.CompilerParams(
            dimension_semantics=("parallel",), vmem_limit_bytes=_vmem_limit(est)),
        name="ssm_in",
    )(xc, wt)


def _cmul(x, y):
    return x[0] * y[0] - x[1] * y[1], x[0] * y[1] + x[1] * y[0]


def _split_hi_lo(x):
    hi = x.astype(BF16)
    lo = (x - hi.astype(F32)).astype(BF16)
    return hi, lo


def _dot_hi_lo(a, b):
    ah, al = _split_hi_lo(a)
    bh, bl = _split_hi_lo(b)
    return _dot(ah, bh) + _dot(al, bh) + _dot(ah, bl)


N_TABLE_INPUTS = 2
N_TABLE_OUTPUTS = 3


def _ssm_tables_kernel(*refs, T, gb, casts):
    n_src = len(casts)
    n_dst = sum(len(splits) for splits in casts)
    ins = refs[:N_TABLE_INPUTS]
    srcs = refs[N_TABLE_INPUTS:N_TABLE_INPUTS + n_src]
    outs = refs[N_TABLE_INPUTS + n_src:N_TABLE_INPUTS + n_src + N_TABLE_OUTPUTS]
    dsts = refs[N_TABLE_INPUTS + n_src + N_TABLE_OUTPUTS:][:n_dst]
    q_ref = refs[-1]
    _cast_rows(srcs, dsts, casts)
    for gi in range(gb):
        _ssm_tables_group(*[r.at[gi] for r in ins + outs + (q_ref,)], T=T)


def _ssm_tables_group(col_ref, row_ref, mt_ref, gm_ref, cs_ref, q_ref, *, T):
    P, C = SSM_STATE, SSM_GROUP
    rpt = V7X_LANES // C
    nt = T // rpt
    wide = 2 * T * C

    lane = lax.broadcasted_iota(jnp.int32, (1, V7X_LANES), 1)
    rr = lax.shift_right_logical(lane, 4)
    expand = (lax.broadcasted_iota(jnp.int32, (C, V7X_LANES), 0)
              == (lax.broadcasted_iota(jnp.int32, (C, V7X_LANES), 1) & (C - 1))).astype(F32)

    pc = col_ref[0:P, C:C + 8]

    lane4 = lax.broadcasted_iota(jnp.int32, (1, 4 * P), 1)
    is_re = (lax.shift_right_logical(lane4, 6) & 1) == 0
    is_f = lane4 < 2 * P
    pr = row_ref[0:8]
    dt4 = jnp.exp(pr[2:3])
    zr4, zi4 = pr[0:1] * dt4, pr[1:2] * dt4

    kp = ((T + 1 + 7) // 8) * 8
    krow = lax.broadcasted_iota(jnp.int32, (kp, 4 * P), 0).astype(F32)
    mag = jnp.exp(krow * zr4)
    pw_r, pw_i = mag * jnp.cos(krow * zi4), mag * jnp.sin(krow * zi4)

    def states_on_rows(tab, lo):
        slab = tab[:, lo:lo + V7X_LANES]
        padded = jnp.concatenate([slab, jnp.zeros((V7X_LANES - kp, V7X_LANES), F32)], axis=0)
        return padded.T[0:P]

    pt_f = states_on_rows(pw_r, 0), states_on_rows(pw_i, 0)
    pt_b = states_on_rows(pw_r, 2 * P), states_on_rows(pw_i, 2 * P)

    def stair(pt, descending):
        re = jnp.zeros((P, V7X_LANES), F32)
        im = re
        for k in range(rpt):
            m = rpt - 1 - k if descending else k
            re = jnp.where(rr == k, pt[0][:, m:m + 1], re)
            im = jnp.where(rr == k, pt[1][:, m:m + 1], im)
        return re, im

    def zoh(d, pt):
        are = pc[:, d:d + 1]
        aim = pc[:, 2 + d:3 + d]
        nr = pt[0][:, 1:2] - 1.0
        ni = pt[1][:, 1:2]
        den = are * are + aim * aim
        return (nr * are + ni * aim) / den, (ni * are - nr * aim) / den

    def col(pt, m):
        return pt[0][:, m:m + 1], pt[1][:, m:m + 1]

    dsc_f = stair(pt_f, True)
    asc_b = stair(pt_b, False)
    btile = jnp.dot(col_ref[:, 0:C], expand, precision=lax.Precision.HIGHEST,
                    preferred_element_type=F32)
    bt = (btile[0:P], btile[P:2 * P])
    bb_f = _cmul(zoh(0, pt_f), bt)
    bd_f = _cmul(dsc_f, bb_f)
    ba_b = _cmul(asc_b, _cmul(zoh(1, pt_b), bt))

    for j in range(nt):
        cols = slice(j * V7X_LANES, (j + 1) * V7X_LANES)
        xf = _cmul(bd_f, col(pt_f, T - rpt - rpt * j))
        xb = _cmul(ba_b, col(pt_b, rpt * j))
        gm_ref[0:P, cols] = xf[0].astype(BF16)
        gm_ref[P:2 * P, cols] = xf[1].astype(BF16)
        gm_ref[2 * P:3 * P, cols] = xb[0].astype(BF16)
        gm_ref[3 * P:4 * P, cols] = xb[1].astype(BF16)

    zeros = jnp.zeros((2 * P, V7X_LANES), F32)
    for j in range(2 * nt):
        cols = slice(j * V7X_LANES, (j + 1) * V7X_LANES)
        if j < nt:
            qf = _cmul(bd_f, col(pt_f, T - rpt * j - (rpt - 1)))
            q_ref[0:P, cols] = qf[0]
            q_ref[P:2 * P, cols] = qf[1]
            q_ref[2 * P:4 * P, cols] = zeros
        else:
            qb = _cmul(ba_b, col(pt_b, rpt * j - T))
            q_ref[2 * P:3 * P, cols] = qb[0]
            q_ref[3 * P:4 * P, cols] = qb[1]
            if j == nt:
                center = rr == 0
                q_ref[0:P, cols] = jnp.where(center, bb_f[0], 0.0)
                q_ref[P:2 * P, cols] = jnp.where(center, bb_f[1], 0.0)
            else:
                q_ref[0:2 * P, cols] = zeros

    cre = row_ref[8:8 + C]
    cim = row_ref[8 + C:8 + 2 * C]

    for r in range(T):
        prr = jnp.where(is_f, pw_r[r + 1:r + 2], pw_r[T - r:T - r + 1])
        pii = jnp.where(is_f, pw_i[r + 1:r + 2], pw_i[T - r:T - r + 1])
        blk = jnp.where(is_re, cre * prr - cim * pii, -(cre * pii + cim * prr))
        cs_ref[r * C:(r + 1) * C, :] = blk.astype(BF16)

    lhs = jnp.where(is_re, cre, -cim)
    zt = _dot_hi_lo(lhs, q_ref[...])
    lanew = lax.broadcasted_iota(jnp.int32, (C, wide), 1)
    roww = lax.broadcasted_iota(jnp.int32, (C, wide), 0)
    diag = (lax.shift_right_logical(lanew, 4) == T) & ((lanew & (C - 1)) == roww)
    d0 = 8 + 2 * C
    dtile = jnp.concatenate([row_ref[d0 + k:d0 + k + 1] for k in range(wide // (4 * P))], axis=1)
    zt = zt + jnp.where(diag, dtile, 0.0)

    for r in range(T):
        off = (T - r) * C
        shifted = pltpu.roll(zt, (wide - off) % wide, 1)
        mt_ref[r * C:(r + 1) * C, :] = shifted[:, :T * C].astype(BF16)


def _ssm_tables(a_re, a_im, log_dt, b_re, b_im, c_re, c_im, d_skip, cast_jobs, *, T):
    G, P, C = SSM_GROUPS, SSM_STATE, SSM_GROUP
    ldt = jnp.broadcast_to(log_dt[:, :, None], (2, G, P))
    zc = jnp.zeros((G, P), F32)
    pcol = jnp.stack([a_re[0], a_re[1], a_im[0], a_im[1], ldt[0], ldt[1], zc, zc], axis=-1)

    def row4(x):
        return jnp.concatenate([x[0], x[0], x[1], x[1]], axis=-1)

    zr4 = jnp.zeros((G, 4 * P), F32)
    prow = jnp.stack([row4(a_re), row4(a_im), row4(ldt), zr4, zr4, zr4, zr4, zr4], axis=1)
    bcat = jnp.concatenate([b_re, b_im], axis=1)
    dtile = jnp.tile(d_skip, (1, 2 * T)).reshape(G, -1, 4 * P)
    colpack = jnp.concatenate([bcat, jnp.pad(pcol, ((0, 0), (0, P), (0, 0)))], axis=-1)
    rowpack = jnp.concatenate([prow, row4(c_re), row4(c_im), dtile], axis=1)

    tc = T * C
    gb = SSM_TABLE_GROUPS
    blk = lambda *shape: pl.BlockSpec((gb,) + shape, lambda g: (g,) + (0,) * len(shape))
    src_specs, dst_specs, dst_shapes, casts, cast_bytes = _cast_job_specs(
        cast_jobs, G // gb, lambda g: g)
    est = 2 * gb * (2 * tc * tc * 2 + 3 * 4 * P * tc * 4) + cast_bytes
    res = pl.pallas_call(
        functools.partial(_ssm_tables_kernel, T=T, gb=gb, casts=casts),
        grid=(G // gb,),
        in_specs=[blk(*colpack.shape[1:]), blk(*rowpack.shape[1:])] + src_specs,
        out_specs=[blk(tc, tc), blk(4 * P, tc), blk(tc, 4 * P)] + dst_specs,
        out_shape=[
            jax.ShapeDtypeStruct((G, tc, tc), BF16),
            jax.ShapeDtypeStruct((G, 4 * P, tc), BF16),
            jax.ShapeDtypeStruct((G, tc, 4 * P), BF16),
        ] + dst_shapes,
        scratch_shapes=[pltpu.VMEM((gb, 4 * P, 2 * tc), F32)],
        compiler_params=pltpu.CompilerParams(
            dimension_semantics=("arbitrary",), vmem_limit_bytes=_vmem_limit(est)),
        name="ssm_tables",
    )(colpack, rowpack, *[job[0] for job in cast_jobs])
    return tuple(res[:N_TABLE_OUTPUTS]), res[N_TABLE_OUTPUTS:]


def _ssm_state_kernel(u_ref, gm_ref, o_ref, *, gb, ns):
    nc = u_ref.shape[-1]
    for gi in range(gb):
        u = u_ref[gi].reshape(-1, nc)
        o_ref[:, gi * ns:(gi + 1) * ns] = _dot(gm_ref[gi], u).T


def _ssm_state(ut, gm):
    G, T, C, nc = ut.shape
    ns = gm.shape[1]
    gb = SSM_GROUPS_PER_STEP
    return pl.pallas_call(
        functools.partial(_ssm_state_kernel, gb=gb, ns=ns),
        grid=(G // gb,),
        in_specs=[pl.BlockSpec((gb, T, C, nc), lambda s: (s, 0, 0, 0)),
                  pl.BlockSpec((gb, ns, T * C), lambda s: (s, 0, 0))],
        out_specs=pl.BlockSpec((nc, gb * ns), lambda s: (0, s)),
        out_shape=jax.ShapeDtypeStruct((nc, G * ns), F32),
        compiler_params=pltpu.CompilerParams(dimension_semantics=("parallel",)),
        name="ssm_state",
    )(ut, gm)


def _ssm_scan_kernel(are_ref, aim_ref, ldt_ref, s_ref, o_ref, x_ref, xs_ref, *, T, n0, seq_starts, seq_ends):
    d = pl.program_id(0)
    j = pl.program_id(1)
    nblk = pl.num_programs(1)
    blk = j + d * (nblk - 1 - 2 * j)
    P = SSM_STATE

    is_start = functools.reduce(jnp.logical_or, [blk == s for s in seq_starts])
    is_end = functools.reduce(jnp.logical_or, [blk == e for e in seq_ends])
    reset = jnp.where(d == 0, is_start, is_end)

    @pl.when(reset)
    def _():
        x_ref[...] = jnp.zeros_like(x_ref)
        xs_ref[...] = jnp.zeros_like(xs_ref)

    dt = jnp.exp(ldt_ref[...])
    zr = are_ref[...] * dt
    zi = aim_ref[...] * dt
    mag = jnp.exp(float(T) * zr)
    mr = mag * jnp.cos(float(T) * zi)
    mi = mag * jnp.sin(float(T) * zi)
    lane = lax.broadcasted_iota(jnp.int32, mr.shape, 1)
    m2 = jnp.where(lane < P, -mi, mi)
    m2s = -m2

    def body(k, carry):
        x, xs = carry
        row = k + d * (n0 - 1 - 2 * k)
        loc = s_ref[row]
        o_ref[row] = x.astype(o_ref.dtype)
        locs = pltpu.roll(loc, P, 1)
        return x * mr + xs * m2 + loc, xs * mr + x * m2s + locs

    x, xs = lax.fori_loop(0, n0, body, (x_ref[...], xs_ref[...]), unroll=4)
    x_ref[...] = x
    xs_ref[...] = xs


def _ssm_scan(s_loc, a_re, a_im, log_dt, *, T, n_unit_seqs):
    G, P = SSM_GROUPS, SSM_STATE
    nc = s_loc.shape[0]
    n0 = SEQ_BLOCK // T
    nblk = nc // n0
    seq_starts = tuple(range(n_unit_seqs + 1))
    seq_ends = tuple(range(n_unit_seqs)) + (nblk - 1,)
    dup = lambda x: jnp.concatenate([x, x], axis=-1)
    are2, aim2 = dup(a_re), dup(a_im)
    ldt2 = jnp.broadcast_to(log_dt[:, :, None], (2, G, 2 * P))

    def blk_map(d, j):
        return (j + d * (nblk - 1 - 2 * j), 0, d)

    par = pl.BlockSpec((None, G, 2 * P), lambda d, j: (d, 0, 0))
    return pl.pallas_call(
        functools.partial(_ssm_scan_kernel, T=T, n0=n0, seq_starts=seq_starts, seq_ends=seq_ends),
        grid=(2, nblk),
        in_specs=[par, par, par, pl.BlockSpec((n0, G, 2 * P), blk_map)],
        out_specs=pl.BlockSpec((n0, G, 2 * P), blk_map),
        out_shape=jax.ShapeDtypeStruct((nc, G, 4 * P), BF16),
        scratch_shapes=[pltpu.VMEM((G, 2 * P), F32), pltpu.VMEM((G, 2 * P), F32)],
        compiler_params=pltpu.CompilerParams(dimension_semantics=("arbitrary", "arbitrary")),
        name="ssm_scan",
    )(are2, aim2, ldt2, s_loc)


def _ssm_out_kernel(u_ref, mt_ref, s_ref, cs_ref, y_ref, *, gb, ns):
    nc = u_ref.shape[-1]
    for gi in range(gb):
        u = u_ref[gi].reshape(-1, nc)
        y = _dot(mt_ref[gi], u) + _dot_nt(cs_ref[gi], s_ref[:, gi * ns:(gi + 1) * ns])
        y_ref[gi] = _gelu_tanh(y).astype(y_ref.dtype).reshape(y_ref.shape[1:])


def _ssm_out(ut, mt, s_in, cs):
    G, T, C, nc = ut.shape
    ns = cs.shape[2]
    gb = SSM_GROUPS_PER_STEP
    return pl.pallas_call(
        functools.partial(_ssm_out_kernel, gb=gb, ns=ns),
        grid=(G // gb,),
        in_specs=[pl.BlockSpec((gb, T, C, nc), lambda s: (s, 0, 0, 0)),
                  pl.BlockSpec((gb, T * C, T * C), lambda s: (s, 0, 0)),
                  pl.BlockSpec((nc, gb * ns), lambda s: (0, s)),
                  pl.BlockSpec((gb, T * C, ns), lambda s: (s, 0, 0))],
        out_specs=pl.BlockSpec((gb, T, C, nc), lambda s: (s, 0, 0, 0)),
        out_shape=jax.ShapeDtypeStruct((G, T, C, nc), BF16),
        compiler_params=pltpu.CompilerParams(dimension_semantics=("parallel",)),
        name="ssm_out",
    )(ut, mt, s_in, cs)


def _ssm_tok_kernel(y_ref, o_ref, stage_ref):
    T, ncb = y_ref.shape[1], y_ref.shape[-1]
    R = stage_ref.shape[2]
    row = lax.broadcasted_iota(jnp.int32, (R * ncb, R * ncb), 0)
    col = lax.broadcasted_iota(jnp.int32, (R * ncb, R * ncb), 1)
    sel = ((row // R == col % ncb) & (row % R == col // ncb)).astype(BF16)
    for k in range(T // R):
        ycat = jnp.concatenate([y_ref[:, k * R + rr].reshape(-1, ncb) for rr in range(R)], axis=1)
        tok = _dot_nt(sel, ycat)
        stage_ref[:, k] = tok.reshape(ncb, R, tok.shape[-1])
    o_ref[...] = stage_ref[...].reshape(o_ref.shape).astype(o_ref.dtype)


def _ssm_tok(yt):
    G, T, C, nc = yt.shape
    ncb = SSM_TOK_CHUNKS
    est = 2 * G * T * C * ncb * 2 + ncb * T * G * C * 4 + 2 * ncb * T * G * C * 2
    return pl.pallas_call(
        _ssm_tok_kernel,
        grid=(nc // ncb,),
        in_specs=[pl.BlockSpec((G, T, C, ncb), lambda s: (0, 0, 0, s))],
        out_specs=pl.BlockSpec((ncb * T, G * C), lambda s: (s, 0)),
        out_shape=jax.ShapeDtypeStruct((nc * T, G * C), BF16),
        scratch_shapes=[pltpu.VMEM((ncb, T // V7X_SUBLANES, V7X_SUBLANES, G * C), F32)],
        compiler_params=pltpu.CompilerParams(
            dimension_semantics=("parallel",), vmem_limit_bytes=_vmem_limit(est)),
        name="ssm_tok",
    )(yt)


def _s5_mixer_gelu(xb, wt, tables, a_re, a_im, log_dt, *, n_unit_seqs):
    T, G, P = SSM_CHUNK, SSM_GROUPS, SSM_STATE
    nc = xb.shape[0] // T
    mt, gm, cs = tables
    ut = _ssm_in(xb.reshape(nc, T * xb.shape[1]), wt, T=T)
    s_loc = _ssm_state(ut, gm)
    s_in = _ssm_scan(s_loc.reshape(nc, G, 4 * P), a_re, a_im, log_dt, T=T, n_unit_seqs=n_unit_seqs)
    yt = _ssm_out(ut, mt, s_in.reshape(nc, G * 4 * P), cs)
    return _ssm_tok(yt)


def _kv_kernel(m_ref, g_ref, b_ref, w_ref, o_ref, mb_ref):
    @pl.when(pl.program_id(0) == 0)
    def _():
        mb_ref[...] = _layernorm_rows(m_ref[...], g_ref[...], b_ref[...]).astype(BF16)

    o_ref[...] = _dot(mb_ref[...], w_ref[...]).astype(o_ref.dtype)


def _kv_proj(mem, ln_g, ln_b, w_kv):
    n, d = mem.shape
    m = w_kv.shape[1]
    tn = KV_TN
    est = 2 * n * d * 4 + n * d * 2 + 2 * d * tn * 2 + 2 * n * tn * 2 + 3 * n * d * 4
    return pl.pallas_call(
        _kv_kernel,
        grid=(m // tn,),
        in_specs=[pl.BlockSpec((n, d), lambda j: (0, 0)),
                  pl.BlockSpec((1, d), lambda j: (0, 0)),
                  pl.BlockSpec((1, d), lambda j: (0, 0)),
                  pl.BlockSpec((d, tn), lambda j: (0, j))],
        out_specs=pl.BlockSpec((n, tn), lambda j: (0, j)),
        out_shape=jax.ShapeDtypeStruct((n, m), BF16),
        scratch_shapes=[pltpu.VMEM((n, d), BF16)],
        compiler_params=pltpu.CompilerParams(
            dimension_semantics=("arbitrary",), vmem_limit_bytes=_vmem_limit(est)),
        name="kv_proj",
    )(mem, ln_g, ln_b, w_kv)


def _mixer_kernel(ya_ref, ub_ref, ob_ref, g0_ref, g1_ref, g2_ref,
                  wa_ref, wb_ref, wgp_ref, wap_ref, o_ref):
    ya = ya_ref[...]
    y_a = _dot(ya, wa_ref[...]) * _sigmoid(_dot(ya, wb_ref[...]))
    y_b = _dot(ub_ref[...], wgp_ref[...])
    y_c = _dot(ob_ref[...], wap_ref[...])
    merged = (_sigmoid(g0_ref[...].astype(F32)) * y_a
              + _sigmoid(g1_ref[...].astype(F32)) * y_b
              + _sigmoid(g2_ref[...].astype(F32)) * y_c)
    o_ref[...] = merged.astype(o_ref.dtype)


def _mixer(ya, ub, ob, proj, w_glu, w_gproj, w_aproj):
    n, wid = ya.shape
    d = D_MODEL
    tm, tn = MIX_TM, MIX_TN
    nn = d // tn
    gate0 = (proj.shape[1] - 3 * d) // tn
    gate_spec = lambda br: pl.BlockSpec((tm, tn), lambda i, j: (i, gate0 + br * nn + j))
    act_spec = pl.BlockSpec((tm, wid), lambda i, j: (i, 0))
    est = (2 * 3 * tm * wid * 2 + 2 * 3 * tm * tn * 2 + 2 * 4 * wid * tn * 2 + 2 * tm * tn * 2
           + 8 * tm * tn * 4)
    return pl.pallas_call(
        _mixer_kernel,
        grid=(n // tm, nn),
        in_specs=[
            act_spec, act_spec, act_spec,
            gate_spec(0), gate_spec(1), gate_spec(2),
            pl.BlockSpec((wid, tn), lambda i, j: (0, j)),
            pl.BlockSpec((wid, tn), lambda i, j: (0, nn + j)),
            pl.BlockSpec((wid, tn), lambda i, j: (0, j)),
            pl.BlockSpec((wid, tn), lambda i, j: (0, j)),
        ],
        out_specs=pl.BlockSpec((tm, tn), lambda i, j: (i, j)),
        out_shape=jax.ShapeDtypeStruct((n, d), BF16),
        compiler_params=pltpu.CompilerParams(
            dimension_semantics=("parallel", "parallel"),
            vmem_limit_bytes=_vmem_limit(est)),
        name="mixer",
    )(ya, ub, ob, proj, proj, proj, w_glu, w_glu, w_gproj, w_aproj)


def _outproj_ln_kernel(m_ref, w_ref, x_ref, g_ref, b_ref, o_ref):
    y = ALPHA * x_ref[...] + _dot(m_ref[...], w_ref[...])
    o_ref[...] = _layernorm_rows(y, g_ref[...], b_ref[...])


def _outproj_ln(merged, w_out, x, ln_g, ln_b):
    n, d = x.shape
    tm = OUT_TM
    est = 2 * tm * d * 2 + 2 * d * d * 2 + 4 * tm * d * 4 + 2 * tm * d * 4
    return pl.pallas_call(
        _outproj_ln_kernel,
        grid=(n // tm,),
        in_specs=[pl.BlockSpec((tm, d), lambda i: (i, 0)),
                  pl.BlockSpec((d, d), lambda i: (0, 0)),
                  pl.BlockSpec((tm, d), lambda i: (i, 0)),
                  pl.BlockSpec((1, d), lambda i: (0, 0)),
                  pl.BlockSpec((1, d), lambda i: (0, 0))],
        out_specs=pl.BlockSpec((tm, d), lambda i: (i, 0)),
        out_shape=jax.ShapeDtypeStruct((n, d), F32),
        compiler_params=pltpu.CompilerParams(
            dimension_semantics=("parallel",), vmem_limit_bytes=_vmem_limit(est)),
        name="outproj_ln",
    )(merged, w_out, x, ln_g, ln_b)


def _layer(groups, ffn1_w_gu, ffn1_w_down, ln1_g, ln1_b, w_in,
           ssm_a_re, ssm_a_im, ssm_log_dt, ssm_b_re, ssm_b_im, ssm_c_re, ssm_c_im, ssm_d, ssm_w_glu,
           gmlp_ln_g, gmlp_ln_b, gmlp_w_s, gmlp_b_s, gmlp_w_proj,
           mem_ln_g, mem_ln_b, attn_w_kv, attn_w_proj,
           w_out, ln2_g, ln2_b, ffn2_w_gu, ffn2_w_down, ln3_g, ln3_b):
    row = lambda v: v.reshape(1, -1)
    whole = lambda a: ((0, a.shape[1]),)
    gate_up = ((0, D_FF), (D_FF, 2 * D_FF))
    tables, early = _ssm_tables(
        ssm_a_re, ssm_a_im, ssm_log_dt, ssm_b_re, ssm_b_im, ssm_c_re, ssm_c_im, ssm_d,
        [(ffn1_w_gu, 32, gate_up), (ffn1_w_down, GMLP_CHUNK, whole(ffn1_w_down))], T=SSM_CHUNK)
    ffn1 = _ffn_weight_set(early[0], early[1], early[2])
    wt_ssm = _ssm_weight_t(w_in)
    w_kv, w_s = attn_w_kv.astype(BF16), gmlp_w_s.astype(BF16)

    jobs = [("ffn2_gu", ffn2_w_gu, 32, gate_up),
            ("ffn2_down", ffn2_w_down, GMLP_CHUNK, whole(ffn2_w_down)),
            ("glu", ssm_w_glu, 16, whole(ssm_w_glu)),
            ("out", w_out, 32, whole(w_out)),
            ("gproj", gmlp_w_proj, 16, whole(gmlp_w_proj)),
            ("aproj", attn_w_proj, 16, whole(attn_w_proj))]
    bf16_w = {}
    stage1 = []
    for gi, (x, mem, n_unit_seqs) in enumerate(groups):
        mine = jobs[gi::len(groups)]
        x1, conv = _ffn_ln(x, ffn1, row(ln1_g), row(ln1_b),
                           [(w_in, 32, whole(w_in))] if gi == 0 else [])
        if gi == 0:
            w_in_b = conv[0]
        proj, x1b, conv = _in_proj(x1, w_in_b, row(gmlp_ln_g), row(gmlp_ln_b),
                                   [job[1:] for job in mine])
        conv = iter(conv)
        for name, _, _, splits in mine:
            bf16_w[name] = [next(conv) for _ in splits]
        stage1.append((x1, proj, x1b))
    ffn2 = _ffn_weight_set(bf16_w["ffn2_gu"][0], bf16_w["ffn2_gu"][1], bf16_w["ffn2_down"][0])
    w_glu, w_out_b = bf16_w["glu"][0], bf16_w["out"][0]
    w_gproj, w_aproj = bf16_w["gproj"][0], bf16_w["aproj"][0]

    outs = []
    for (x, mem, n_unit_seqs), (x1, proj, x1b) in zip(groups, stage1):
        kv = _kv_proj(mem, row(mem_ln_g), row(mem_ln_b), w_kv)
        ub, ob = _branches(proj, kv, w_s, gmlp_b_s)
        ya = _s5_mixer_gelu(x1b, wt_ssm, tables, ssm_a_re, ssm_a_im, ssm_log_dt,
                            n_unit_seqs=n_unit_seqs)
        merged = _mixer(ya, ub, ob, proj, w_glu, w_gproj, w_aproj)
        x2 = _outproj_ln(merged, w_out_b, x1, row(ln2_g), row(ln2_b))
        outs.append(_ffn_ln(x2, ffn2, row(ln3_g), row(ln3_b))[0])
    return outs


def kernel(x_prompt, x_sample, mem_prompt, mem_sample, ffn1_w_gu, ffn1_w_down, ln1_g, ln1_b, w_in,
           ssm_a_re, ssm_a_im, ssm_log_dt, ssm_b_re, ssm_b_im, ssm_c_re, ssm_c_im, ssm_d, ssm_w_glu,
           gmlp_ln_g, gmlp_ln_b, gmlp_w_s, gmlp_b_s, gmlp_w_proj, mem_ln_g, mem_ln_b, attn_w_kv,
           attn_w_proj, w_out, ln2_g, ln2_b, ffn2_w_gu, ffn2_w_down, ln3_g, ln3_b):
    d = x_prompt.shape[-1]
    assert x_prompt.shape[1] == SEQ_BLOCK and x_sample.shape[0] == 1
    assert x_sample.shape[1] % SEQ_BLOCK == 0
    xs = [x_prompt.reshape(-1, d), x_sample.reshape(-1, d)]
    mems = [mem_prompt.reshape(-1, d), mem_sample.reshape(-1, d)]
    n_unit = [x_prompt.shape[0], 0]
    for l in range(DEPTH):
        xs = _layer(list(zip(xs, mems, n_unit)),
                    ffn1_w_gu[l], ffn1_w_down[l], ln1_g[l], ln1_b[l], w_in[l],
                    ssm_a_re[l], ssm_a_im[l], ssm_log_dt[l], ssm_b_re[l], ssm_b_im[l],
                    ssm_c_re[l], ssm_c_im[l], ssm_d[l], ssm_w_glu[l],
                    gmlp_ln_g[l], gmlp_ln_b[l], gmlp_w_s[l], gmlp_b_s[l], gmlp_w_proj[l],
                    mem_ln_g[l], mem_ln_b[l], attn_w_kv[l], attn_w_proj[l],
                    w_out[l], ln2_g[l], ln2_b[l], ffn2_w_gu[l], ffn2_w_down[l], ln3_g[l], ln3_b[l])
    return (xs[0].reshape(x_prompt.shape), xs[1].reshape(x_sample.shape))
```

```python
import functools
import math

import jax
import jax.numpy as jnp
from jax import lax
from jax.experimental import pallas as pl
from jax.experimental.pallas import tpu as pltpu

F32 = jnp.float32
BF16 = jnp.bfloat16

D_MODEL = 2048
DEPTH = 1
SEQ_BLOCK = 4096
N_MEM = 256
SSM_WIDTH = D_MODEL // 2
SSM_GROUP = 16
SSM_GROUPS = SSM_WIDTH // SSM_GROUP
SSM_STATE = 64
GMLP_WIDTH = D_MODEL // 2
GMLP_CHUNK = 128
GMLP_HEADS = 8
GMLP_HEAD_DIM = GMLP_WIDTH // GMLP_HEADS
ATTN_HEADS = 4
ATTN_HEAD_DIM = D_MODEL // 8
ATTN_WIDTH = ATTN_HEADS * ATTN_HEAD_DIM
D_FF = 5504
ALPHA = (2.0 * DEPTH) ** 0.25
LN_EPS = 1e-5

V7X_LANES = 128
V7X_SUBLANES = 8
V7X_VMEM_BYTES = 64 * 1024 * 1024

SSM_CHUNK = 32
SSM_GROUPS_PER_STEP = 16
SSM_TABLE_GROUPS = 1
SSM_IN_POSITIONS = 4
SSM_TOK_CHUNKS = 128
FFN_TM = 512
FFN_TF = 1024
PROJ_TM = 1024
PROJ_TN = 1024
BRANCH_TM = 1024
MIX_TM = 1024
MIX_TN = 512
OUT_TM = 512
KV_TN = 512


def _vmem_limit(nbytes):
    return int(min(nbytes + (16 << 20), V7X_VMEM_BYTES - (4 << 20)))


def _layernorm_rows(y, g, b):
    mu = jnp.mean(y, axis=-1, keepdims=True)
    yc = y - mu
    var = jnp.mean(yc * yc, axis=-1, keepdims=True)
    return yc * lax.rsqrt(var + LN_EPS) * g + b


def _gelu_tanh(x):
    c = math.sqrt(2.0 / math.pi)
    return 0.5 * x * (1.0 + jnp.tanh(c * (x + 0.044715 * (x * x * x))))


def _sigmoid(x):
    return 1.0 / (1.0 + jnp.exp(-x))


def _dot(a, b):
    return jnp.dot(a, b, preferred_element_type=F32)


def _dot_nt(a, b):
    return lax.dot_general(a, b, (((1,), (1,)), ((), ())), preferred_element_type=F32)


def _swiglu_down(xb, wg, wu, wd):
    gate = _dot(xb, wg)
    up = _dot(xb, wu)
    act = (gate * _sigmoid(gate) * up).astype(BF16)
    return _dot(act, wd)


def _ffn_ln_kernel(x_ref, wg_ref, wu_ref, wd_ref, wgt_ref, wut_ref, wdt_ref, g_ref, b_ref,
                   *refs, casts):
    n_src = len(casts)
    src_refs = refs[:n_src]
    o_ref = refs[n_src]
    dst_refs = refs[n_src + 1:-1]
    xb_ref = refs[-1]
    j = pl.program_id(1)

    @pl.when(j == 0)
    def _():
        xb_ref[...] = x_ref[...].astype(BF16)
        o_ref[...] = jnp.zeros_like(o_ref)

    _cast_rows(src_refs, dst_refs, casts)
    o_ref[...] += _swiglu_down(xb_ref[...], wg_ref[...], wu_ref[...], wd_ref[...])

    @pl.when(j == pl.num_programs(1) - 1)
    def _():
        acc = o_ref[...] + _swiglu_down(xb_ref[...], wgt_ref[...], wut_ref[...], wdt_ref[...])
        y = ALPHA * x_ref[...] + 0.5 * acc
        o_ref[...] = _layernorm_rows(y, g_ref[...], b_ref[...])


def _ffn_ln(x, weights, ln_g, ln_b, cast_jobs=()):
    wg, wu, wd, wgt, wut, wdt = weights
    n, d = x.shape
    tm, tf = FFN_TM, FFN_TF
    nf = wd.shape[0] // tf
    ft = wdt.shape[0]
    once = dict(pipeline_mode=pl.Buffered(1))
    src_specs, dst_specs, dst_shapes, casts, cast_bytes = _cast_job_specs(
        cast_jobs, (n // tm) * nf, lambda i, j: i * nf + j)
    est = (2 * tm * d * 4 + 2 * tm * d * 4 + tm * d * 2
           + 2 * 3 * (d * tf * 2) + 3 * (d * ft * 2) + 3 * tm * tf * 4 + cast_bytes)
    res = pl.pallas_call(
        functools.partial(_ffn_ln_kernel, casts=casts),
        grid=(n // tm, nf),
        in_specs=[
            pl.BlockSpec((tm, d), lambda i, j: (i, 0)),
            pl.BlockSpec((d, tf), lambda i, j: (0, j)),
            pl.BlockSpec((d, tf), lambda i, j: (0, j)),
            pl.BlockSpec((tf, d), lambda i, j: (j, 0)),
            pl.BlockSpec((d, ft), lambda i, j: (0, 0), **once),
            pl.BlockSpec((d, ft), lambda i, j: (0, 0), **once),
            pl.BlockSpec((ft, d), lambda i, j: (0, 0), **once),
            pl.BlockSpec((1, d), lambda i, j: (0, 0)),
            pl.BlockSpec((1, d), lambda i, j: (0, 0)),
        ] + src_specs,
        out_specs=[pl.BlockSpec((tm, d), lambda i, j: (i, 0))] + dst_specs,
        out_shape=[jax.ShapeDtypeStruct((n, d), F32)] + dst_shapes,
        scratch_shapes=[pltpu.VMEM((tm, d), BF16)],
        compiler_params=pltpu.CompilerParams(
            dimension_semantics=("arbitrary", "arbitrary"),
            vmem_limit_bytes=_vmem_limit(est)),
        name="ffn_ln",
    )(x, wg, wu, wd, wgt, wut, wdt, ln_g, ln_b, *[job[0] for job in cast_jobs])
    return res[0], res[1:]


def _ffn_weight_set(wg, wu, wd):
    full = (D_FF // FFN_TF) * FFN_TF
    return wg, wu, wd, wg[:, full:], wu[:, full:], wd[full:]


def _cast_job_specs(cast_jobs, steps, step_index):
    src_specs, dst_specs, dst_shapes, casts, nbytes = [], [], [], [], 0
    for src, rows, splits in cast_jobs:
        nblk = src.shape[0] // rows
        assert src.shape[0] % rows == 0 and nblk <= steps
        blk_map = lambda *idx, nblk=nblk: (jnp.minimum(step_index(*idx), nblk - 1), 0)
        src_specs.append(pl.BlockSpec((rows, src.shape[1]), blk_map))
        for lo, hi in splits:
            dst_specs.append(pl.BlockSpec((rows, hi - lo), blk_map))
            dst_shapes.append(jax.ShapeDtypeStruct((src.shape[0], hi - lo), BF16))
        casts.append(tuple(splits))
        nbytes += 2 * rows * src.shape[1] * 6
    return src_specs, dst_specs, dst_shapes, tuple(casts), nbytes


def _cast_rows(src_refs, dst_refs, casts):
    dsts = iter(dst_refs)
    for src, splits in zip(src_refs, casts):
        for lo, hi in splits:
            next(dsts)[...] = src[:, lo:hi].astype(BF16)


def _in_proj_kernel(x_ref, w_ref, lng_ref, lnb_ref, *refs, casts):
    n_src = len(casts)
    src_refs = refs[:n_src]
    o_ref, xb_ref = refs[n_src:n_src + 2]
    dst_refs = refs[n_src + 2:]
    j = pl.program_id(1)

    @pl.when(j == 0)
    def _():
        xb_ref[...] = x_ref[...].astype(BF16)

    def convert_weights():
        _cast_rows(src_refs, dst_refs, casts)

    @pl.when(j == 0)
    def _():
        convert_weights()
        o_ref[...] = _gelu_tanh(_dot(xb_ref[...], w_ref[...])).astype(o_ref.dtype)

    @pl.when(j == 1)
    def _():
        convert_weights()
        v = _gelu_tanh(_dot(xb_ref[...], w_ref[...]))
        o_ref[...] = _layernorm_rows(v, lng_ref[...], lnb_ref[...]).astype(o_ref.dtype)

    @pl.when(j >= 2)
    def _():
        convert_weights()
        o_ref[...] = _dot(xb_ref[...], w_ref[...]).astype(o_ref.dtype)


def _in_proj(x, w, gln_g, gln_b, cast_jobs):
    n, k = x.shape
    tm, tn = PROJ_TM, PROJ_TN
    assert tn == GMLP_WIDTH
    skip = SSM_WIDTH // tn
    m = w.shape[1] - SSM_WIDTH
    ncol = m // tn
    est = 2 * tm * k * 4 + 2 * tm * k * 2 + 2 * k * tn * 2 + 2 * tm * tn * 2 + 3 * tm * tn * 4
    src_specs, dst_specs, dst_shapes, casts, cast_bytes = _cast_job_specs(
        cast_jobs, (n // tm) * ncol, lambda i, j: i * ncol + j)
    est += cast_bytes
    res = pl.pallas_call(
        functools.partial(_in_proj_kernel, casts=casts),
        grid=(n // tm, ncol),
        in_specs=[pl.BlockSpec((tm, k), lambda i, j: (i, 0)),
                  pl.BlockSpec((k, tn), lambda i, j: (0, skip + j)),
                  pl.BlockSpec((1, tn), lambda i, j: (0, 0)),
                  pl.BlockSpec((1, tn), lambda i, j: (0, 0))] + src_specs,
        out_specs=[pl.BlockSpec((tm, tn), lambda i, j: (i, j)),
                   pl.BlockSpec((tm, k), lambda i, j: (i, 0))] + dst_specs,
        out_shape=[jax.ShapeDtypeStruct((n, m), BF16),
                   jax.ShapeDtypeStruct((n, k), BF16)] + dst_shapes,
        compiler_params=pltpu.CompilerParams(
            dimension_semantics=("arbitrary", "arbitrary"),
            vmem_limit_bytes=_vmem_limit(est)),
        name="in_proj",
    )(x, w, gln_g, gln_b, *[job[0] for job in cast_jobs])
    return res[0], res[1], res[2:]


def _branches_kernel(u_ref, v_ref, q_ref, kv_ref, ws_ref, bs_ref, ub_ref, ob_ref, *, tm):
    hd = GMLP_HEAD_DIM

    def chunk_body(ci, carry):
        rows = pl.ds(pl.multiple_of(ci * GMLP_CHUNK, GMLP_CHUNK), GMLP_CHUNK)
        for h in range(GMLP_HEADS):
            cols = slice(h * hd, (h + 1) * hd)
            mixed = _dot(ws_ref[h], v_ref[rows, cols]) + bs_ref[:, h:h + 1]
            ub_ref[rows, cols] = (u_ref[rows, cols].astype(F32) * mixed).astype(BF16)
        return carry

    lax.fori_loop(0, tm // GMLP_CHUNK, chunk_body, 0)

    ad = ATTN_HEAD_DIM
    scale = ad ** -0.5
    for h in range(ATTN_HEADS):
        cols = slice(h * ad, (h + 1) * ad)
        vcols = slice(ATTN_WIDTH + h * ad, ATTN_WIDTH + (h + 1) * ad)
        s = _dot_nt(q_ref[:, cols], kv_ref[:, cols]) * scale
        p = jnp.exp(s - jnp.max(s, axis=-1, keepdims=True))
        p = p * (1.0 / jnp.sum(p, axis=-1, keepdims=True))
        ob_ref[:, cols] = _dot(p.astype(BF16), kv_ref[:, vcols]).astype(BF16)


def _branches(proj, kv, w_s, b_s):
    n = proj.shape[0]
    tm = BRANCH_TM
    wid = GMLP_WIDTH
    assert wid == ATTN_WIDTH
    blocks_per_seq = SEQ_BLOCK // tm
    n_mem_batches = kv.shape[0] // N_MEM
    col = lambda c: pl.BlockSpec((tm, wid), lambda i: (i, c))
    full2 = lambda a: pl.BlockSpec(a.shape, lambda i: (0, 0))
    out = pl.BlockSpec((tm, wid), lambda i: (i, 0))
    return pl.pallas_call(
        functools.partial(_branches_kernel, tm=tm),
        grid=(n // tm,),
        in_specs=[col(0), col(1), col(2),
                  pl.BlockSpec((N_MEM, 2 * ATTN_WIDTH),
                               lambda i: (jnp.minimum(i // blocks_per_seq, n_mem_batches - 1), 0)),
                  pl.BlockSpec(w_s.shape, lambda i: (0, 0, 0)),
                  full2(b_s)],
        out_specs=[out, out],
        out_shape=[jax.ShapeDtypeStruct((n, wid), BF16), jax.ShapeDtypeStruct((n, wid), BF16)],
        compiler_params=pltpu.CompilerParams(dimension_semantics=("parallel",)),
        name="branches",
    )(proj, proj, proj, kv, w_s, b_s)


def _ssm_weight_t_kernel(w_ref, o_ref):
    o_ref[...] = w_ref[...].T.astype(o_ref.dtype)


def _ssm_weight_t(w_in):
    d = w_in.shape[0]
    tn = 2 * V7X_LANES
    return pl.pallas_call(
        _ssm_weight_t_kernel,
        grid=(SSM_WIDTH // tn,),
        in_specs=[pl.BlockSpec((d, tn), lambda j: (0, j))],
        out_specs=pl.BlockSpec((tn, d), lambda j: (j, 0)),
        out_shape=jax.ShapeDtypeStruct((SSM_WIDTH, d), BF16),
        compiler_params=pltpu.CompilerParams(dimension_semantics=("parallel",)),
        name="ssm_weight_t",
    )(w_in)


def _ssm_in_kernel(x_ref, wt_ref, o_ref, *, rb):
    d = wt_ref.shape[1]
    for k in range(rb):
        ut = _dot_nt(wt_ref[...], x_ref[:, k * d:(k + 1) * d])
        o_ref[:, k] = ut.astype(BF16).reshape(o_ref.shape[0], o_ref.shape[2], o_ref.shape[3])


def _ssm_in(xc, wt, *, T):
    nc = xc.shape[0]
    d = xc.shape[1] // T
    G, C = SSM_GROUPS, SSM_GROUP
    rb = SSM_IN_POSITIONS
    est = 2 * nc * rb * d * 2 + 2 * G * C * d * 2 + 2 * rb * G * C * nc * 2 + 2 * G * C * nc * 4
    return pl.pallas_call(
        functools.partial(_ssm_in_kernel, rb=rb),
        grid=(T // rb,),
        in_specs=[pl.BlockSpec((nc, rb * d), lambda r: (0, r)),
                  pl.BlockSpec((G * C, d), lambda r: (0, 0))],
        out_specs=pl.BlockSpec((G, rb, C, nc), lambda r: (0, r, 0, 0)),
        out_shape=jax.ShapeDtypeStruct((G, T, C, nc), BF16),
        compiler_params=pltpu.CompilerParams(
            dimension_semantics=("parallel",), vmem_limit_bytes=_vmem_limit(est)),
        name="ssm_in",
    )(xc, wt)


def _cmul(x, y):
    return x[0] * y[0] - x[1] * y[1], x[0] * y[1] + x[1] * y[0]


def _split_hi_lo(x):
    hi = x.astype(BF16)
    lo = (x - hi.astype(F32)).astype(BF16)
    return hi, lo


def _dot_hi_lo(a, b):
    ah, al = _split_hi_lo(a)
    bh, bl = _split_hi_lo(b)
    return _dot(ah, bh) + _dot(al, bh) + _dot(ah, bl)


N_TABLE_INPUTS = 2
N_TABLE_OUTPUTS = 3


def _ssm_tables_kernel(*refs, T, gb, casts):
    n_src = len(casts)
    n_dst = sum(len(splits) for splits in casts)
    ins = refs[:N_TABLE_INPUTS]
    srcs = refs[N_TABLE_INPUTS:N_TABLE_INPUTS + n_src]
    outs = refs[N_TABLE_INPUTS + n_src:N_TABLE_INPUTS + n_src + N_TABLE_OUTPUTS]
    dsts = refs[N_TABLE_INPUTS + n_src + N_TABLE_OUTPUTS:][:n_dst]
    q_ref = refs[-1]
    _cast_rows(srcs, dsts, casts)
    for gi in range(gb):
        _ssm_tables_group(*[r.at[gi] for r in ins + outs + (q_ref,)], T=T)


def _ssm_tables_group(col_ref, row_ref, mt_ref, gm_ref, cs_ref, q_ref, *, T):
    P, C = SSM_STATE, SSM_GROUP
    rpt = V7X_LANES // C
    nt = T // rpt
    wide = 2 * T * C

    lane = lax.broadcasted_iota(jnp.int32, (1, V7X_LANES), 1)
    rr = lax.shift_right_logical(lane, 4)
    expand = (lax.broadcasted_iota(jnp.int32, (C, V7X_LANES), 0)
              == (lax.broadcasted_iota(jnp.int32, (C, V7X_LANES), 1) & (C - 1))).astype(F32)

    pc = col_ref[0:P, C:C + 8]

    lane4 = lax.broadcasted_iota(jnp.int32, (1, 4 * P), 1)
    is_re = (lax.shift_right_logical(lane4, 6) & 1) == 0
    is_f = lane4 < 2 * P
    pr = row_ref[0:8]
    dt4 = jnp.exp(pr[2:3])
    zr4, zi4 = pr[0:1] * dt4, pr[1:2] * dt4

    kp = ((T + 1 + 7) // 8) * 8
    krow = lax.broadcasted_iota(jnp.int32, (kp, 4 * P), 0).astype(F32)
    mag = jnp.exp(krow * zr4)
    pw_r, pw_i = mag * jnp.cos(krow * zi4), mag * jnp.sin(krow * zi4)

    def states_on_rows(tab, lo):
        slab = tab[:, lo:lo + V7X_LANES]
        padded = jnp.concatenate([slab, jnp.zeros((V7X_LANES - kp, V7X_LANES), F32)], axis=0)
        return padded.T[0:P]

    pt_f = states_on_rows(pw_r, 0), states_on_rows(pw_i, 0)
    pt_b = states_on_rows(pw_r, 2 * P), states_on_rows(pw_i, 2 * P)

    def stair(pt, descending):
        re = jnp.zeros((P, V7X_LANES), F32)
        im = re
        for k in range(rpt):
            m = rpt - 1 - k if descending else k
            re = jnp.where(rr == k, pt[0][:, m:m + 1], re)
            im = jnp.where(rr == k, pt[1][:, m:m + 1], im)
        return re, im

    def zoh(d, pt):
        are = pc[:, d:d + 1]
        aim = pc[:, 2 + d:3 + d]
        nr = pt[0][:, 1:2] - 1.0
        ni = pt[1][:, 1:2]
        den = are * are + aim * aim
        return (nr * are + ni * aim) / den, (ni * are - nr * aim) / den

    def col(pt, m):
        return pt[0][:, m:m + 1], pt[1][:, m:m + 1]

    dsc_f = stair(pt_f, True)
    asc_b = stair(pt_b, False)
    btile = jnp.dot(col_ref[:, 0:C], expand, precision=lax.Precision.HIGHEST,
                    preferred_element_type=F32)
    bt = (btile[0:P], btile[P:2 * P])
    bb_f = _cmul(zoh(0, pt_f), bt)
    bd_f = _cmul(dsc_f, bb_f)
    ba_b = _cmul(asc_b, _cmul(zoh(1, pt_b), bt))

    for j in range(nt):
        cols = slice(j * V7X_LANES, (j + 1) * V7X_LANES)
        xf = _cmul(bd_f, col(pt_f, T - rpt - rpt * j))
        xb = _cmul(ba_b, col(pt_b, rpt * j))
        gm_ref[0:P, cols] = xf[0].astype(BF16)
        gm_ref[P:2 * P, cols] = xf[1].astype(BF16)
        gm_ref[2 * P:3 * P, cols] = xb[0].astype(BF16)
        gm_ref[3 * P:4 * P, cols] = xb[1].astype(BF16)

    zeros = jnp.zeros((2 * P, V7X_LANES), F32)
    for j in range(2 * nt):
        cols = slice(j * V7X_LANES, (j + 1) * V7X_LANES)
        if j < nt:
            qf = _cmul(bd_f, col(pt_f, T - rpt * j - (rpt - 1)))
            q_ref[0:P, cols] = qf[0]
            q_ref[P:2 * P, cols] = qf[1]
            q_ref[2 * P:4 * P, cols] = zeros
        else:
            qb = _cmul(ba_b, col(pt_b, rpt * j - T))
            q_ref[2 * P:3 * P, cols] = qb[0]
            q_ref[3 * P:4 * P, cols] = qb[1]
            if j == nt:
                center = rr == 0
                q_ref[0:P, cols] = jnp.where(center, bb_f[0], 0.0)
                q_ref[P:2 * P, cols] = jnp.where(center, bb_f[1], 0.0)
            else:
                q_ref[0:2 * P, cols] = zeros

    cre = row_ref[8:8 + C]
    cim = row_ref[8 + C:8 + 2 * C]

    for r in range(T):
        prr = jnp.where(is_f, pw_r[r + 1:r + 2], pw_r[T - r:T - r + 1])
        pii = jnp.where(is_f, pw_i[r + 1:r + 2], pw_i[T - r:T - r + 1])
        blk = jnp.where(is_re, cre * prr - cim * pii, -(cre * pii + cim * prr))
        cs_ref[r * C:(r + 1) * C, :] = blk.astype(BF16)

    lhs = jnp.where(is_re, cre, -cim)
    zt = _dot_hi_lo(lhs, q_ref[...])
    lanew = lax.broadcasted_iota(jnp.int32, (C, wide), 1)
    roww = lax.broadcasted_iota(jnp.int32, (C, wide), 0)
    diag = (lax.shift_right_logical(lanew, 4) == T) & ((lanew & (C - 1)) == roww)
    d0 = 8 + 2 * C
    dtile = jnp.concatenate([row_ref[d0 + k:d0 + k + 1] for k in range(wide // (4 * P))], axis=1)
    zt = zt + jnp.where(diag, dtile, 0.0)

    for r in range(T):
        off = (T - r) * C
        shifted = pltpu.roll(zt, (wide - off) % wide, 1)
        mt_ref[r * C:(r + 1) * C, :] = shifted[:, :T * C].astype(BF16)


def _ssm_tables(a_re, a_im, log_dt, b_re, b_im, c_re, c_im, d_skip, cast_jobs, *, T):
    G, P, C = SSM_GROUPS, SSM_STATE, SSM_GROUP
    ldt = jnp.broadcast_to(log_dt[:, :, None], (2, G, P))
    zc = jnp.zeros((G, P), F32)
    pcol = jnp.stack([a_re[0], a_re[1], a_im[0], a_im[1], ldt[0], ldt[1], zc, zc], axis=-1)

    def row4(x):
        return jnp.concatenate([x[0], x[0], x[1], x[1]], axis=-1)

    zr4 = jnp.zeros((G, 4 * P), F32)
    prow = jnp.stack([row4(a_re), row4(a_im), row4(ldt), zr4, zr4, zr4, zr4, zr4], axis=1)
    bcat = jnp.concatenate([b_re, b_im], axis=1)
    dtile = jnp.tile(d_skip, (1, 2 * T)).reshape(G, -1, 4 * P)
    colpack = jnp.concatenate([bcat, jnp.pad(pcol, ((0, 0), (0, P), (0, 0)))], axis=-1)
    rowpack = jnp.concatenate([prow, row4(c_re), row4(c_im), dtile], axis=1)

    tc = T * C
    gb = SSM_TABLE_GROUPS
    blk = lambda *shape: pl.BlockSpec((gb,) + shape, lambda g: (g,) + (0,) * len(shape))
    src_specs, dst_specs, dst_shapes, casts, cast_bytes = _cast_job_specs(
        cast_jobs, G // gb, lambda g: g)
    est = 2 * gb * (2 * tc * tc * 2 + 3 * 4 * P * tc * 4) + cast_bytes
    res = pl.pallas_call(
        functools.partial(_ssm_tables_kernel, T=T, gb=gb, casts=casts),
        grid=(G // gb,),
        in_specs=[blk(*colpack.shape[1:]), blk(*rowpack.shape[1:])] + src_specs,
        out_specs=[blk(tc, tc), blk(4 * P, tc), blk(tc, 4 * P)] + dst_specs,
        out_shape=[
            jax.ShapeDtypeStruct((G, tc, tc), BF16),
            jax.ShapeDtypeStruct((G, 4 * P, tc), BF16),
            jax.ShapeDtypeStruct((G, tc, 4 * P), BF16),
        ] + dst_shapes,
        scratch_shapes=[pltpu.VMEM((gb, 4 * P, 2 * tc), F32)],
        compiler_params=pltpu.CompilerParams(
            dimension_semantics=("arbitrary",), vmem_limit_bytes=_vmem_limit(est)),
        name="ssm_tables",
    )(colpack, rowpack, *[job[0] for job in cast_jobs])
    return tuple(res[:N_TABLE_OUTPUTS]), res[N_TABLE_OUTPUTS:]


def _ssm_state_kernel(u_ref, gm_ref, o_ref, *, gb, ns):
    nc = u_ref.shape[-1]
    for gi in range(gb):
        u = u_ref[gi].reshape(-1, nc)
        o_ref[:, gi * ns:(gi + 1) * ns] = _dot(gm_ref[gi], u).T


def _ssm_state(ut, gm):
    G, T, C, nc = ut.shape
    ns = gm.shape[1]
    gb = SSM_GROUPS_PER_STEP
    return pl.pallas_call(
        functools.partial(_ssm_state_kernel, gb=gb, ns=ns),
        grid=(G // gb,),
        in_specs=[pl.BlockSpec((gb, T, C, nc), lambda s: (s, 0, 0, 0)),
                  pl.BlockSpec((gb, ns, T * C), lambda s: (s, 0, 0))],
        out_specs=pl.BlockSpec((nc, gb * ns), lambda s: (0, s)),
        out_shape=jax.ShapeDtypeStruct((nc, G * ns), F32),
        compiler_params=pltpu.CompilerParams(dimension_semantics=("parallel",)),
        name="ssm_state",
    )(ut, gm)


def _ssm_scan_kernel(are_ref, aim_ref, ldt_ref, s_ref, o_ref, x_ref, xs_ref, *, T, n0, seq_starts, seq_ends):
    d = pl.program_id(0)
    j = pl.program_id(1)
    nblk = pl.num_programs(1)
    blk = j + d * (nblk - 1 - 2 * j)
    P = SSM_STATE

    is_start = functools.reduce(jnp.logical_or, [blk == s for s in seq_starts])
    is_end = functools.reduce(jnp.logical_or, [blk == e for e in seq_ends])
    reset = jnp.where(d == 0, is_start, is_end)

    @pl.when(reset)
    def _():
        x_ref[...] = jnp.zeros_like(x_ref)
        xs_ref[...] = jnp.zeros_like(xs_ref)

    dt = jnp.exp(ldt_ref[...])
    zr = are_ref[...] * dt
    zi = aim_ref[...] * dt
    mag = jnp.exp(float(T) * zr)
    mr = mag * jnp.cos(float(T) * zi)
    mi = mag * jnp.sin(float(T) * zi)
    lane = lax.broadcasted_iota(jnp.int32, mr.shape, 1)
    m2 = jnp.where(lane < P, -mi, mi)
    m2s = -m2

    def body(k, carry):
        x, xs = carry
        row = k + d * (n0 - 1 - 2 * k)
        loc = s_ref[row]
        o_ref[row] = x.astype(o_ref.dtype)
        locs = pltpu.roll(loc, P, 1)
        return x * mr + xs * m2 + loc, xs * mr + x * m2s + locs

    x, xs = lax.fori_loop(0, n0, body, (x_ref[...], xs_ref[...]), unroll=4)
    x_ref[...] = x
    xs_ref[...] = xs


def _ssm_scan(s_loc, a_re, a_im, log_dt, *, T, n_unit_seqs):
    G, P = SSM_GROUPS, SSM_STATE
    nc = s_loc.shape[0]
    n0 = SEQ_BLOCK // T
    nblk = nc // n0
    seq_starts = tuple(range(n_unit_seqs + 1))
    seq_ends = tuple(range(n_unit_seqs)) + (nblk - 1,)
    dup = lambda x: jnp.concatenate([x, x], axis=-1)
    are2, aim2 = dup(a_re), dup(a_im)
    ldt2 = jnp.broadcast_to(log_dt[:, :, None], (2, G, 2 * P))

    def blk_map(d, j):
        return (j + d * (nblk - 1 - 2 * j), 0, d)

    par = pl.BlockSpec((None, G, 2 * P), lambda d, j: (d, 0, 0))
    return pl.pallas_call(
        functools.partial(_ssm_scan_kernel, T=T, n0=n0, seq_starts=seq_starts, seq_ends=seq_ends),
        grid=(2, nblk),
        in_specs=[par, par, par, pl.BlockSpec((n0, G, 2 * P), blk_map)],
        out_specs=pl.BlockSpec((n0, G, 2 * P), blk_map),
        out_shape=jax.ShapeDtypeStruct((nc, G, 4 * P), BF16),
        scratch_shapes=[pltpu.VMEM((G, 2 * P), F32), pltpu.VMEM((G, 2 * P), F32)],
        compiler_params=pltpu.CompilerParams(dimension_semantics=("arbitrary", "arbitrary")),
        name="ssm_scan",
    )(are2, aim2, ldt2, s_loc)


def _ssm_out_kernel(u_ref, mt_ref, s_ref, cs_ref, y_ref, *, gb, ns):
    nc = u_ref.shape[-1]
    for gi in range(gb):
        u = u_ref[gi].reshape(-1, nc)
        y = _dot(mt_ref[gi], u) + _dot_nt(cs_ref[gi], s_ref[:, gi * ns:(gi + 1) * ns])
        y_ref[gi] = _gelu_tanh(y).astype(y_ref.dtype).reshape(y_ref.shape[1:])


def _ssm_out(ut, mt, s_in, cs):
    G, T, C, nc = ut.shape
    ns = cs.shape[2]
    gb = SSM_GROUPS_PER_STEP
    return pl.pallas_call(
        functools.partial(_ssm_out_kernel, gb=gb, ns=ns),
        grid=(G // gb,),
        in_specs=[pl.BlockSpec((gb, T, C, nc), lambda s: (s, 0, 0, 0)),
                  pl.BlockSpec((gb, T * C, T * C), lambda s: (s, 0, 0)),
                  pl.BlockSpec((nc, gb * ns), lambda s: (0, s)),
                  pl.BlockSpec((gb, T * C, ns), lambda s: (s, 0, 0))],
        out_specs=pl.BlockSpec((gb, T, C, nc), lambda s: (s, 0, 0, 0)),
        out_shape=jax.ShapeDtypeStruct((G, T, C, nc), BF16),
        compiler_params=pltpu.CompilerParams(dimension_semantics=("parallel",)),
        name="ssm_out",
    )(ut, mt, s_in, cs)


def _ssm_tok_kernel(y_ref, o_ref, stage_ref):
    T, ncb = y_ref.shape[1], y_ref.shape[-1]
    R = stage_ref.shape[2]
    row = lax.broadcasted_iota(jnp.int32, (R * ncb, R * ncb), 0)
    col = lax.broadcasted_iota(jnp.int32, (R * ncb, R * ncb), 1)
    sel = ((row // R == col % ncb) & (row % R == col // ncb)).astype(BF16)
    for k in range(T // R):
        ycat = jnp.concatenate([y_ref[:, k * R + rr].reshape(-1, ncb) for rr in range(R)], axis=1)
        tok = _dot_nt(sel, ycat)
        stage_ref[:, k] = tok.reshape(ncb, R, tok.shape[-1])
    o_ref[...] = stage_ref[...].reshape(o_ref.shape).astype(o_ref.dtype)


def _ssm_tok(yt):
    G, T, C, nc = yt.shape
    ncb = SSM_TOK_CHUNKS
    est = 2 * G * T * C * ncb * 2 + ncb * T * G * C * 4 + 2 * ncb * T * G * C * 2
    return pl.pallas_call(
        _ssm_tok_kernel,
        grid=(nc // ncb,),
        in_specs=[pl.BlockSpec((G, T, C, ncb), lambda s: (0, 0, 0, s))],
        out_specs=pl.BlockSpec((ncb * T, G * C), lambda s: (s, 0)),
        out_shape=jax.ShapeDtypeStruct((nc * T, G * C), BF16),
        scratch_shapes=[pltpu.VMEM((ncb, T // V7X_SUBLANES, V7X_SUBLANES, G * C), F32)],
        compiler_params=pltpu.CompilerParams(
            dimension_semantics=("parallel",), vmem_limit_bytes=_vmem_limit(est)),
        name="ssm_tok",
    )(yt)


def _s5_mixer_gelu(xb, wt, tables, a_re, a_im, log_dt, *, n_unit_seqs):
    T, G, P = SSM_CHUNK, SSM_GROUPS, SSM_STATE
    nc = xb.shape[0] // T
    mt, gm, cs = tables
    ut = _ssm_in(xb.reshape(nc, T * xb.shape[1]), wt, T=T)
    s_loc = _ssm_state(ut, gm)
    s_in = _ssm_scan(s_loc.reshape(nc, G, 4 * P), a_re, a_im, log_dt, T=T, n_unit_seqs=n_unit_seqs)
    yt = _ssm_out(ut, mt, s_in.reshape(nc, G * 4 * P), cs)
    return _ssm_tok(yt)


def _kv_kernel(m_ref, g_ref, b_ref, w_ref, o_ref, mb_ref):
    @pl.when(pl.program_id(0) == 0)
    def _():
        mb_ref[...] = _layernorm_rows(m_ref[...], g_ref[...], b_ref[...]).astype(BF16)

    o_ref[...] = _dot(mb_ref[...], w_ref[...]).astype(o_ref.dtype)


def _kv_proj(mem, ln_g, ln_b, w_kv):
    n, d = mem.shape
    m = w_kv.shape[1]
    tn = KV_TN
    est = 2 * n * d * 4 + n * d * 2 + 2 * d * tn * 2 + 2 * n * tn * 2 + 3 * n * d * 4
    return pl.pallas_call(
        _kv_kernel,
        grid=(m // tn,),
        in_specs=[pl.BlockSpec((n, d), lambda j: (0, 0)),
                  pl.BlockSpec((1, d), lambda j: (0, 0)),
                  pl.BlockSpec((1, d), lambda j: (0, 0)),
                  pl.BlockSpec((d, tn), lambda j: (0, j))],
        out_specs=pl.BlockSpec((n, tn), lambda j: (0, j)),
        out_shape=jax.ShapeDtypeStruct((n, m), BF16),
        scratch_shapes=[pltpu.VMEM((n, d), BF16)],
        compiler_params=pltpu.CompilerParams(
            dimension_semantics=("arbitrary",), vmem_limit_bytes=_vmem_limit(est)),
        name="kv_proj",
    )(mem, ln_g, ln_b, w_kv)


def _mixer_kernel(ya_ref, ub_ref, ob_ref, g0_ref, g1_ref, g2_ref,
                  wa_ref, wb_ref, wgp_ref, wap_ref, o_ref):
    ya = ya_ref[...]
    y_a = _dot(ya, wa_ref[...]) * _sigmoid(_dot(ya, wb_ref[...]))
    y_b = _dot(ub_ref[...], wgp_ref[...])
    y_c = _dot(ob_ref[...], wap_ref[...])
    merged = (_sigmoid(g0_ref[...].astype(F32)) * y_a
              + _sigmoid(g1_ref[...].astype(F32)) * y_b
              + _sigmoid(g2_ref[...].astype(F32)) * y_c)
    o_ref[...] = merged.astype(o_ref.dtype)


def _mixer(ya, ub, ob, proj, w_glu, w_gproj, w_aproj):
    n, wid = ya.shape
    d = D_MODEL
    tm, tn = MIX_TM, MIX_TN
    nn = d // tn
    gate0 = (proj.shape[1] - 3 * d) // tn
    gate_spec = lambda br: pl.BlockSpec((tm, tn), lambda i, j: (i, gate0 + br * nn + j))
    act_spec = pl.BlockSpec((tm, wid), lambda i, j: (i, 0))
    est = (2 * 3 * tm * wid * 2 + 2 * 3 * tm * tn * 2 + 2 * 4 * wid * tn * 2 + 2 * tm * tn * 2
           + 8 * tm * tn * 4)
    return pl.pallas_call(
        _mixer_kernel,
        grid=(n // tm, nn),
        in_specs=[
            act_spec, act_spec, act_spec,
            gate_spec(0), gate_spec(1), gate_spec(2),
            pl.BlockSpec((wid, tn), lambda i, j: (0, j)),
            pl.BlockSpec((wid, tn), lambda i, j: (0, nn + j)),
            pl.BlockSpec((wid, tn), lambda i, j: (0, j)),
            pl.BlockSpec((wid, tn), lambda i, j: (0, j)),
        ],
        out_specs=pl.BlockSpec((tm, tn), lambda i, j: (i, j)),
        out_shape=jax.ShapeDtypeStruct((n, d), BF16),
        compiler_params=pltpu.CompilerParams(
            dimension_semantics=("parallel", "parallel"),
            vmem_limit_bytes=_vmem_limit(est)),
        name="mixer",
    )(ya, ub, ob, proj, proj, proj, w_glu, w_glu, w_gproj, w_aproj)


def _outproj_ln_kernel(m_ref, w_ref, x_ref, g_ref, b_ref, o_ref):
    y = ALPHA * x_ref[...] + _dot(m_ref[...], w_ref[...])
    o_ref[...] = _layernorm_rows(y, g_ref[...], b_ref[...])


def _outproj_ln(merged, w_out, x, ln_g, ln_b):
    n, d = x.shape
    tm = OUT_TM
    est = 2 * tm * d * 2 + 2 * d * d * 2 + 4 * tm * d * 4 + 2 * tm * d * 4
    return pl.pallas_call(
        _outproj_ln_kernel,
        grid=(n // tm,),
        in_specs=[pl.BlockSpec((tm, d), lambda i: (i, 0)),
                  pl.BlockSpec((d, d), lambda i: (0, 0)),
                  pl.BlockSpec((tm, d), lambda i: (i, 0)),
                  pl.BlockSpec((1, d), lambda i: (0, 0)),
                  pl.BlockSpec((1, d), lambda i: (0, 0))],
        out_specs=pl.BlockSpec((tm, d), lambda i: (i, 0)),
        out_shape=jax.ShapeDtypeStruct((n, d), F32),
        compiler_params=pltpu.CompilerParams(
            dimension_semantics=("parallel",), vmem_limit_bytes=_vmem_limit(est)),
        name="outproj_ln",
    )(merged, w_out, x, ln_g, ln_b)


def _layer(groups, ffn1_w_gu, ffn1_w_down, ln1_g, ln1_b, w_in,
           ssm_a_re, ssm_a_im, ssm_log_dt, ssm_b_re, ssm_b_im, ssm_c_re, ssm_c_im, ssm_d, ssm_w_glu,
           gmlp_ln_g, gmlp_ln_b, gmlp_w_s, gmlp_b_s, gmlp_w_proj,
           mem_ln_g, mem_ln_b, attn_w_kv, attn_w_proj,
           w_out, ln2_g, ln2_b, ffn2_w_gu, ffn2_w_down, ln3_g, ln3_b):
    row = lambda v: v.reshape(1, -1)
    whole = lambda a: ((0, a.shape[1]),)
    gate_up = ((0, D_FF), (D_FF, 2 * D_FF))
    tables, early = _ssm_tables(
        ssm_a_re, ssm_a_im, ssm_log_dt, ssm_b_re, ssm_b_im, ssm_c_re, ssm_c_im, ssm_d,
        [(ffn1_w_gu, 32, gate_up), (ffn1_w_down, GMLP_CHUNK, whole(ffn1_w_down))], T=SSM_CHUNK)
    ffn1 = _ffn_weight_set(early[0], early[1], early[2])
    wt_ssm = _ssm_weight_t(w_in)
    w_kv, w_s = attn_w_kv.astype(BF16), gmlp_w_s.astype(BF16)

    ffn_jobs = [[("w_in", w_in, 32, whole(w_in))]] + [[] for _ in groups[1:]]
    jobs = [("ffn2_down", ffn2_w_down, GMLP_CHUNK, whole(ffn2_w_down)),
            ("out", w_out, 32, whole(w_out)),
            ("glu", ssm_w_glu, 16, whole(ssm_w_glu)),
            ("aproj", attn_w_proj, 16, whole(attn_w_proj)),
            ("gproj", gmlp_w_proj, 16, whole(gmlp_w_proj))]
    ffn_jobs[-1 if len(groups) > 1 else 0].append(("ffn2_gu", ffn2_w_gu, 32, gate_up))
    bf16_w = {}

    def keep(job_list, converted):
        converted = iter(converted)
        for name, _, _, splits in job_list:
            bf16_w[name] = [next(converted) for _ in splits]

    stage1 = []
    for gi, (x, mem, n_unit_seqs) in enumerate(groups):
        mine = jobs[gi::len(groups)]
        x1, conv = _ffn_ln(x, ffn1, row(ln1_g), row(ln1_b), [job[1:] for job in ffn_jobs[gi]])
        keep(ffn_jobs[gi], conv)
        proj, x1b, conv = _in_proj(x1, bf16_w["w_in"][0], row(gmlp_ln_g), row(gmlp_ln_b),
                                   [job[1:] for job in mine])
        keep(mine, conv)
        stage1.append((x1, proj, x1b))
    ffn2 = _ffn_weight_set(bf16_w["ffn2_gu"][0], bf16_w["ffn2_gu"][1], bf16_w["ffn2_down"][0])
    w_glu, w_out_b = bf16_w["glu"][0], bf16_w["out"][0]
    w_gproj, w_aproj = bf16_w["gproj"][0], bf16_w["aproj"][0]

    outs = []
    for (x, mem, n_unit_seqs), (x1, proj, x1b) in zip(groups, stage1):
        kv = _kv_proj(mem, row(mem_ln_g), row(mem_ln_b), w_kv)
        ub, ob = _branches(proj, kv, w_s, gmlp_b_s)
        ya = _s5_mixer_gelu(x1b, wt_ssm, tables, ssm_a_re, ssm_a_im, ssm_log_dt,
                            n_unit_seqs=n_unit_seqs)
        merged = _mixer(ya, ub, ob, proj, w_glu, w_gproj, w_aproj)
        x2 = _outproj_ln(merged, w_out_b, x1, row(ln2_g), row(ln2_b))
        outs.append(_ffn_ln(x2, ffn2, row(ln3_g), row(ln3_b))[0])
    return outs


def kernel(x_prompt, x_sample, mem_prompt, mem_sample, ffn1_w_gu, ffn1_w_down, ln1_g, ln1_b, w_in,
           ssm_a_re, ssm_a_im, ssm_log_dt, ssm_b_re, ssm_b_im, ssm_c_re, ssm_c_im, ssm_d, ssm_w_glu,
           gmlp_ln_g, gmlp_ln_b, gmlp_w_s, gmlp_b_s, gmlp_w_proj, mem_ln_g, mem_ln_b, attn_w_kv,
           attn_w_proj, w_out, ln2_g, ln2_b, ffn2_w_gu, ffn2_w_down, ln3_g, ln3_b):
    d = x_prompt.shape[-1]
    assert x_prompt.shape[1] == SEQ_BLOCK and x_sample.shape[0] == 1
    assert x_sample.shape[1] % SEQ_BLOCK == 0
    xs = [x_prompt.reshape(-1, d), x_sample.reshape(-1, d)]
    mems = [mem_prompt.reshape(-1, d), mem_sample.reshape(-1, d)]
    n_unit = [x_prompt.shape[0], 0]
    for l in range(DEPTH):
        xs = _layer(list(zip(xs, mems, n_unit)),
                    ffn1_w_gu[l], ffn1_w_down[l], ln1_g[l], ln1_b[l], w_in[l],
                    ssm_a_re[l], ssm_a_im[l], ssm_log_dt[l], ssm_b_re[l], ssm_b_im[l],
                    ssm_c_re[l], ssm_c_im[l], ssm_d[l], ssm_w_glu[l],
                    gmlp_ln_g[l], gmlp_ln_b[l], gmlp_w_s[l], gmlp_b_s[l], gmlp_w_proj[l],
                    mem_ln_g[l], mem_ln_b[l], attn_w_kv[l], attn_w_proj[l],
                    w_out[l], ln2_g[l], ln2_b[l], ffn2_w_gu[l], ffn2_w_down[l], ln3_g[l], ln3_b[l])
    return (xs[0].reshape(x_prompt.shape), xs[1].reshape(x_sample.shape))
```

```python
import functools
import math

import jax
import jax.numpy as jnp
from jax import lax
from jax.experimental import pallas as pl
from jax.experimental.pallas import tpu as pltpu

F32 = jnp.float32
BF16 = jnp.bfloat16

D_MODEL = 2048
DEPTH = 1
SEQ_BLOCK = 4096
N_MEM = 256
SSM_WIDTH = D_MODEL // 2
SSM_GROUP = 16
SSM_GROUPS = SSM_WIDTH // SSM_GROUP
SSM_STATE = 64
GMLP_WIDTH = D_MODEL // 2
GMLP_CHUNK = 128
GMLP_HEADS = 8
GMLP_HEAD_DIM = GMLP_WIDTH // GMLP_HEADS
ATTN_HEADS = 4
ATTN_HEAD_DIM = D_MODEL // 8
ATTN_WIDTH = ATTN_HEADS * ATTN_HEAD_DIM
D_FF = 5504
ALPHA = (2.0 * DEPTH) ** 0.25
LN_EPS = 1e-5

V7X_LANES = 128
V7X_SUBLANES = 8
V7X_VMEM_BYTES = 64 * 1024 * 1024

SSM_CHUNK = 32
SSM_GROUPS_PER_STEP = 16
SSM_TABLE_GROUPS = 1
SSM_IN_POSITIONS = 4
SSM_TOK_CHUNKS = 128
FFN_TM = 512
FFN_TF = 1024
PROJ_TM = 1024
PROJ_TN = 1024
BRANCH_TM = 1024
MIX_TM = 1024
MIX_TN = 512
OUT_TM = 512
KV_TN = 512


def _vmem_limit(nbytes):
    return int(min(nbytes + (16 << 20), V7X_VMEM_BYTES - (4 << 20)))


def _layernorm_rows(y, g, b):
    mu = jnp.mean(y, axis=-1, keepdims=True)
    yc = y - mu
    var = jnp.mean(yc * yc, axis=-1, keepdims=True)
    return yc * lax.rsqrt(var + LN_EPS) * g + b


def _gelu_tanh(x):
    c = math.sqrt(2.0 / math.pi)
    return 0.5 * x * (1.0 + jnp.tanh(c * (x + 0.044715 * (x * x * x))))


def _sigmoid(x):
    return 1.0 / (1.0 + jnp.exp(-x))


def _dot(a, b):
    return jnp.dot(a, b, preferred_element_type=F32)


def _dot_nt(a, b):
    return lax.dot_general(a, b, (((1,), (1,)), ((), ())), preferred_element_type=F32)


def _swiglu_down(xb, wg, wu, wd):
    gate = _dot(xb, wg)
    up = _dot(xb, wu)
    act = (gate * _sigmoid(gate) * up).astype(BF16)
    return _dot(act, wd)


def _ffn_ln_step(j, nf, x_ref, wg_ref, wu_ref, wd_ref, wgt_ref, wut_ref, wdt_ref, g_ref, b_ref,
                 *refs, casts):
    n_src = len(casts)
    src_refs = refs[:n_src]
    o_ref = refs[n_src]
    dst_refs = refs[n_src + 1:-1]
    xb_ref = refs[-1]

    @pl.when(j == 0)
    def _():
        xb_ref[...] = x_ref[...].astype(BF16)
        o_ref[...] = jnp.zeros_like(o_ref)

    _cast_rows(src_refs, dst_refs, casts)
    o_ref[...] += _swiglu_down(xb_ref[...], wg_ref[...], wu_ref[...], wd_ref[...])

    @pl.when(j == nf - 1)
    def _():
        acc = o_ref[...] + _swiglu_down(xb_ref[...], wgt_ref[...], wut_ref[...], wdt_ref[...])
        y = ALPHA * x_ref[...] + 0.5 * acc
        o_ref[...] = _layernorm_rows(y, g_ref[...], b_ref[...])


def _ffn_ln(x, weights, ln_g, ln_b, cast_jobs=()):
    wg, wu, wd, wgt, wut, wdt = weights
    n, d = x.shape
    tm, tf = FFN_TM, FFN_TF
    nf = wd.shape[0] // tf
    ft = wdt.shape[0]
    once = dict(pipeline_mode=pl.Buffered(1))
    src_specs, dst_specs, dst_shapes, casts, cast_bytes = _cast_job_specs(
        cast_jobs, (n // tm) * nf, lambda i, j: i * nf + j)
    est = (2 * tm * d * 4 + 2 * tm * d * 4 + tm * d * 2
           + 2 * 3 * (d * tf * 2) + 3 * (d * ft * 2) + 3 * tm * tf * 4 + cast_bytes)
    in_specs = [
        pl.BlockSpec((tm, d), lambda i, j: (i, 0)),
        pl.BlockSpec((d, tf), lambda i, j: (0, j)),
        pl.BlockSpec((d, tf), lambda i, j: (0, j)),
        pl.BlockSpec((tf, d), lambda i, j: (j, 0)),
        pl.BlockSpec((d, ft), lambda i, j: (0, 0), **once),
        pl.BlockSpec((d, ft), lambda i, j: (0, 0), **once),
        pl.BlockSpec((ft, d), lambda i, j: (0, 0), **once),
        pl.BlockSpec((1, d), lambda i, j: (0, 0)),
        pl.BlockSpec((1, d), lambda i, j: (0, 0)),
    ] + src_specs
    out_specs = [pl.BlockSpec((tm, d), lambda i, j: (i, 0))] + dst_specs
    n_in, n_out = len(in_specs), len(out_specs)

    def whole_call(*refs):
        xb_ref, count_ref = refs[-2:]
        count_ref[0] = 0

        def step(*blocks):
            t = count_ref[0]
            count_ref[0] = t + 1
            _ffn_ln_step(lax.rem(t, nf), nf, *blocks, xb_ref, casts=casts)

        pltpu.emit_pipeline(step, grid=(n // tm, nf), in_specs=in_specs,
                            out_specs=out_specs)(*refs[:n_in + n_out])

    anywhere = pl.BlockSpec(memory_space=pl.ANY)
    res = pl.pallas_call(
        whole_call,
        in_specs=[anywhere] * n_in,
        out_specs=[anywhere] * n_out,
        out_shape=[jax.ShapeDtypeStruct((n, d), F32)] + dst_shapes,
        scratch_shapes=[pltpu.VMEM((tm, d), BF16), pltpu.SMEM((1,), jnp.int32)],
        compiler_params=pltpu.CompilerParams(vmem_limit_bytes=_vmem_limit(est)),
        name="ffn_ln",
    )(x, wg, wu, wd, wgt, wut, wdt, ln_g, ln_b, *[job[0] for job in cast_jobs])
    return res[0], res[1:]


def _ffn_weight_set(wg, wu, wd):
    full = (D_FF // FFN_TF) * FFN_TF
    return wg, wu, wd, wg[:, full:], wu[:, full:], wd[full:]


def _cast_job_specs(cast_jobs, steps, step_index):
    src_specs, dst_specs, dst_shapes, casts, nbytes = [], [], [], [], 0
    for src, rows, splits in cast_jobs:
        nblk = src.shape[0] // rows
        assert src.shape[0] % rows == 0 and nblk <= steps
        blk_map = lambda *idx, nblk=nblk: (jnp.minimum(step_index(*idx), nblk - 1), 0)
        src_specs.append(pl.BlockSpec((rows, src.shape[1]), blk_map))
        for lo, hi in splits:
            dst_specs.append(pl.BlockSpec((rows, hi - lo), blk_map))
            dst_shapes.append(jax.ShapeDtypeStruct((src.shape[0], hi - lo), BF16))
        casts.append(tuple(splits))
        nbytes += 2 * rows * src.shape[1] * 6
    return src_specs, dst_specs, dst_shapes, tuple(casts), nbytes


def _cast_rows(src_refs, dst_refs, casts):
    dsts = iter(dst_refs)
    for src, splits in zip(src_refs, casts):
        for lo, hi in splits:
            next(dsts)[...] = src[:, lo:hi].astype(BF16)


def _in_proj_kernel(x_ref, w_ref, lng_ref, lnb_ref, *refs, casts):
    n_src = len(casts)
    src_refs = refs[:n_src]
    o_ref, xb_ref = refs[n_src:n_src + 2]
    dst_refs = refs[n_src + 2:]
    j = pl.program_id(1)

    @pl.when(j == 0)
    def _():
        xb_ref[...] = x_ref[...].astype(BF16)

    def convert_weights():
        _cast_rows(src_refs, dst_refs, casts)

    @pl.when(j == 0)
    def _():
        convert_weights()
        o_ref[...] = _gelu_tanh(_dot(xb_ref[...], w_ref[...])).astype(o_ref.dtype)

    @pl.when(j == 1)
    def _():
        convert_weights()
        v = _gelu_tanh(_dot(xb_ref[...], w_ref[...]))
        o_ref[...] = _layernorm_rows(v, lng_ref[...], lnb_ref[...]).astype(o_ref.dtype)

    @pl.when(j >= 2)
    def _():
        convert_weights()
        o_ref[...] = _dot(xb_ref[...], w_ref[...]).astype(o_ref.dtype)


def _in_proj(x, w, gln_g, gln_b, cast_jobs):
    n, k = x.shape
    tm, tn = PROJ_TM, PROJ_TN
    assert tn == GMLP_WIDTH
    skip = SSM_WIDTH // tn
    m = w.shape[1] - SSM_WIDTH
    ncol = m // tn
    est = 2 * tm * k * 4 + 2 * tm * k * 2 + 2 * k * tn * 2 + 2 * tm * tn * 2 + 3 * tm * tn * 4
    src_specs, dst_specs, dst_shapes, casts, cast_bytes = _cast_job_specs(
        cast_jobs, (n // tm) * ncol, lambda i, j: i * ncol + j)
    est += cast_bytes
    res = pl.pallas_call(
        functools.partial(_in_proj_kernel, casts=casts),
        grid=(n // tm, ncol),
        in_specs=[pl.BlockSpec((tm, k), lambda i, j: (i, 0)),
                  pl.BlockSpec((k, tn), lambda i, j: (0, skip + j)),
                  pl.BlockSpec((1, tn), lambda i, j: (0, 0)),
                  pl.BlockSpec((1, tn), lambda i, j: (0, 0))] + src_specs,
        out_specs=[pl.BlockSpec((tm, tn), lambda i, j: (i, j)),
                   pl.BlockSpec((tm, k), lambda i, j: (i, 0))] + dst_specs,
        out_shape=[jax.ShapeDtypeStruct((n, m), BF16),
                   jax.ShapeDtypeStruct((n, k), BF16)] + dst_shapes,
        compiler_params=pltpu.CompilerParams(
            dimension_semantics=("arbitrary", "arbitrary"),
            vmem_limit_bytes=_vmem_limit(est)),
        name="in_proj",
    )(x, w, gln_g, gln_b, *[job[0] for job in cast_jobs])
    return res[0], res[1], res[2:]


def _branches_kernel(u_ref, v_ref, q_ref, kv_ref, ws_ref, bs_ref, ub_ref, ob_ref, *, tm):
    hd = GMLP_HEAD_DIM

    def chunk_body(ci, carry):
        rows = pl.ds(pl.multiple_of(ci * GMLP_CHUNK, GMLP_CHUNK), GMLP_CHUNK)
        for h in range(GMLP_HEADS):
            cols = slice(h * hd, (h + 1) * hd)
            mixed = _dot(ws_ref[h], v_ref[rows, cols]) + bs_ref[:, h:h + 1]
            ub_ref[rows, cols] = (u_ref[rows, cols].astype(F32) * mixed).astype(BF16)
        return carry

    lax.fori_loop(0, tm // GMLP_CHUNK, chunk_body, 0)

    ad = ATTN_HEAD_DIM
    scale = ad ** -0.5
    for h in range(ATTN_HEADS):
        cols = slice(h * ad, (h + 1) * ad)
        vcols = slice(ATTN_WIDTH + h * ad, ATTN_WIDTH + (h + 1) * ad)
        s = _dot_nt(q_ref[:, cols], kv_ref[:, cols]) * scale
        p = jnp.exp(s - jnp.max(s, axis=-1, keepdims=True))
        p = p * (1.0 / jnp.sum(p, axis=-1, keepdims=True))
        ob_ref[:, cols] = _dot(p.astype(BF16), kv_ref[:, vcols]).astype(BF16)


def _branches(proj, kv, w_s, b_s):
    n = proj.shape[0]
    tm = BRANCH_TM
    wid = GMLP_WIDTH
    assert wid == ATTN_WIDTH
    blocks_per_seq = SEQ_BLOCK // tm
    n_mem_batches = kv.shape[0] // N_MEM
    col = lambda c: pl.BlockSpec((tm, wid), lambda i: (i, c))
    full2 = lambda a: pl.BlockSpec(a.shape, lambda i: (0, 0))
    out = pl.BlockSpec((tm, wid), lambda i: (i, 0))
    return pl.pallas_call(
        functools.partial(_branches_kernel, tm=tm),
        grid=(n // tm,),
        in_specs=[col(0), col(1), col(2),
                  pl.BlockSpec((N_MEM, 2 * ATTN_WIDTH),
                               lambda i: (jnp.minimum(i // blocks_per_seq, n_mem_batches - 1), 0)),
                  pl.BlockSpec(w_s.shape, lambda i: (0, 0, 0)),
                  full2(b_s)],
        out_specs=[out, out],
        out_shape=[jax.ShapeDtypeStruct((n, wid), BF16), jax.ShapeDtypeStruct((n, wid), BF16)],
        compiler_params=pltpu.CompilerParams(dimension_semantics=("parallel",)),
        name="branches",
    )(proj, proj, proj, kv, w_s, b_s)


def _ssm_weight_t_kernel(w_ref, o_ref):
    o_ref[...] = w_ref[...].T.astype(o_ref.dtype)


def _ssm_weight_t(w_in):
    d = w_in.shape[0]
    tn = 2 * V7X_LANES
    return pl.pallas_call(
        _ssm_weight_t_kernel,
        grid=(SSM_WIDTH // tn,),
        in_specs=[pl.BlockSpec((d, tn), lambda j: (0, j))],
        out_specs=pl.BlockSpec((tn, d), lambda j: (j, 0)),
        out_shape=jax.ShapeDtypeStruct((SSM_WIDTH, d), BF16),
        compiler_params=pltpu.CompilerParams(dimension_semantics=("parallel",)),
        name="ssm_weight_t",
    )(w_in)


def _ssm_in_kernel(x_ref, wt_ref, o_ref, *, rb):
    d = wt_ref.shape[1]
    for k in range(rb):
        ut = _dot_nt(wt_ref[...], x_ref[:, k * d:(k + 1) * d])
        o_ref[:, k] = ut.astype(BF16).reshape(o_ref.shape[0], o_ref.shape[2], o_ref.shape[3])


def _ssm_in(xc, wt, *, T):
    nc = xc.shape[0]
    d = xc.shape[1] // T
    G, C = SSM_GROUPS, SSM_GROUP
    rb = SSM_IN_POSITIONS
    est = 2 * nc * rb * d * 2 + 2 * G * C * d * 2 + 2 * rb * G * C * nc * 2 + 2 * G * C * nc * 4
    return pl.pallas_call(
        functools.partial(_ssm_in_kernel, rb=rb),
        grid=(T // rb,),
        in_specs=[pl.BlockSpec((nc, rb * d), lambda r: (0, r)),
                  pl.BlockSpec((G * C, d), lambda r: (0, 0))],
        out_specs=pl.BlockSpec((G, rb, C, nc), lambda r: (0, r, 0, 0)),
        out_shape=jax.ShapeDtypeStruct((G, T, C, nc), BF16),
        compiler_params=pltpu.CompilerParams(
            dimension_semantics=("parallel",), vmem_limit_bytes=_vmem_limit(est)),
        name="ssm_in",
    )(xc, wt)


def _cmul(x, y):
    return x[0] * y[0] - x[1] * y[1], x[0] * y[1] + x[1] * y[0]


def _split_hi_lo(x):
    hi = x.astype(BF16)
    lo = (x - hi.astype(F32)).astype(BF16)
    return hi, lo


def _dot_hi_lo(a, b):
    ah, al = _split_hi_lo(a)
    bh, bl = _split_hi_lo(b)
    return _dot(ah, bh) + _dot(al, bh) + _dot(ah, bl)


N_TABLE_INPUTS = 2
N_TABLE_OUTPUTS = 3


def _ssm_tables_kernel(*refs, T, gb, casts):
    n_src = len(casts)
    n_dst = sum(len(splits) for splits in casts)
    ins = refs[:N_TABLE_INPUTS]
    srcs = refs[N_TABLE_INPUTS:N_TABLE_INPUTS + n_src]
    outs = refs[N_TABLE_INPUTS + n_src:N_TABLE_INPUTS + n_src + N_TABLE_OUTPUTS]
    dsts = refs[N_TABLE_INPUTS + n_src + N_TABLE_OUTPUTS:][:n_dst]
    q_ref = refs[-1]
    _cast_rows(srcs, dsts, casts)
    for gi in range(gb):
        _ssm_tables_group(*[r.at[gi] for r in ins + outs + (q_ref,)], T=T)


def _ssm_tables_group(col_ref, row_ref, mt_ref, gm_ref, cs_ref, q_ref, *, T):
    P, C = SSM_STATE, SSM_GROUP
    rpt = V7X_LANES // C
    nt = T // rpt
    wide = 2 * T * C

    lane = lax.broadcasted_iota(jnp.int32, (1, V7X_LANES), 1)
    rr = lax.shift_right_logical(lane, 4)
    expand = (lax.broadcasted_iota(jnp.int32, (C, V7X_LANES), 0)
              == (lax.broadcasted_iota(jnp.int32, (C, V7X_LANES), 1) & (C - 1))).astype(F32)

    pc = col_ref[0:P, C:C + 8]

    lane4 = lax.broadcasted_iota(jnp.int32, (1, 4 * P), 1)
    is_re = (lax.shift_right_logical(lane4, 6) & 1) == 0
    is_f = lane4 < 2 * P
    pr = row_ref[0:8]
    dt4 = jnp.exp(pr[2:3])
    zr4, zi4 = pr[0:1] * dt4, pr[1:2] * dt4

    kp = ((T + 1 + 7) // 8) * 8
    krow = lax.broadcasted_iota(jnp.int32, (kp, 4 * P), 0).astype(F32)
    mag = jnp.exp(krow * zr4)
    pw_r, pw_i = mag * jnp.cos(krow * zi4), mag * jnp.sin(krow * zi4)

    def states_on_rows(tab, lo):
        slab = tab[:, lo:lo + V7X_LANES]
        padded = jnp.concatenate([slab, jnp.zeros((V7X_LANES - kp, V7X_LANES), F32)], axis=0)
        return padded.T[0:P]

    pt_f = states_on_rows(pw_r, 0), states_on_rows(pw_i, 0)
    pt_b = states_on_rows(pw_r, 2 * P), states_on_rows(pw_i, 2 * P)

    def stair(pt, descending):
        re = jnp.zeros((P, V7X_LANES), F32)
        im = re
        for k in range(rpt):
            m = rpt - 1 - k if descending else k
            re = jnp.where(rr == k, pt[0][:, m:m + 1], re)
            im = jnp.where(rr == k, pt[1][:, m:m + 1], im)
        return re, im

    def zoh(d, pt):
        are = pc[:, d:d + 1]
        aim = pc[:, 2 + d:3 + d]
        nr = pt[0][:, 1:2] - 1.0
        ni = pt[1][:, 1:2]
        den = are * are + aim * aim
        return (nr * are + ni * aim) / den, (ni * are - nr * aim) / den

    def col(pt, m):
        return pt[0][:, m:m + 1], pt[1][:, m:m + 1]

    dsc_f = stair(pt_f, True)
    asc_b = stair(pt_b, False)
    btile = jnp.dot(col_ref[:, 0:C], expand, precision=lax.Precision.HIGHEST,
                    preferred_element_type=F32)
    bt = (btile[0:P], btile[P:2 * P])
    bb_f = _cmul(zoh(0, pt_f), bt)
    bd_f = _cmul(dsc_f, bb_f)
    ba_b = _cmul(asc_b, _cmul(zoh(1, pt_b), bt))

    for j in range(nt):
        cols = slice(j * V7X_LANES, (j + 1) * V7X_LANES)
        xf = _cmul(bd_f, col(pt_f, T - rpt - rpt * j))
        xb = _cmul(ba_b, col(pt_b, rpt * j))
        gm_ref[0:P, cols] = xf[0].astype(BF16)
        gm_ref[P:2 * P, cols] = xf[1].astype(BF16)
        gm_ref[2 * P:3 * P, cols] = xb[0].astype(BF16)
        gm_ref[3 * P:4 * P, cols] = xb[1].astype(BF16)

    zeros = jnp.zeros((2 * P, V7X_LANES), F32)
    for j in range(2 * nt):
        cols = slice(j * V7X_LANES, (j + 1) * V7X_LANES)
        if j < nt:
            qf = _cmul(bd_f, col(pt_f, T - rpt * j - (rpt - 1)))
            q_ref[0:P, cols] = qf[0]
            q_ref[P:2 * P, cols] = qf[1]
            q_ref[2 * P:4 * P, cols] = zeros
        else:
            qb = _cmul(ba_b, col(pt_b, rpt * j - T))
            q_ref[2 * P:3 * P, cols] = qb[0]
            q_ref[3 * P:4 * P, cols] = qb[1]
            if j == nt:
                center = rr == 0
                q_ref[0:P, cols] = jnp.where(center, bb_f[0], 0.0)
                q_ref[P:2 * P, cols] = jnp.where(center, bb_f[1], 0.0)
            else:
                q_ref[0:2 * P, cols] = zeros

    cre = row_ref[8:8 + C]
    cim = row_ref[8 + C:8 + 2 * C]

    for r in range(T):
        prr = jnp.where(is_f, pw_r[r + 1:r + 2], pw_r[T - r:T - r + 1])
        pii = jnp.where(is_f, pw_i[r + 1:r + 2], pw_i[T - r:T - r + 1])
        blk = jnp.where(is_re, cre * prr - cim * pii, -(cre * pii + cim * prr))
        cs_ref[r * C:(r + 1) * C, :] = blk.astype(BF16)

    lhs = jnp.where(is_re, cre, -cim)
    zt = _dot_hi_lo(lhs, q_ref[...])
    lanew = lax.broadcasted_iota(jnp.int32, (C, wide), 1)
    roww = lax.broadcasted_iota(jnp.int32, (C, wide), 0)
    diag = (lax.shift_right_logical(lanew, 4) == T) & ((lanew & (C - 1)) == roww)
    d0 = 8 + 2 * C
    dtile = jnp.concatenate([row_ref[d0 + k:d0 + k + 1] for k in range(wide // (4 * P))], axis=1)
    zt = zt + jnp.where(diag, dtile, 0.0)

    for r in range(T):
        off = (T - r) * C
        shifted = pltpu.roll(zt, (wide - off) % wide, 1)
        mt_ref[r * C:(r + 1) * C, :] = shifted[:, :T * C].astype(BF16)


def _ssm_tables(a_re, a_im, log_dt, b_re, b_im, c_re, c_im, d_skip, cast_jobs, *, T):
    G, P, C = SSM_GROUPS, SSM_STATE, SSM_GROUP
    ldt = jnp.broadcast_to(log_dt[:, :, None], (2, G, P))
    zc = jnp.zeros((G, P), F32)
    pcol = jnp.stack([a_re[0], a_re[1], a_im[0], a_im[1], ldt[0], ldt[1], zc, zc], axis=-1)

    def row4(x):
        return jnp.concatenate([x[0], x[0], x[1], x[1]], axis=-1)

    zr4 = jnp.zeros((G, 4 * P), F32)
    prow = jnp.stack([row4(a_re), row4(a_im), row4(ldt), zr4, zr4, zr4, zr4, zr4], axis=1)
    bcat = jnp.concatenate([b_re, b_im], axis=1)
    dtile = jnp.tile(d_skip, (1, 2 * T)).reshape(G, -1, 4 * P)
    colpack = jnp.concatenate([bcat, jnp.pad(pcol, ((0, 0), (0, P), (0, 0)))], axis=-1)
    rowpack = jnp.concatenate([prow, row4(c_re), row4(c_im), dtile], axis=1)

    tc = T * C
    gb = SSM_TABLE_GROUPS
    blk = lambda *shape: pl.BlockSpec((gb,) + shape, lambda g: (g,) + (0,) * len(shape))
    src_specs, dst_specs, dst_shapes, casts, cast_bytes = _cast_job_specs(
        cast_jobs, G // gb, lambda g: g)
    est = 2 * gb * (2 * tc * tc * 2 + 3 * 4 * P * tc * 4) + cast_bytes
    res = pl.pallas_call(
        functools.partial(_ssm_tables_kernel, T=T, gb=gb, casts=casts),
        grid=(G // gb,),
        in_specs=[blk(*colpack.shape[1:]), blk(*rowpack.shape[1:])] + src_specs,
        out_specs=[blk(tc, tc), blk(4 * P, tc), blk(tc, 4 * P)] + dst_specs,
        out_shape=[
            jax.ShapeDtypeStruct((G, tc, tc), BF16),
            jax.ShapeDtypeStruct((G, 4 * P, tc), BF16),
            jax.ShapeDtypeStruct((G, tc, 4 * P), BF16),
        ] + dst_shapes,
        scratch_shapes=[pltpu.VMEM((gb, 4 * P, 2 * tc), F32)],
        compiler_params=pltpu.CompilerParams(
            dimension_semantics=("arbitrary",), vmem_limit_bytes=_vmem_limit(est)),
        name="ssm_tables",
    )(colpack, rowpack, *[job[0] for job in cast_jobs])
    return tuple(res[:N_TABLE_OUTPUTS]), res[N_TABLE_OUTPUTS:]


def _ssm_state_kernel(u_ref, gm_ref, o_ref, *, gb, ns):
    nc = u_ref.shape[-1]
    for gi in range(gb):
        u = u_ref[gi].reshape(-1, nc)
        o_ref[:, gi * ns:(gi + 1) * ns] = _dot(gm_ref[gi], u).T


def _ssm_state(ut, gm):
    G, T, C, nc = ut.shape
    ns = gm.shape[1]
    gb = SSM_GROUPS_PER_STEP
    return pl.pallas_call(
        functools.partial(_ssm_state_kernel, gb=gb, ns=ns),
        grid=(G // gb,),
        in_specs=[pl.BlockSpec((gb, T, C, nc), lambda s: (s, 0, 0, 0)),
                  pl.BlockSpec((gb, ns, T * C), lambda s: (s, 0, 0))],
        out_specs=pl.BlockSpec((nc, gb * ns), lambda s: (0, s)),
        out_shape=jax.ShapeDtypeStruct((nc, G * ns), F32),
        compiler_params=pltpu.CompilerParams(dimension_semantics=("parallel",)),
        name="ssm_state",
    )(ut, gm)


def _ssm_scan_kernel(are_ref, aim_ref, ldt_ref, s_ref, o_ref, x_ref, xs_ref, *, T, n0, seq_starts, seq_ends):
    d = pl.program_id(0)
    j = pl.program_id(1)
    nblk = pl.num_programs(1)
    blk = j + d * (nblk - 1 - 2 * j)
    P = SSM_STATE

    is_start = functools.reduce(jnp.logical_or, [blk == s for s in seq_starts])
    is_end = functools.reduce(jnp.logical_or, [blk == e for e in seq_ends])
    reset = jnp.where(d == 0, is_start, is_end)

    @pl.when(reset)
    def _():
        x_ref[...] = jnp.zeros_like(x_ref)
        xs_ref[...] = jnp.zeros_like(xs_ref)

    dt = jnp.exp(ldt_ref[...])
    zr = are_ref[...] * dt
    zi = aim_ref[...] * dt
    mag = jnp.exp(float(T) * zr)
    mr = mag * jnp.cos(float(T) * zi)
    mi = mag * jnp.sin(float(T) * zi)
    lane = lax.broadcasted_iota(jnp.int32, mr.shape, 1)
    m2 = jnp.where(lane < P, -mi, mi)
    m2s = -m2

    def body(k, carry):
        x, xs = carry
        row = k + d * (n0 - 1 - 2 * k)
        loc = s_ref[row]
        o_ref[row] = x.astype(o_ref.dtype)
        locs = pltpu.roll(loc, P, 1)
        return x * mr + xs * m2 + loc, xs * mr + x * m2s + locs

    x, xs = lax.fori_loop(0, n0, body, (x_ref[...], xs_ref[...]), unroll=4)
    x_ref[...] = x
    xs_ref[...] = xs


def _ssm_scan(s_loc, a_re, a_im, log_dt, *, T, n_unit_seqs):
    G, P = SSM_GROUPS, SSM_STATE
    nc = s_loc.shape[0]
    n0 = SEQ_BLOCK // T
    nblk = nc // n0
    seq_starts = tuple(range(n_unit_seqs + 1))
    seq_ends = tuple(range(n_unit_seqs)) + (nblk - 1,)
    dup = lambda x: jnp.concatenate([x, x], axis=-1)
    are2, aim2 = dup(a_re), dup(a_im)
    ldt2 = jnp.broadcast_to(log_dt[:, :, None], (2, G, 2 * P))

    def blk_map(d, j):
        return (j + d * (nblk - 1 - 2 * j), 0, d)

    par = pl.BlockSpec((None, G, 2 * P), lambda d, j: (d, 0, 0))
    return pl.pallas_call(
        functools.partial(_ssm_scan_kernel, T=T, n0=n0, seq_starts=seq_starts, seq_ends=seq_ends),
        grid=(2, nblk),
        in_specs=[par, par, par, pl.BlockSpec((n0, G, 2 * P), blk_map)],
        out_specs=pl.BlockSpec((n0, G, 2 * P), blk_map),
        out_shape=jax.ShapeDtypeStruct((nc, G, 4 * P), BF16),
        scratch_shapes=[pltpu.VMEM((G, 2 * P), F32), pltpu.VMEM((G, 2 * P), F32)],
        compiler_params=pltpu.CompilerParams(dimension_semantics=("arbitrary", "arbitrary")),
        name="ssm_scan",
    )(are2, aim2, ldt2, s_loc)


def _ssm_out_kernel(u_ref, mt_ref, s_ref, cs_ref, y_ref, *, gb, ns):
    nc = u_ref.shape[-1]
    for gi in range(gb):
        u = u_ref[gi].reshape(-1, nc)
        y = _dot(mt_ref[gi], u) + _dot_nt(cs_ref[gi], s_ref[:, gi * ns:(gi + 1) * ns])
        y_ref[gi] = _gelu_tanh(y).astype(y_ref.dtype).reshape(y_ref.shape[1:])


def _ssm_out(ut, mt, s_in, cs):
    G, T, C, nc = ut.shape
    ns = cs.shape[2]
    gb = SSM_GROUPS_PER_STEP
    return pl.pallas_call(
        functools.partial(_ssm_out_kernel, gb=gb, ns=ns),
        grid=(G // gb,),
        in_specs=[pl.BlockSpec((gb, T, C, nc), lambda s: (s, 0, 0, 0)),
                  pl.BlockSpec((gb, T * C, T * C), lambda s: (s, 0, 0)),
                  pl.BlockSpec((nc, gb * ns), lambda s: (0, s)),
                  pl.BlockSpec((gb, T * C, ns), lambda s: (s, 0, 0))],
        out_specs=pl.BlockSpec((gb, T, C, nc), lambda s: (s, 0, 0, 0)),
        out_shape=jax.ShapeDtypeStruct((G, T, C, nc), BF16),
        compiler_params=pltpu.CompilerParams(dimension_semantics=("parallel",)),
        name="ssm_out",
    )(ut, mt, s_in, cs)


def _ssm_tok_kernel(y_ref, o_ref, stage_ref):
    T, ncb = y_ref.shape[1], y_ref.shape[-1]
    R = stage_ref.shape[2]
    row = lax.broadcasted_iota(jnp.int32, (R * ncb, R * ncb), 0)
    col = lax.broadcasted_iota(jnp.int32, (R * ncb, R * ncb), 1)
    sel = ((row // R == col % ncb) & (row % R == col // ncb)).astype(BF16)
    for k in range(T // R):
        ycat = jnp.concatenate([y_ref[:, k * R + rr].reshape(-1, ncb) for rr in range(R)], axis=1)
        tok = _dot_nt(sel, ycat)
        stage_ref[:, k] = tok.reshape(ncb, R, tok.shape[-1])
    o_ref[...] = stage_ref[...].reshape(o_ref.shape).astype(o_ref.dtype)


def _ssm_tok(yt):
    G, T, C, nc = yt.shape
    ncb = SSM_TOK_CHUNKS
    est = 2 * G * T * C * ncb * 2 + ncb * T * G * C * 4 + 2 * ncb * T * G * C * 2
    return pl.pallas_call(
        _ssm_tok_kernel,
        grid=(nc // ncb,),
        in_specs=[pl.BlockSpec((G, T, C, ncb), lambda s: (0, 0, 0, s))],
        out_specs=pl.BlockSpec((ncb * T, G * C), lambda s: (s, 0)),
        out_shape=jax.ShapeDtypeStruct((nc * T, G * C), BF16),
        scratch_shapes=[pltpu.VMEM((ncb, T // V7X_SUBLANES, V7X_SUBLANES, G * C), F32)],
        compiler_params=pltpu.CompilerParams(
            dimension_semantics=("parallel",), vmem_limit_bytes=_vmem_limit(est)),
        name="ssm_tok",
    )(yt)


def _s5_mixer_gelu(xb, wt, tables, a_re, a_im, log_dt, *, n_unit_seqs):
    T, G, P = SSM_CHUNK, SSM_GROUPS, SSM_STATE
    nc = xb.shape[0] // T
    mt, gm, cs = tables
    ut = _ssm_in(xb.reshape(nc, T * xb.shape[1]), wt, T=T)
    s_loc = _ssm_state(ut, gm)
    s_in = _ssm_scan(s_loc.reshape(nc, G, 4 * P), a_re, a_im, log_dt, T=T, n_unit_seqs=n_unit_seqs)
    yt = _ssm_out(ut, mt, s_in.reshape(nc, G * 4 * P), cs)
    return _ssm_tok(yt)


def _kv_kernel(m_ref, g_ref, b_ref, w_ref, o_ref, mb_ref):
    @pl.when(pl.program_id(0) == 0)
    def _():
        mb_ref[...] = _layernorm_rows(m_ref[...], g_ref[...], b_ref[...]).astype(BF16)

    o_ref[...] = _dot(mb_ref[...], w_ref[...]).astype(o_ref.dtype)


def _kv_proj(mem, ln_g, ln_b, w_kv):
    n, d = mem.shape
    m = w_kv.shape[1]
    tn = KV_TN
    est = 2 * n * d * 4 + n * d * 2 + 2 * d * tn * 2 + 2 * n * tn * 2 + 3 * n * d * 4
    return pl.pallas_call(
        _kv_kernel,
        grid=(m // tn,),
        in_specs=[pl.BlockSpec((n, d), lambda j: (0, 0)),
                  pl.BlockSpec((1, d), lambda j: (0, 0)),
                  pl.BlockSpec((1, d), lambda j: (0, 0)),
                  pl.BlockSpec((d, tn), lambda j: (0, j))],
        out_specs=pl.BlockSpec((n, tn), lambda j: (0, j)),
        out_shape=jax.ShapeDtypeStruct((n, m), BF16),
        scratch_shapes=[pltpu.VMEM((n, d), BF16)],
        compiler_params=pltpu.CompilerParams(
            dimension_semantics=("arbitrary",), vmem_limit_bytes=_vmem_limit(est)),
        name="kv_proj",
    )(mem, ln_g, ln_b, w_kv)


def _mixer_kernel(ya_ref, ub_ref, ob_ref, g0_ref, g1_ref, g2_ref,
                  wa_ref, wb_ref, wgp_ref, wap_ref, o_ref):
    ya = ya_ref[...]
    y_a = _dot(ya, wa_ref[...]) * _sigmoid(_dot(ya, wb_ref[...]))
    y_b = _dot(ub_ref[...], wgp_ref[...])
    y_c = _dot(ob_ref[...], wap_ref[...])
    merged = (_sigmoid(g0_ref[...].astype(F32)) * y_a
              + _sigmoid(g1_ref[...].astype(F32)) * y_b
              + _sigmoid(g2_ref[...].astype(F32)) * y_c)
    o_ref[...] = merged.astype(o_ref.dtype)


def _mixer(ya, ub, ob, proj, w_glu, w_gproj, w_aproj):
    n, wid = ya.shape
    d = D_MODEL
    tm, tn = MIX_TM, MIX_TN
    nn = d // tn
    gate0 = (proj.shape[1] - 3 * d) // tn
    gate_spec = lambda br: pl.BlockSpec((tm, tn), lambda i, j: (i, gate0 + br * nn + j))
    act_spec = pl.BlockSpec((tm, wid), lambda i, j: (i, 0))
    est = (2 * 3 * tm * wid * 2 + 2 * 3 * tm * tn * 2 + 2 * 4 * wid * tn * 2 + 2 * tm * tn * 2
           + 8 * tm * tn * 4)
    return pl.pallas_call(
        _mixer_kernel,
        grid=(n // tm, nn),
        in_specs=[
            act_spec, act_spec, act_spec,
            gate_spec(0), gate_spec(1), gate_spec(2),
            pl.BlockSpec((wid, tn), lambda i, j: (0, j)),
            pl.BlockSpec((wid, tn), lambda i, j: (0, nn + j)),
            pl.BlockSpec((wid, tn), lambda i, j: (0, j)),
            pl.BlockSpec((wid, tn), lambda i, j: (0, j)),
        ],
        out_specs=pl.BlockSpec((tm, tn), lambda i, j: (i, j)),
        out_shape=jax.ShapeDtypeStruct((n, d), BF16),
        compiler_params=pltpu.CompilerParams(
            dimension_semantics=("parallel", "parallel"),
            vmem_limit_bytes=_vmem_limit(est)),
        name="mixer",
    )(ya, ub, ob, proj, proj, proj, w_glu, w_glu, w_gproj, w_aproj)


def _outproj_ln_kernel(m_ref, w_ref, x_ref, g_ref, b_ref, o_ref):
    y = ALPHA * x_ref[...] + _dot(m_ref[...], w_ref[...])
    o_ref[...] = _layernorm_rows(y, g_ref[...], b_ref[...])


def _outproj_ln(merged, w_out, x, ln_g, ln_b):
    n, d = x.shape
    tm = OUT_TM
    est = 2 * tm * d * 2 + 2 * d * d * 2 + 4 * tm * d * 4 + 2 * tm * d * 4
    return pl.pallas_call(
        _outproj_ln_kernel,
        grid=(n // tm,),
        in_specs=[pl.BlockSpec((tm, d), lambda i: (i, 0)),
                  pl.BlockSpec((d, d), lambda i: (0, 0)),
                  pl.BlockSpec((tm, d), lambda i: (i, 0)),
                  pl.BlockSpec((1, d), lambda i: (0, 0)),
                  pl.BlockSpec((1, d), lambda i: (0, 0))],
        out_specs=pl.BlockSpec((tm, d), lambda i: (i, 0)),
        out_shape=jax.ShapeDtypeStruct((n, d), F32),
        compiler_params=pltpu.CompilerParams(
            dimension_semantics=("parallel",), vmem_limit_bytes=_vmem_limit(est)),
        name="outproj_ln",
    )(merged, w_out, x, ln_g, ln_b)


def _layer(groups, ffn1_w_gu, ffn1_w_down, ln1_g, ln1_b, w_in,
           ssm_a_re, ssm_a_im, ssm_log_dt, ssm_b_re, ssm_b_im, ssm_c_re, ssm_c_im, ssm_d, ssm_w_glu,
           gmlp_ln_g, gmlp_ln_b, gmlp_w_s, gmlp_b_s, gmlp_w_proj,
           mem_ln_g, mem_ln_b, attn_w_kv, attn_w_proj,
           w_out, ln2_g, ln2_b, ffn2_w_gu, ffn2_w_down, ln3_g, ln3_b):
    row = lambda v: v.reshape(1, -1)
    whole = lambda a: ((0, a.shape[1]),)
    gate_up = ((0, D_FF), (D_FF, 2 * D_FF))
    tables, early = _ssm_tables(
        ssm_a_re, ssm_a_im, ssm_log_dt, ssm_b_re, ssm_b_im, ssm_c_re, ssm_c_im, ssm_d,
        [(ffn1_w_gu, 32, gate_up), (ffn1_w_down, GMLP_CHUNK, whole(ffn1_w_down))], T=SSM_CHUNK)
    ffn1 = _ffn_weight_set(early[0], early[1], early[2])
    wt_ssm = _ssm_weight_t(w_in)
    w_kv, w_s = attn_w_kv.astype(BF16), gmlp_w_s.astype(BF16)

    ffn_jobs = [[("w_in", w_in, 32, whole(w_in))]] + [[] for _ in groups[1:]]
    jobs = [("ffn2_down", ffn2_w_down, GMLP_CHUNK, whole(ffn2_w_down)),
            ("out", w_out, 32, whole(w_out)),
            ("glu", ssm_w_glu, 16, whole(ssm_w_glu)),
            ("aproj", attn_w_proj, 16, whole(attn_w_proj)),
            ("gproj", gmlp_w_proj, 16, whole(gmlp_w_proj))]
    ffn_jobs[-1 if len(groups) > 1 else 0].append(("ffn2_gu", ffn2_w_gu, 32, gate_up))
    bf16_w = {}

    def keep(job_list, converted):
        converted = iter(converted)
        for name, _, _, splits in job_list:
            bf16_w[name] = [next(converted) for _ in splits]

    stage1 = []
    for gi, (x, mem, n_unit_seqs) in enumerate(groups):
        mine = jobs[gi::len(groups)]
        x1, conv = _ffn_ln(x, ffn1, row(ln1_g), row(ln1_b), [job[1:] for job in ffn_jobs[gi]])
        keep(ffn_jobs[gi], conv)
        proj, x1b, conv = _in_proj(x1, bf16_w["w_in"][0], row(gmlp_ln_g), row(gmlp_ln_b),
                                   [job[1:] for job in mine])
        keep(mine, conv)
        stage1.append((x1, proj, x1b))
    ffn2 = _ffn_weight_set(bf16_w["ffn2_gu"][0], bf16_w["ffn2_gu"][1], bf16_w["ffn2_down"][0])
    w_glu, w_out_b = bf16_w["glu"][0], bf16_w["out"][0]
    w_gproj, w_aproj = bf16_w["gproj"][0], bf16_w["aproj"][0]

    outs = []
    for (x, mem, n_unit_seqs), (x1, proj, x1b) in zip(groups, stage1):
        kv = _kv_proj(mem, row(mem_ln_g), row(mem_ln_b), w_kv)
        ub, ob = _branches(proj, kv, w_s, gmlp_b_s)
        ya = _s5_mixer_gelu(x1b, wt_ssm, tables, ssm_a_re, ssm_a_im, ssm_log_dt,
                            n_unit_seqs=n_unit_seqs)
        merged = _mixer(ya, ub, ob, proj, w_glu, w_gproj, w_aproj)
        x2 = _outproj_ln(merged, w_out_b, x1, row(ln2_g), row(ln2_b))
        outs.append(_ffn_ln(x2, ffn2, row(ln3_g), row(ln3_b))[0])
    return outs


def kernel(x_prompt, x_sample, mem_prompt, mem_sample, ffn1_w_gu, ffn1_w_down, ln1_g, ln1_b, w_in,
           ssm_a_re, ssm_a_im, ssm_log_dt, ssm_b_re, ssm_b_im, ssm_c_re, ssm_c_im, ssm_d, ssm_w_glu,
           gmlp_ln_g, gmlp_ln_b, gmlp_w_s, gmlp_b_s, gmlp_w_proj, mem_ln_g, mem_ln_b, attn_w_kv,
           attn_w_proj, w_out, ln2_g, ln2_b, ffn2_w_gu, ffn2_w_down, ln3_g, ln3_b):
    d = x_prompt.shape[-1]
    assert x_prompt.shape[1] == SEQ_BLOCK and x_sample.shape[0] == 1
    assert x_sample.shape[1] % SEQ_BLOCK == 0
    xs = [x_prompt.reshape(-1, d), x_sample.reshape(-1, d)]
    mems = [mem_prompt.reshape(-1, d), mem_sample.reshape(-1, d)]
    n_unit = [x_prompt.shape[0], 0]
    for l in range(DEPTH):
        xs = _layer(list(zip(xs, mems, n_unit)),
                    ffn1_w_gu[l], ffn1_w_down[l], ln1_g[l], ln1_b[l], w_in[l],
                    ssm_a_re[l], ssm_a_im[l], ssm_log_dt[l], ssm_b_re[l], ssm_b_im[l],
                    ssm_c_re[l], ssm_c_im[l], ssm_d[l], ssm_w_glu[l],
                    gmlp_ln_g[l], gmlp_ln_b[l], gmlp_w_s[l], gmlp_b_s[l], gmlp_w_proj[l],
                    mem_ln_g[l], mem_ln_b[l], attn_w_kv[l], attn_w_proj[l],
                    w_out[l], ln2_g[l], ln2_b[l], ffn2_w_gu[l], ffn2_w_down[l], ln3_g[l], ln3_b[l])
    return (xs[0].reshape(x_prompt.shape), xs[1].reshape(x_sample.shape))
```
